```python
import math
import jax, jax.numpy as jnp
from jax import lax
import numpy as np

D_MODEL = 1024
BATCH = 8
SEQ = 4096
DEPTH = 1

D_SSM = D_MODEL
SSM_GROUP = 16
N_GROUPS = D_SSM // SSM_GROUP
STATE = 64
N_DIR = 2
DT_MIN = 1e-3
DT_MAX = 1e-1
D_CONV = D_MODEL
CONV_W = 3
EPS = 1e-6
SPLIT_SIZES = (D_SSM, D_SSM, D_CONV, D_CONV, D_CONV, D_CONV, D_MODEL, D_MODEL)
D_IN_PROJ = sum(SPLIT_SIZES)

kernel_name = "hybrid_s5_shortconv_gated_block"


def rmsnorm(x, g):
    xf = x.astype(jnp.float32)
    xf = xf * lax.rsqrt(jnp.mean(xf * xf, axis=-1, keepdims=True) + EPS)
    return (xf * g.astype(jnp.float32)).astype(x.dtype)


def cmul(ar, ai, br, bi):
    return ar * br - ai * bi, ar * bi + ai * br


def s5_scan(u, lam_re, lam_im, log_dt, b_re, b_im, c_re, c_im, reverse):
    lam_re = lam_re.astype(jnp.float32)
    lam_im = lam_im.astype(jnp.float32)
    dt = jnp.exp(log_dt.astype(jnp.float32))[:, None]
    mag = jnp.exp(lam_re * dt)
    ab_re, ab_im = mag * jnp.cos(lam_im * dt), mag * jnp.sin(lam_im * dt)
    den = lam_re * lam_re + lam_im * lam_im
    nr, ni = ab_re - 1.0, ab_im
    coef_re = (nr * lam_re + ni * lam_im) / den
    coef_im = (ni * lam_re - nr * lam_im) / den
    bb_re, bb_im = cmul(coef_re[..., None], coef_im[..., None],
                        b_re.astype(jnp.float32), b_im.astype(jnp.float32))
    bu_re = jnp.einsum('blgh,gph->blgp', u, bb_re)
    bu_im = jnp.einsum('blgh,gph->blgp', u, bb_im)
    a_re = jnp.broadcast_to(ab_re, bu_re.shape)
    a_im = jnp.broadcast_to(ab_im, bu_re.shape)

    def combine(e1, e2):
        a1r, a1i, b1r, b1i = e1
        a2r, a2i, b2r, b2i = e2
        ar, ai = cmul(a2r, a2i, a1r, a1i)
        tr, ti = cmul(a2r, a2i, b1r, b1i)
        return ar, ai, tr + b2r, ti + b2i

    _, _, s_re, s_im = lax.associative_scan(combine, (a_re, a_im, bu_re, bu_im),
                                            axis=1, reverse=reverse)
    return (jnp.einsum('blgp,ghp->blgh', s_re, c_re.astype(jnp.float32))
            - jnp.einsum('blgp,ghp->blgh', s_im, c_im.astype(jnp.float32)))


def setup_inputs(seed: int = 0) -> dict:
    key = jax.random.key(seed)
    ks = jax.random.split(key, 20)
    f32 = jnp.float32
    x = jax.random.normal(ks[0], (BATCH, SEQ, D_MODEL), f32)
    norm_g = 1.0 + 0.02 * jax.random.normal(ks[1], (DEPTH, D_MODEL), f32)
    w_in = jax.random.normal(ks[2], (DEPTH, D_MODEL, D_IN_PROJ), f32) * D_MODEL ** -0.5
    n = jnp.arange(STATE, dtype=f32)
    lam_re = -0.5 + 0.01 * jax.random.normal(ks[3], (DEPTH, N_DIR, N_GROUPS, STATE), f32)
    lam_im = math.pi * n + 0.01 * jax.random.normal(ks[4], (DEPTH, N_DIR, N_GROUPS, STATE), f32)
    log_dt = jax.random.uniform(ks[5], (DEPTH, N_DIR, N_GROUPS), f32,
                                minval=math.log(DT_MIN), maxval=math.log(DT_MAX))
    b_scale = (2.0 * SSM_GROUP) ** -0.5
    c_scale = (2.0 * STATE) ** -0.5
    ssm_b_re = jax.random.normal(ks[6], (DEPTH, N_DIR, N_GROUPS, STATE, SSM_GROUP), f32) * b_scale
    ssm_b_im = jax.random.normal(ks[7], (DEPTH, N_DIR, N_GROUPS, STATE, SSM_GROUP), f32) * b_scale
    ssm_c_re = jax.random.normal(ks[8], (DEPTH, N_DIR, N_GROUPS, SSM_GROUP, STATE), f32) * c_scale
    ssm_c_im = jax.random.normal(ks[9], (DEPTH, N_DIR, N_GROUPS, SSM_GROUP, STATE), f32) * c_scale
    ssm_d = jax.random.normal(ks[10], (DEPTH, D_SSM), f32)
    w_glu = jax.random.normal(ks[11], (DEPTH, D_SSM, D_SSM), f32) * D_SSM ** -0.5
    conv_w = jax.random.normal(ks[12], (DEPTH, CONV_W, D_CONV), f32) * CONV_W ** -0.5
    conv_b = 0.01 * jax.random.normal(ks[13], (DEPTH, D_CONV), f32)
    w_branch_a = jax.random.normal(ks[14], (DEPTH, D_SSM, D_MODEL), f32) * D_SSM ** -0.5
    w_branch_b = jax.random.normal(ks[15], (DEPTH, D_CONV, D_MODEL), f32) * D_CONV ** -0.5
    w_out = jax.random.normal(ks[16], (DEPTH, D_MODEL, D_MODEL), f32) * D_MODEL ** -0.5
    final_g = 1.0 + 0.02 * jax.random.normal(ks[17], (D_MODEL,), f32)
    return {"x": x, "norm_g": norm_g, "w_in": w_in, "lam_re": lam_re, "lam_im": lam_im,
            "log_dt": log_dt, "ssm_b_re": ssm_b_re, "ssm_b_im": ssm_b_im,
            "ssm_c_re": ssm_c_re, "ssm_c_im": ssm_c_im, "ssm_d": ssm_d, "w_glu": w_glu,
            "conv_w": conv_w, "conv_b": conv_b, "w_branch_a": w_branch_a,
            "w_branch_b": w_branch_b, "w_out": w_out, "final_g": final_g}


def reference(x, norm_g, w_in, lam_re, lam_im, log_dt, ssm_b_re, ssm_b_im, ssm_c_re,
              ssm_c_im, ssm_d, w_glu, conv_w, conv_b, w_branch_a, w_branch_b, w_out,
              final_g):
    B, L, _ = x.shape
    split_idx = [int(i) for i in np.cumsum(np.array(SPLIT_SIZES))[:-1]]
    h = x
    for layer in range(DEPTH):
        xn = rmsnorm(h, norm_g[layer])
        proj = jnp.einsum('bld,de->ble', xn, w_in[layer])
        u, z_a, v, b_g, c_g, z_b, g_a, g_b = jnp.split(proj, split_idx, axis=-1)

        uf = u.astype(jnp.float32).reshape(B, L, N_GROUPS, SSM_GROUP)
        y = ssm_d[layer].astype(jnp.float32).reshape(N_GROUPS, SSM_GROUP) * uf
        for d in range(N_DIR):
            y = y + s5_scan(uf, lam_re[layer, d], lam_im[layer, d], log_dt[layer, d],
                            ssm_b_re[layer, d], ssm_b_im[layer, d],
                            ssm_c_re[layer, d], ssm_c_im[layer, d], reverse=(d == 1))
        y = jax.nn.gelu(y.reshape(B, L, D_SSM).astype(x.dtype))
        y = y * jax.nn.sigmoid(jnp.einsum('ble,ef->blf', y, w_glu[layer]))
        y_a = y * jax.nn.silu(z_a)

        cv = c_g * v
        pad = (CONV_W - 1) // 2
        cvp = jnp.pad(cv, ((0, 0), (pad, CONV_W - 1 - pad), (0, 0)))
        conv = conv_b[layer]
        for k in range(CONV_W):
            conv = conv + cvp[:, k:k + L] * conv_w[layer, k]
        y_b = b_g * conv * jax.nn.silu(z_b)

        o_a = jnp.einsum('ble,ed->bld', y_a, w_branch_a[layer])
        o_b = jnp.einsum('ble,ed->bld', y_b, w_branch_b[layer])
        merged = jax.nn.sigmoid(g_a) * o_a + jax.nn.sigmoid(g_b) * o_b
        h = h + jnp.einsum('bld,de->ble', merged, w_out[layer])
    return rmsnorm(h, final_g)
```

```python
import functools
import math

import jax
import jax.numpy as jnp
from jax import lax
from jax.experimental import pallas as pl
from jax.experimental.pallas import tpu as pltpu

F32 = jnp.float32
BF16 = jnp.bfloat16

D_MODEL = 1024
SSM_GROUP = 16
N_GROUPS = D_MODEL // SSM_GROUP
STATE = 64
CHUNK = 16
CW = CHUNK * SSM_GROUP
EPS = 1e-6
SUBLANES = 8
VMEM_LIMIT_BYTES = 60 * 1024 * 1024

U_TILE = 1024
TM = 512
HALO = SUBLANES


def _rmsnorm(x, g):
    ms = jnp.mean(x * x, axis=-1, keepdims=True)
    return x * lax.rsqrt(ms + EPS) * g


def _sigmoid(x):
    return 0.5 * jnp.tanh(0.5 * x) + 0.5


def _gelu_tanh(x):
    c = math.sqrt(2.0 / math.pi)
    return x * (0.5 * (1.0 + jnp.tanh(c * (x + 0.044715 * (x * x * x)))))


def _uproj_kernel(x_ref, g_ref, w_ref, u_ref):
    xn = _rmsnorm(x_ref[...], g_ref[...]).astype(BF16)
    u_ref[...] = jnp.dot(xn, w_ref[...], preferred_element_type=F32).astype(BF16)


def _u_proj(x2d, norm_g, w_u):
    t = x2d.shape[0]
    return pl.pallas_call(
        _uproj_kernel,
        grid=(t // U_TILE,),
        in_specs=[
            pl.BlockSpec((U_TILE, D_MODEL), lambda i: (i, 0)),
            pl.BlockSpec((1, D_MODEL), lambda i: (0, 0)),
            pl.BlockSpec((D_MODEL, D_MODEL), lambda i: (0, 0)),
        ],
        out_specs=pl.BlockSpec((U_TILE, D_MODEL), lambda i: (i, 0)),
        out_shape=jax.ShapeDtypeStruct((t, D_MODEL), BF16),
        compiler_params=pltpu.CompilerParams(
            dimension_semantics=("arbitrary",), vmem_limit_bytes=VMEM_LIMIT_BYTES),
        name="u_proj",
    )(x2d, norm_g, w_u)


def _s5_kernel(u_ref, m_ref, wst_ref, wout_ref, aq_ref, y_ref, sloc_ref, st_ref, prev_ref,
               *, n_chunks, batch):
    u = u_ref[0]
    sloc_ref[...] = jnp.dot(u, wst_ref[0], preferred_element_type=F32)

    half = 2 * STATE
    aq = aq_ref[0]
    ar = jnp.broadcast_to(aq[0:1, :], (batch, half))
    ai = jnp.broadcast_to(aq[1:2, :], (batch, half))
    is_fwd = lax.broadcasted_iota(jnp.int32, (batch, half), 1) < STATE
    is_fwd2 = jnp.concatenate([is_fwd, is_fwd], axis=1)

    st_ref[0:batch, :] = jnp.zeros((batch, 2 * half), F32)

    def scan_step(k, carry):
        sr, si = carry
        rf = pl.multiple_of(k * batch, batch)
        rb = pl.multiple_of((n_chunks - 1 - k) * batch, batch)
        xf = sloc_ref[pl.ds(rf, batch), :]
        xb = sloc_ref[pl.ds(rb, batch), :]
        xr = jnp.where(is_fwd, xf[:, :half], xb[:, :half])
        xi = jnp.where(is_fwd, xf[:, half:], xb[:, half:])
        nr = ar * sr - ai * si + xr
        ni = ar * si + ai * sr + xi
        ro = pl.multiple_of((k + 1) * batch, batch)
        st_ref[pl.ds(ro, batch), :] = jnp.concatenate([nr, ni], axis=1)
        return nr, ni

    zero = jnp.zeros((batch, half), F32)
    lax.fori_loop(0, n_chunks, scan_step, (zero, zero), unroll=4)

    def gather_step(c, carry):
        rf = pl.multiple_of(c * batch, batch)
        rb = pl.multiple_of((n_chunks - 1 - c) * batch, batch)
        prev_ref[pl.ds(rf, batch), :] = jnp.where(
            is_fwd2, st_ref[pl.ds(rf, batch), :], st_ref[pl.ds(rb, batch), :])
        return carry

    lax.fori_loop(0, n_chunks, gather_step, 0, unroll=4)

    y = jnp.dot(u, m_ref[0], preferred_element_type=F32)
    y = y + jnp.dot(prev_ref[...].astype(BF16), wout_ref[0], preferred_element_type=F32)
    y_ref[0] = y


def _s5_chunked(u_g, m, wst, wout, aq, *, n_chunks, batch):
    rows = n_chunks * batch
    kern = functools.partial(_s5_kernel, n_chunks=n_chunks, batch=batch)
    wspec = pl.BlockSpec((1, CW, CW), lambda g: (g, 0, 0))
    return pl.pallas_call(
        kern,
        grid=(N_GROUPS,),
        in_specs=[
            pl.BlockSpec((1, rows, CW), lambda g: (g, 0, 0)),
            wspec, wspec, wspec,
            pl.BlockSpec((1, 2, 2 * STATE), lambda g: (g, 0, 0)),
        ],
        out_specs=pl.BlockSpec((1, rows, CW), lambda g: (g, 0, 0)),
        out_shape=jax.ShapeDtypeStruct((N_GROUPS, rows, CW), F32),
        scratch_shapes=[
            pltpu.VMEM((rows, CW), F32),
            pltpu.VMEM((rows + batch, CW), F32),
            pltpu.VMEM((rows, CW), F32),
        ],
        compiler_params=pltpu.CompilerParams(
            dimension_semantics=("arbitrary",), vmem_limit_bytes=VMEM_LIMIT_BYTES),
        name="s5_chunk",
    )(u_g, m, wst, wout, aq)


def _s5_operators(lam_re, lam_im, log_dt, b_re, b_im, c_re, c_im, ssm_d):
    dt = jnp.exp(log_dt)[..., None]
    mag = jnp.exp(lam_re * dt)
    ab_re, ab_im = mag * jnp.cos(lam_im * dt), mag * jnp.sin(lam_im * dt)
    den = lam_re * lam_re + lam_im * lam_im
    nr, ni = ab_re - 1.0, ab_im
    coef_re = (nr * lam_re + ni * lam_im) / den
    coef_im = (ni * lam_re - nr * lam_im) / den
    bb_re = coef_re[..., None] * b_re - coef_im[..., None] * b_im
    bb_im = coef_re[..., None] * b_im + coef_im[..., None] * b_re

    tau = jnp.arange(CHUNK + 1, dtype=F32)[:, None, None, None]
    pmag = jnp.exp(tau * (lam_re * dt)[None])
    pw_re = pmag * jnp.cos(tau * (lam_im * dt)[None])
    pw_im = pmag * jnp.sin(tau * (lam_im * dt)[None])

    ca_re = c_re[None] * pw_re[:, :, :, None, :] - c_im[None] * pw_im[:, :, :, None, :]
    ca_im = c_re[None] * pw_im[:, :, :, None, :] + c_im[None] * pw_re[:, :, :, None, :]
    hp = lax.Precision.HIGHEST
    k_tau = (jnp.einsum('tdgop,dgpi->tdgoi', ca_re[:CHUNK], bb_re, precision=hp)
             - jnp.einsum('tdgop,dgpi->tdgoi', ca_im[:CHUNK], bb_im, precision=hp))

    s_idx = jnp.arange(CHUNK)[:, None]
    t_idx = jnp.arange(CHUNK)[None, :]
    lag = t_idx - s_idx
    kf = k_tau[:, 0][jnp.clip(lag, 0, CHUNK - 1)]
    kb = k_tau[:, 1][jnp.clip(-lag, 0, CHUNK - 1)]
    m = (jnp.where((lag >= 0)[:, :, None, None, None], kf, 0.0)
         + jnp.where((lag <= 0)[:, :, None, None, None], kb, 0.0))
    d_diag = ssm_d.reshape(N_GROUPS, SSM_GROUP)[:, :, None] * jnp.eye(SSM_GROUP, dtype=F32)
    m = m + jnp.where((lag == 0)[:, :, None, None, None], d_diag[None, None], 0.0)
    m = m.transpose(2, 0, 4, 1, 3).reshape(N_GROUPS, CW, CW)

    rev = jnp.arange(CHUNK - 1, -1, -1)
    pf_re, pf_im = pw_re[rev, 0], pw_im[rev, 0]
    pb_re, pb_im = pw_re[:CHUNK, 1], pw_im[:CHUNK, 1]

    def outer(p_re, p_im, d):
        re = p_re[:, :, :, None] * bb_re[d][None] - p_im[:, :, :, None] * bb_im[d][None]
        im = p_re[:, :, :, None] * bb_im[d][None] + p_im[:, :, :, None] * bb_re[d][None]
        return re.transpose(1, 0, 3, 2), im.transpose(1, 0, 3, 2)

    wf_re, wf_im = outer(pf_re, pf_im, 0)
    wb_re, wb_im = outer(pb_re, pb_im, 1)
    wst = jnp.concatenate([wf_re, wb_re, wf_im, wb_im], axis=-1).reshape(N_GROUPS, CW, 4 * STATE)

    of_re, of_im = ca_re[1:, 0], ca_im[1:, 0]
    ob_re, ob_im = ca_re[rev + 1, 1], ca_im[rev + 1, 1]

    def to_rows(x):
        return x.transpose(1, 3, 0, 2).reshape(N_GROUPS, STATE, CW)

    wout = jnp.concatenate([to_rows(of_re), to_rows(ob_re), -to_rows(of_im), -to_rows(ob_im)], axis=1)

    aq = jnp.stack([jnp.concatenate([pw_re[CHUNK, 0], pw_re[CHUNK, 1]], axis=-1),
                    jnp.concatenate([pw_im[CHUNK, 0], pw_im[CHUNK, 1]], axis=-1)], axis=1)
    return m.astype(BF16), wst.astype(BF16), wout.astype(BF16), aq


def _block_kernel(x_ref, xprev_ref, xnext_ref, y_ref, ng_ref, w7_ref, wglu_ref, wa_ref, wb_ref,
                  wo_ref, cw_ref, cb_ref, fg_ref, o_ref, xs_ref, *, tiles_per_seq):
    i = pl.program_id(0)
    x = x_ref[...]
    ng = ng_ref[...]
    xs_ref[0:TM, :] = _rmsnorm(x, ng).astype(BF16)
    halo = jnp.concatenate([xprev_ref[...], xnext_ref[...]], axis=0)
    xs_ref[TM:TM + 2 * HALO, :] = _rmsnorm(halo, ng).astype(BF16)

    def proj(k, lhs):
        return jnp.dot(lhs, w7_ref[:, k * D_MODEL:(k + 1) * D_MODEL], preferred_element_type=F32)

    xall = xs_ref[...]
    cv_all = proj(1, xall) * proj(3, xall)
    cv = cv_all[:TM]
    pos = i % tiles_per_seq
    prev_row = jnp.where(pos == 0, 0.0, cv_all[TM + HALO - 1:TM + HALO])
    next_row = jnp.where(pos == tiles_per_seq - 1, 0.0, cv_all[TM + HALO:TM + HALO + 1])
    row = lax.broadcasted_iota(jnp.int32, (TM, 1), 0)
    cv_m1 = jnp.where(row == 0, prev_row, pltpu.roll(cv, 1, 0))
    cv_p1 = jnp.where(row == TM - 1, next_row, pltpu.roll(cv, TM - 1, 0))
    cw = cw_ref[...]
    conv = cb_ref[...] + cv_m1 * cw[0:1] + cv * cw[1:2] + cv_p1 * cw[2:3]

    xs = xs_ref[0:TM, :]
    z_b = proj(4, xs)
    y_b = (proj(2, xs) * conv * (z_b * _sigmoid(z_b))).astype(BF16)
    merged = _sigmoid(proj(6, xs)) * jnp.dot(y_b, wb_ref[...], preferred_element_type=F32)

    yg = _gelu_tanh(y_ref[...])
    yg = yg * _sigmoid(jnp.dot(yg.astype(BF16), wglu_ref[...], preferred_element_type=F32))
    z_a = proj(0, xs)
    y_a = (yg * (z_a * _sigmoid(z_a))).astype(BF16)
    merged = merged + _sigmoid(proj(5, xs)) * jnp.dot(y_a, wa_ref[...], preferred_element_type=F32)

    h = x + jnp.dot(merged.astype(BF16), wo_ref[...], preferred_element_type=F32)
    o_ref[...] = _rmsnorm(h, fg_ref[...])


def _fused_block(x2d, y2d, norm_g, w7, w_glu, w_a, w_b, w_o, conv_w, conv_b, final_g, *, seq_len):
    t = x2d.shape[0]
    n_tiles = t // TM
    tiles_per_seq = seq_len // TM
    hb = TM // HALO
    n_hblocks = t // HALO

    def const(shape):
        return pl.BlockSpec(shape, lambda i: (0,) * len(shape), pipeline_mode=pl.Buffered(1))

    kern = functools.partial(_block_kernel, tiles_per_seq=tiles_per_seq)
    return pl.pallas_call(
        kern,
        grid=(n_tiles,),
        in_specs=[
            pl.BlockSpec((TM, D_MODEL), lambda i: (i, 0)),
            pl.BlockSpec((HALO, D_MODEL), lambda i: (jnp.maximum(i * hb - 1, 0), 0)),
            pl.BlockSpec((HALO, D_MODEL), lambda i: (jnp.minimum((i + 1) * hb, n_hblocks - 1), 0)),
            pl.BlockSpec((TM, D_MODEL), lambda i: (i, 0)),
            const((1, D_MODEL)),
            const((D_MODEL, 7 * D_MODEL)),
            const((D_MODEL, D_MODEL)),
            const((D_MODEL, D_MODEL)),
            const((D_MODEL, D_MODEL)),
            const((D_MODEL, D_MODEL)),
            const((3, D_MODEL)),
            const((1, D_MODEL)),
            const((1, D_MODEL)),
        ],
        out_specs=pl.BlockSpec((TM, D_MODEL), lambda i: (i, 0)),
        out_shape=jax.ShapeDtypeStruct((t, D_MODEL), F32),
        scratch_shapes=[pltpu.VMEM((TM + 2 * HALO, D_MODEL), BF16)],
        compiler_params=pltpu.CompilerParams(
            dimension_semantics=("arbitrary",), vmem_limit_bytes=VMEM_LIMIT_BYTES),
        name="fused_block",
    )(x2d, x2d, x2d, y2d, norm_g, w7, w_glu, w_a, w_b, w_o, conv_w, conv_b, final_g)


def kernel(x, norm_g, w_in, lam_re, lam_im, log_dt, ssm_b_re, ssm_b_im, ssm_c_re, ssm_c_im, ssm_d,
           w_glu, conv_w, conv_b, w_branch_a, w_branch_b, w_out, final_g):
    bsz, seq_len, d = x.shape
    depth = norm_g.shape[0]
    assert depth == 1 and d == D_MODEL and bsz == SUBLANES
    assert seq_len % TM == 0 and (bsz * seq_len) % U_TILE == 0
    n_chunks = seq_len // CHUNK
    h = x.reshape(bsz * seq_len, d)
    for layer in range(depth):
        w_all = w_in[layer].astype(BF16)
        ng = norm_g[layer].reshape(1, d)
        u = _u_proj(h, ng, w_all[:, :d])
        u_g = (u.reshape(bsz, n_chunks, CHUNK, N_GROUPS, SSM_GROUP)
               .transpose(3, 1, 0, 2, 4).reshape(N_GROUPS, n_chunks * bsz, CW))
        m, wst, wout, aq = _s5_operators(
            lam_re[layer], lam_im[layer], log_dt[layer], ssm_b_re[layer], ssm_b_im[layer],
            ssm_c_re[layer], ssm_c_im[layer], ssm_d[layer])
        y_g = _s5_chunked(u_g, m, wst, wout, aq, n_chunks=n_chunks, batch=bsz)
        y = (y_g.reshape(N_GROUPS, n_chunks, bsz, CHUNK, SSM_GROUP)
             .transpose(2, 1, 3, 0, 4).reshape(bsz * seq_len, d))
        h = _fused_block(
            h, y, ng, w_all[:, d:], w_glu[layer].astype(BF16), w_branch_a[layer].astype(BF16),
            w_branch_b[layer].astype(BF16), w_out[layer].astype(BF16), conv_w[layer],
            conv_b[layer].reshape(1, d), final_g.reshape(1, d), seq_len=seq_len)
    return h.reshape(bsz, seq_len, d)
```

```python
import functools
import math

import jax
import jax.numpy as jnp
from jax import lax
from jax.experimental import pallas as pl
from jax.experimental.pallas import tpu as pltpu

F32 = jnp.float32
BF16 = jnp.bfloat16

D_MODEL = 1024
SSM_GROUP = 16
N_GROUPS = D_MODEL // SSM_GROUP
STATE = 64
CHUNK = 16
CW = CHUNK * SSM_GROUP
SW = 4 * STATE
HALF = 2 * STATE
LANES = 128
EPS = 1e-6
SUBLANES = 8
VMEM_LIMIT_BYTES = 60 * 1024 * 1024

NT_DIMS = (((1,), (1,)), ((), ()))
TN_DIMS = (((0,), (0,)), ((), ()))


def _rmsnorm(x, g):
    ms = jnp.mean(x * x, axis=-1, keepdims=True)
    return x * lax.rsqrt(ms + EPS) * g


def _sigmoid(x):
    return 0.5 * jnp.tanh(0.5 * x) + 0.5


def _gelu_tanh(x):
    c = math.sqrt(2.0 / math.pi)
    return x * (0.5 * (1.0 + jnp.tanh(c * (x + 0.044715 * (x * x * x)))))


def _params(n_axes):
    return pltpu.CompilerParams(
        dimension_semantics=("arbitrary",) * n_axes, vmem_limit_bytes=VMEM_LIMIT_BYTES)


def _const_spec(shape):
    return pl.BlockSpec(shape, lambda *_: (0,) * len(shape), pipeline_mode=pl.Buffered(1))


def _in_proj_kernel(x_ref, g_ref, w_ref, u_ref, cv_ref):
    xn = _rmsnorm(x_ref[...], g_ref[...]).astype(BF16)
    p = lax.dot_general(w_ref[...], xn, NT_DIMS, preferred_element_type=F32)
    u_ref[...] = p[:D_MODEL].astype(BF16)
    cv_ref[...] = (p[D_MODEL:2 * D_MODEL] * p[2 * D_MODEL:]).astype(BF16)


def _in_proj_t(x3, norm_g, w3t, *, bsz, n_chunks):
    tok_spec = pl.BlockSpec((None, n_chunks, D_MODEL), lambda b, tl: (b, 0, tl))
    out_spec = pl.BlockSpec((None, D_MODEL, n_chunks), lambda b, tl: (tl, 0, b))
    out_shape = jax.ShapeDtypeStruct((CHUNK, D_MODEL, bsz * n_chunks), BF16)
    return pl.pallas_call(
        _in_proj_kernel,
        grid=(bsz, CHUNK),
        in_specs=[tok_spec, _const_spec((1, D_MODEL)), _const_spec((3 * D_MODEL, D_MODEL))],
        out_specs=[out_spec, out_spec],
        out_shape=[out_shape, out_shape],
        compiler_params=_params(2),
        name="in_proj_t",
    )(x3, norm_g, w3t)


def _s5_kernel(u_ref, mt_ref, wst_ref, woutt_ref, aq_ref, y_ref, sloc_ref, st_ref, prev_ref,
               *, n_chunks, batch):
    n = batch * n_chunks
    ut = u_ref[...].reshape(CW, n)
    wst = wst_ref[0]

    for b in range(batch):
        sl = lax.dot_general(ut[:, b * n_chunks:(b + 1) * n_chunks], wst, TN_DIMS,
                             preferred_element_type=F32)
        sloc_ref[0, pl.ds(b, n_chunks, stride=batch), :] = sl[:, :HALF]
        sloc_ref[1, pl.ds(b, n_chunks, stride=batch), :] = sl[:, HALF:]

    aq = aq_ref[0]
    ar = jnp.broadcast_to(aq[0:1, :], (batch, HALF))
    ai = jnp.broadcast_to(aq[1:2, :], (batch, HALF))
    is_fwd = lax.broadcasted_iota(jnp.int32, (batch, HALF), 1) < STATE

    zero = jnp.zeros((batch, HALF), F32)
    st_ref[0, 0:batch, :] = zero
    st_ref[1, 0:batch, :] = zero

    def scan_step(k, carry):
        sr, si = carry
        rf = pl.multiple_of(k * batch, batch)
        rb = pl.multiple_of((n_chunks - 1 - k) * batch, batch)
        xr = jnp.where(is_fwd, sloc_ref[0, pl.ds(rf, batch), :], sloc_ref[0, pl.ds(rb, batch), :])
        xi = jnp.where(is_fwd, sloc_ref[1, pl.ds(rf, batch), :], sloc_ref[1, pl.ds(rb, batch), :])
        nr = ar * sr - ai * si + xr
        ni = ar * si + ai * sr + xi
        ro = pl.multiple_of((k + 1) * batch, batch)
        st_ref[0, pl.ds(ro, batch), :] = nr
        st_ref[1, pl.ds(ro, batch), :] = ni
        return nr, ni

    lax.fori_loop(0, n_chunks, scan_step, (zero, zero), unroll=4)

    def gather_step(c, carry):
        rf = pl.multiple_of(c * batch, batch)
        rb = pl.multiple_of((n_chunks - 1 - c) * batch, batch)
        for h in range(2):
            prev_ref[h, pl.ds(rf, batch), :] = jnp.where(
                is_fwd, st_ref[h, pl.ds(rf, batch), :], st_ref[h, pl.ds(rb, batch), :])
        return carry

    lax.fori_loop(0, n_chunks, gather_step, 0, unroll=4)

    mt = mt_ref[0]
    woutt = woutt_ref[0]
    for b in range(batch):
        cols = slice(b * n_chunks, (b + 1) * n_chunks)
        pb = jnp.concatenate([prev_ref[0, pl.ds(b, n_chunks, stride=batch), :],
                              prev_ref[1, pl.ds(b, n_chunks, stride=batch), :]], axis=1)
        yb = jnp.dot(mt, ut[:, cols], preferred_element_type=F32)
        yb = yb + lax.dot_general(woutt, pb.astype(BF16), NT_DIMS, preferred_element_type=F32)
        y_ref[:, :, cols] = yb.reshape(CHUNK, SSM_GROUP, n_chunks)


def _s5_t(ut, mt, wst, woutt, aq, *, n_chunks, batch):
    n = batch * n_chunks
    kern = functools.partial(_s5_kernel, n_chunks=n_chunks, batch=batch)
    op_spec = pl.BlockSpec((1, CW, CW), lambda g: (g, 0, 0))
    act_spec = pl.BlockSpec((CHUNK, SSM_GROUP, n), lambda g: (0, g, 0))
    return pl.pallas_call(
        kern,
        grid=(N_GROUPS,),
        in_specs=[act_spec, op_spec, op_spec, op_spec,
                  pl.BlockSpec((1, 2, HALF), lambda g: (g, 0, 0))],
        out_specs=act_spec,
        out_shape=jax.ShapeDtypeStruct((CHUNK, D_MODEL, n), F32),
        scratch_shapes=[
            pltpu.VMEM((2, n, HALF), F32),
            pltpu.VMEM((2, n + batch, HALF), F32),
            pltpu.VMEM((2, n, HALF), F32),
        ],
        compiler_params=_params(1),
        name="s5_t",
    )(ut, mt, wst, woutt, aq)


def _s5_operators(lam_re, lam_im, log_dt, b_re, b_im, c_re, c_im, ssm_d):
    dt = jnp.exp(log_dt)[..., None]
    mag = jnp.exp(lam_re * dt)
    ab_re, ab_im = mag * jnp.cos(lam_im * dt), mag * jnp.sin(lam_im * dt)
    den = lam_re * lam_re + lam_im * lam_im
    nr, ni = ab_re - 1.0, ab_im
    coef_re = (nr * lam_re + ni * lam_im) / den
    coef_im = (ni * lam_re - nr * lam_im) / den
    bb_re = coef_re[..., None] * b_re - coef_im[..., None] * b_im
    bb_im = coef_re[..., None] * b_im + coef_im[..., None] * b_re

    tau = jnp.arange(CHUNK + 1, dtype=F32)[:, None, None, None]
    pmag = jnp.exp(tau * (lam_re * dt)[None])
    pw_re = pmag * jnp.cos(tau * (lam_im * dt)[None])
    pw_im = pmag * jnp.sin(tau * (lam_im * dt)[None])

    ca_re = c_re[None] * pw_re[:, :, :, None, :] - c_im[None] * pw_im[:, :, :, None, :]
    ca_im = c_re[None] * pw_im[:, :, :, None, :] + c_im[None] * pw_re[:, :, :, None, :]
    hp = lax.Precision.HIGHEST
    k_tau = (jnp.einsum('tdgop,dgpi->tdgoi', ca_re[:CHUNK], bb_re, precision=hp)
             - jnp.einsum('tdgop,dgpi->tdgoi', ca_im[:CHUNK], bb_im, precision=hp))

    s_idx = jnp.arange(CHUNK)[:, None]
    t_idx = jnp.arange(CHUNK)[None, :]
    lag = t_idx - s_idx
    kf = k_tau[:, 0][jnp.clip(lag, 0, CHUNK - 1)]
    kb = k_tau[:, 1][jnp.clip(-lag, 0, CHUNK - 1)]
    m = (jnp.where((lag >= 0)[:, :, None, None, None], kf, 0.0)
         + jnp.where((lag <= 0)[:, :, None, None, None], kb, 0.0))
    d_diag = ssm_d.reshape(N_GROUPS, SSM_GROUP)[:, :, None] * jnp.eye(SSM_GROUP, dtype=F32)
    m = m + jnp.where((lag == 0)[:, :, None, None, None], d_diag[None, None], 0.0)
    mt = m.transpose(2, 1, 3, 0, 4).reshape(N_GROUPS, CW, CW)

    rev = jnp.arange(CHUNK - 1, -1, -1)
    pf_re, pf_im = pw_re[rev, 0], pw_im[rev, 0]
    pb_re, pb_im = pw_re[:CHUNK, 1], pw_im[:CHUNK, 1]

    def outer(p_re, p_im, d):
        re = p_re[:, :, :, None] * bb_re[d][None] - p_im[:, :, :, None] * bb_im[d][None]
        im = p_re[:, :, :, None] * bb_im[d][None] + p_im[:, :, :, None] * bb_re[d][None]
        return re.transpose(1, 0, 3, 2), im.transpose(1, 0, 3, 2)

    wf_re, wf_im = outer(pf_re, pf_im, 0)
    wb_re, wb_im = outer(pb_re, pb_im, 1)
    wst = jnp.concatenate([wf_re, wb_re, wf_im, wb_im], axis=-1).reshape(N_GROUPS, CW, SW)

    of_re, of_im = ca_re[1:, 0], ca_im[1:, 0]
    ob_re, ob_im = ca_re[rev + 1, 1], ca_im[rev + 1, 1]
    woutt = jnp.concatenate([of_re, ob_re, -of_im, -ob_im], axis=-1)
    woutt = woutt.transpose(1, 0, 2, 3).reshape(N_GROUPS, CW, SW)

    aq = jnp.stack([jnp.concatenate([pw_re[CHUNK, 0], pw_re[CHUNK, 1]], axis=-1),
                    jnp.concatenate([pw_im[CHUNK, 0], pw_im[CHUNK, 1]], axis=-1)], axis=1)
    return mt.astype(BF16), wst.astype(BF16), woutt.astype(BF16), aq


def _block_kernel(x_ref, y_ref, cvm_ref, cv_ref, cvp_ref, ng_ref, w5t_ref, wglut_ref, wat_ref,
                  wbt_ref, wo_ref, cw_ref, cb_ref, fg_ref, o_ref, *, n_chunks):
    tl = pl.program_id(1)
    x = x_ref[...]
    xn = _rmsnorm(x, ng_ref[...]).astype(BF16)
    p = lax.dot_general(w5t_ref[...], xn, NT_DIMS, preferred_element_type=F32)

    def blk(k):
        return p[k * D_MODEL:(k + 1) * D_MODEL]

    def wide(ref, k=None):
        v = ref[...] if k is None else ref[k]
        return jnp.concatenate([v] * (n_chunks // LANES), axis=1)

    lane = lax.broadcasted_iota(jnp.int32, (D_MODEL, n_chunks), 1)
    cvm = cvm_ref[...].astype(F32)
    cvp = cvp_ref[...].astype(F32)
    cvm = jnp.where(tl == 0, jnp.where(lane == 0, 0.0, pltpu.roll(cvm, 1, 1)), cvm)
    cvp = jnp.where(tl == CHUNK - 1,
                    jnp.where(lane == n_chunks - 1, 0.0, pltpu.roll(cvp, n_chunks - 1, 1)), cvp)
    conv = (wide(cb_ref) + cvm * wide(cw_ref, 0) + cv_ref[...].astype(F32) * wide(cw_ref, 1)
            + cvp * wide(cw_ref, 2))

    z_b = blk(2)
    y_b = (blk(1) * conv * (z_b * _sigmoid(z_b))).astype(BF16)
    merged = _sigmoid(blk(4)) * jnp.dot(wbt_ref[...], y_b, preferred_element_type=F32)

    yg = _gelu_tanh(y_ref[...])
    yg = yg * _sigmoid(jnp.dot(wglut_ref[...], yg.astype(BF16), preferred_element_type=F32))
    z_a = blk(0)
    y_a = (yg * (z_a * _sigmoid(z_a))).astype(BF16)
    merged = merged + _sigmoid(blk(3)) * jnp.dot(wat_ref[...], y_a, preferred_element_type=F32)

    delta = lax.dot_general(merged.astype(BF16), wo_ref[...], TN_DIMS,
                            preferred_element_type=F32)
    o_ref[...] = _rmsnorm(x + delta, fg_ref[...])


def _block_t(x3, yt, cvt, norm_g, w5t, wglut, wat, wbt, w_o, cw, cb, final_g, *, bsz, n_chunks):
    tok_spec = pl.BlockSpec((None, n_chunks, D_MODEL), lambda b, tl: (b, 0, tl))

    def set_spec(shift):
        return pl.BlockSpec((None, D_MODEL, n_chunks),
                            lambda b, tl: ((tl + shift) % CHUNK, 0, b))

    kern = functools.partial(_block_kernel, n_chunks=n_chunks)
    return pl.pallas_call(
        kern,
        grid=(bsz, CHUNK),
        in_specs=[
            tok_spec, set_spec(0), set_spec(CHUNK - 1), set_spec(0), set_spec(1),
            _const_spec((1, D_MODEL)),
            _const_spec((5 * D_MODEL, D_MODEL)),
            _const_spec((D_MODEL, D_MODEL)),
            _const_spec((D_MODEL, D_MODEL)),
            _const_spec((D_MODEL, D_MODEL)),
            _const_spec((D_MODEL, D_MODEL)),
            _const_spec((3, D_MODEL, LANES)),
            _const_spec((D_MODEL, LANES)),
            _const_spec((1, D_MODEL)),
        ],
        out_specs=tok_spec,
        out_shape=jax.ShapeDtypeStruct(x3.shape, F32),
        compiler_params=_params(2),
        name="block_t",
    )(x3, yt, cvt, cvt, cvt, norm_g, w5t, wglut, wat, wbt, w_o, cw, cb, final_g)


def kernel(x, norm_g, w_in, lam_re, lam_im, log_dt, ssm_b_re, ssm_b_im, ssm_c_re, ssm_c_im, ssm_d,
           w_glu, conv_w, conv_b, w_branch_a, w_branch_b, w_out, final_g):
    bsz, seq_len, d = x.shape
    assert norm_g.shape[0] == 1 and d == D_MODEL and bsz == SUBLANES
    n_chunks = seq_len // CHUNK
    assert seq_len == n_chunks * CHUNK and n_chunks % LANES == 0

    x3 = x.reshape(bsz, n_chunks, CHUNK * d)
    ng = norm_g[0].reshape(1, d)
    wt = w_in[0].T.astype(BF16).reshape(8, d, d)
    w3t = wt[jnp.array([0, 2, 4])].reshape(3 * d, d)
    w5t = wt[jnp.array([1, 3, 5, 6, 7])].reshape(5 * d, d)

    ut, cvt = _in_proj_t(x3, ng, w3t, bsz=bsz, n_chunks=n_chunks)
    mt, wst, woutt, aq = _s5_operators(
        lam_re[0], lam_im[0], log_dt[0], ssm_b_re[0], ssm_b_im[0], ssm_c_re[0], ssm_c_im[0],
        ssm_d[0])
    yt = _s5_t(ut, mt, wst, woutt, aq, n_chunks=n_chunks, batch=bsz)

    cw = jnp.broadcast_to(conv_w[0][:, :, None], (3, d, LANES))
    cb = jnp.broadcast_to(conv_b[0][:, None], (d, LANES))
    out3 = _block_t(
        x3, yt, cvt, ng, w5t, w_glu[0].T.astype(BF16), w_branch_a[0].T.astype(BF16),
        w_branch_b[0].T.astype(BF16), w_out[0].astype(BF16), cw, cb, final_g.reshape(1, d),
        bsz=bsz, n_chunks=n_chunks)
    return out3.reshape(bsz, seq_len, d)
```

```python
import functools
import math

import jax
import jax.numpy as jnp
from jax import lax
from jax.experimental import pallas as pl
from jax.experimental.pallas import tpu as pltpu

F32 = jnp.float32
BF16 = jnp.bfloat16

D_MODEL = 1024
SSM_GROUP = 16
N_GROUPS = D_MODEL // SSM_GROUP
STATE = 64
CHUNK = 16
CW = CHUNK * SSM_GROUP
SW = 4 * STATE
HALF = 2 * STATE
LANES = 128
EPS = 1e-6
SUBLANES = 8
SETS_PER_STEP = 2
VMEM_LIMIT_BYTES = 60 * 1024 * 1024

NT_DIMS = (((1,), (1,)), ((), ()))
TN_DIMS = (((0,), (0,)), ((), ()))


def _rmsnorm(x, g):
    ms = jnp.mean(x * x, axis=-1, keepdims=True)
    return x * lax.rsqrt(ms + EPS) * g


def _sigmoid(x):
    return 0.5 * jnp.tanh(0.5 * x) + 0.5


def _gelu_tanh(x):
    c = math.sqrt(2.0 / math.pi)
    return x * (0.5 * (1.0 + jnp.tanh(c * (x + 0.044715 * (x * x * x)))))


def _params(n_axes):
    return pltpu.CompilerParams(
        dimension_semantics=("arbitrary",) * n_axes, vmem_limit_bytes=VMEM_LIMIT_BYTES)


def _const_spec(shape):
    return pl.BlockSpec(shape, lambda *_: (0,) * len(shape), pipeline_mode=pl.Buffered(1))


def _load_sets(ref):
    return jnp.concatenate(
        [ref[:, k * D_MODEL:(k + 1) * D_MODEL] for k in range(SETS_PER_STEP)], axis=0)


def _in_proj_kernel(x_ref, g_ref, wut_ref, wvc_ref, u_ref, cv_ref, *, n_chunks):
    xn = _rmsnorm(_load_sets(x_ref), g_ref[...]).astype(BF16)
    ut = lax.dot_general(wut_ref[...], xn, NT_DIMS, preferred_element_type=F32)
    vc = jnp.dot(xn, wvc_ref[...], preferred_element_type=F32)
    cv = (vc[:, :D_MODEL] * vc[:, D_MODEL:]).astype(BF16)
    for k in range(SETS_PER_STEP):
        u_ref[k] = ut[:, k * n_chunks:(k + 1) * n_chunks].astype(BF16)
        cv_ref[:, k * D_MODEL:(k + 1) * D_MODEL] = cv[k * n_chunks:(k + 1) * n_chunks]


def _in_proj_t(x3, norm_g, wut, wvc, *, bsz, n_chunks):
    tok_spec = pl.BlockSpec((None, n_chunks, SETS_PER_STEP * D_MODEL), lambda b, tp: (b, 0, tp))
    return pl.pallas_call(
        functools.partial(_in_proj_kernel, n_chunks=n_chunks),
        grid=(bsz, CHUNK // SETS_PER_STEP),
        in_specs=[tok_spec, _const_spec((1, D_MODEL)), _const_spec((D_MODEL, D_MODEL)),
                  _const_spec((D_MODEL, 2 * D_MODEL))],
        out_specs=[pl.BlockSpec((SETS_PER_STEP, D_MODEL, n_chunks), lambda b, tp: (tp, 0, b)),
                   tok_spec],
        out_shape=[jax.ShapeDtypeStruct((CHUNK, D_MODEL, bsz * n_chunks), BF16),
                   jax.ShapeDtypeStruct(x3.shape, BF16)],
        compiler_params=_params(2),
        name="in_proj_t",
    )(x3, norm_g, wut, wvc)


def _s5_kernel(u_ref, mt_ref, wst_ref, woutt_ref, aq_ref, y_ref, sloc_ref, st_ref, prev_ref,
               *, n_chunks, batch):
    n = batch * n_chunks
    ut = u_ref[...].reshape(CW, n)
    wst = wst_ref[0]

    for b in range(batch):
        sl = lax.dot_general(ut[:, b * n_chunks:(b + 1) * n_chunks], wst, TN_DIMS,
                             preferred_element_type=F32)
        sloc_ref[0, pl.ds(b, n_chunks, stride=batch), :] = sl[:, :HALF]
        sloc_ref[1, pl.ds(b, n_chunks, stride=batch), :] = sl[:, HALF:]

    aq = aq_ref[0]
    ar = jnp.broadcast_to(aq[0:1, :], (batch, HALF))
    ai = jnp.broadcast_to(aq[1:2, :], (batch, HALF))
    is_fwd = lax.broadcasted_iota(jnp.int32, (batch, HALF), 1) < STATE

    zero = jnp.zeros((batch, HALF), F32)
    st_ref[0, 0:batch, :] = zero
    st_ref[1, 0:batch, :] = zero

    def scan_step(k, carry):
        sr, si = carry
        rf = pl.multiple_of(k * batch, batch)
        rb = pl.multiple_of((n_chunks - 1 - k) * batch, batch)
        xr = jnp.where(is_fwd, sloc_ref[0, pl.ds(rf, batch), :], sloc_ref[0, pl.ds(rb, batch), :])
        xi = jnp.where(is_fwd, sloc_ref[1, pl.ds(rf, batch), :], sloc_ref[1, pl.ds(rb, batch), :])
        nr = ar * sr - ai * si + xr
        ni = ar * si + ai * sr + xi
        ro = pl.multiple_of((k + 1) * batch, batch)
        st_ref[0, pl.ds(ro, batch), :] = nr
        st_ref[1, pl.ds(ro, batch), :] = ni
        return nr, ni

    lax.fori_loop(0, n_chunks, scan_step, (zero, zero), unroll=4)

    def gather_step(c, carry):
        rf = pl.multiple_of(c * batch, batch)
        rb = pl.multiple_of((n_chunks - 1 - c) * batch, batch)
        for h in range(2):
            prev_ref[h, pl.ds(rf, batch), :] = jnp.where(
                is_fwd, st_ref[h, pl.ds(rf, batch), :], st_ref[h, pl.ds(rb, batch), :])
        return carry

    lax.fori_loop(0, n_chunks, gather_step, 0, unroll=4)

    mt = mt_ref[0]
    woutt = woutt_ref[0]
    for b in range(batch):
        cols = slice(b * n_chunks, (b + 1) * n_chunks)
        pb = jnp.concatenate([prev_ref[0, pl.ds(b, n_chunks, stride=batch), :],
                              prev_ref[1, pl.ds(b, n_chunks, stride=batch), :]], axis=1)
        yb = jnp.dot(mt, ut[:, cols], preferred_element_type=F32)
        yb = yb + lax.dot_general(woutt, pb.astype(BF16), NT_DIMS, preferred_element_type=F32)
        y_ref[:, :, cols] = yb.reshape(CHUNK, SSM_GROUP, n_chunks)


def _s5_t(ut, mt, wst, woutt, aq, *, n_chunks, batch):
    n = batch * n_chunks
    kern = functools.partial(_s5_kernel, n_chunks=n_chunks, batch=batch)
    op_spec = pl.BlockSpec((1, CW, CW), lambda g: (g, 0, 0))
    act_spec = pl.BlockSpec((CHUNK, SSM_GROUP, n), lambda g: (0, g, 0))
    return pl.pallas_call(
        kern,
        grid=(N_GROUPS,),
        in_specs=[act_spec, op_spec, op_spec, op_spec,
                  pl.BlockSpec((1, 2, HALF), lambda g: (g, 0, 0))],
        out_specs=act_spec,
        out_shape=jax.ShapeDtypeStruct((CHUNK, D_MODEL, n), F32),
        scratch_shapes=[
            pltpu.VMEM((2, n, HALF), F32),
            pltpu.VMEM((2, n + batch, HALF), F32),
            pltpu.VMEM((2, n, HALF), F32),
        ],
        compiler_params=_params(1),
        name="s5_t",
    )(ut, mt, wst, woutt, aq)


def _s5_operators(lam_re, lam_im, log_dt, b_re, b_im, c_re, c_im, ssm_d):
    dt = jnp.exp(log_dt)[..., None]
    mag = jnp.exp(lam_re * dt)
    ab_re, ab_im = mag * jnp.cos(lam_im * dt), mag * jnp.sin(lam_im * dt)
    den = lam_re * lam_re + lam_im * lam_im
    nr, ni = ab_re - 1.0, ab_im
    coef_re = (nr * lam_re + ni * lam_im) / den
    coef_im = (ni * lam_re - nr * lam_im) / den
    bb_re = coef_re[..., None] * b_re - coef_im[..., None] * b_im
    bb_im = coef_re[..., None] * b_im + coef_im[..., None] * b_re

    tau = jnp.arange(CHUNK + 1, dtype=F32)[:, None, None, None]
    pmag = jnp.exp(tau * (lam_re * dt)[None])
    pw_re = pmag * jnp.cos(tau * (lam_im * dt)[None])
    pw_im = pmag * jnp.sin(tau * (lam_im * dt)[None])

    ca_re = c_re[None] * pw_re[:, :, :, None, :] - c_im[None] * pw_im[:, :, :, None, :]
    ca_im = c_re[None] * pw_im[:, :, :, None, :] + c_im[None] * pw_re[:, :, :, None, :]
    hp = lax.Precision.HIGHEST
    k_tau = (jnp.einsum('tdgop,dgpi->tdgoi', ca_re[:CHUNK], bb_re, precision=hp)
             - jnp.einsum('tdgop,dgpi->tdgoi', ca_im[:CHUNK], bb_im, precision=hp))

    s_idx = jnp.arange(CHUNK)[:, None]
    t_idx = jnp.arange(CHUNK)[None, :]
    lag = t_idx - s_idx
    kf = k_tau[:, 0][jnp.clip(lag, 0, CHUNK - 1)]
    kb = k_tau[:, 1][jnp.clip(-lag, 0, CHUNK - 1)]
    m = (jnp.where((lag >= 0)[:, :, None, None, None], kf, 0.0)
         + jnp.where((lag <= 0)[:, :, None, None, None], kb, 0.0))
    d_diag = ssm_d.reshape(N_GROUPS, SSM_GROUP)[:, :, None] * jnp.eye(SSM_GROUP, dtype=F32)
    m = m + jnp.where((lag == 0)[:, :, None, None, None], d_diag[None, None], 0.0)
    mt = m.transpose(2, 1, 3, 0, 4).reshape(N_GROUPS, CW, CW)

    rev = jnp.arange(CHUNK - 1, -1, -1)
    pf_re, pf_im = pw_re[rev, 0], pw_im[rev, 0]
    pb_re, pb_im = pw_re[:CHUNK, 1], pw_im[:CHUNK, 1]

    def outer(p_re, p_im, d):
        re = p_re[:, :, :, None] * bb_re[d][None] - p_im[:, :, :, None] * bb_im[d][None]
        im = p_re[:, :, :, None] * bb_im[d][None] + p_im[:, :, :, None] * bb_re[d][None]
        return re.transpose(1, 0, 3, 2), im.transpose(1, 0, 3, 2)

    wf_re, wf_im = outer(pf_re, pf_im, 0)
    wb_re, wb_im = outer(pb_re, pb_im, 1)
    wst = jnp.concatenate([wf_re, wb_re, wf_im, wb_im], axis=-1).reshape(N_GROUPS, CW, SW)

    of_re, of_im = ca_re[1:, 0], ca_im[1:, 0]
    ob_re, ob_im = ca_re[rev + 1, 1], ca_im[rev + 1, 1]
    woutt = jnp.concatenate([of_re, ob_re, -of_im, -ob_im], axis=-1)
    woutt = woutt.transpose(1, 0, 2, 3).reshape(N_GROUPS, CW, SW)

    aq = jnp.stack([jnp.concatenate([pw_re[CHUNK, 0], pw_re[CHUNK, 1]], axis=-1),
                    jnp.concatenate([pw_im[CHUNK, 0], pw_im[CHUNK, 1]], axis=-1)], axis=1)
    return mt.astype(BF16), wst.astype(BF16), woutt.astype(BF16), aq


def _block_kernel(x_ref, yt_ref, cv_ref, cvprev_ref, cvnext_ref, ng_ref, w5_ref, wglu_ref, wa_ref,
                  wb_ref, wo_ref, cw_ref, cb_ref, fg_ref, o_ref, xn_ref, *, n_chunks):
    tp = pl.program_id(1)
    x = _load_sets(x_ref)
    xn_ref[...] = _rmsnorm(x, ng_ref[...]).astype(BF16)

    def proj(k):
        return jnp.dot(xn_ref[...], w5_ref[:, k * D_MODEL:(k + 1) * D_MODEL],
                       preferred_element_type=F32)

    row = lax.broadcasted_iota(jnp.int32, (n_chunks, D_MODEL), 0)
    cvprev = cvprev_ref[...].astype(F32)
    cvnext = cvnext_ref[...].astype(F32)
    cvprev = jnp.where(tp == 0, jnp.where(row == 0, 0.0, pltpu.roll(cvprev, 1, 0)), cvprev)
    cvnext = jnp.where(tp == CHUNK // SETS_PER_STEP - 1,
                       jnp.where(row == n_chunks - 1, 0.0, pltpu.roll(cvnext, n_chunks - 1, 0)),
                       cvnext)
    cvs = [cv_ref[:, k * D_MODEL:(k + 1) * D_MODEL].astype(F32) for k in range(SETS_PER_STEP)]
    cvm = jnp.concatenate([cvprev] + cvs[:-1], axis=0)
    cvp = jnp.concatenate(cvs[1:] + [cvnext], axis=0)
    cw = cw_ref[...]
    conv = cb_ref[...] + cvm * cw[0:1] + jnp.concatenate(cvs, axis=0) * cw[1:2] + cvp * cw[2:3]

    z_b = proj(2)
    y_b = (proj(1) * conv * (z_b * _sigmoid(z_b))).astype(BF16)
    merged = _sigmoid(proj(4)) * jnp.dot(y_b, wb_ref[...], preferred_element_type=F32)

    yg = _gelu_tanh(jnp.concatenate([yt_ref[k].T for k in range(SETS_PER_STEP)], axis=0))
    yg = yg * _sigmoid(jnp.dot(yg.astype(BF16), wglu_ref[...], preferred_element_type=F32))
    z_a = proj(0)
    y_a = (yg * (z_a * _sigmoid(z_a))).astype(BF16)
    merged = merged + _sigmoid(proj(3)) * jnp.dot(y_a, wa_ref[...], preferred_element_type=F32)

    h = x + jnp.dot(merged.astype(BF16), wo_ref[...], preferred_element_type=F32)
    out = _rmsnorm(h, fg_ref[...])
    for k in range(SETS_PER_STEP):
        o_ref[:, k * D_MODEL:(k + 1) * D_MODEL] = out[k * n_chunks:(k + 1) * n_chunks]


def _block_t(x3, yt, cv3, norm_g, w5, w_glu, w_a, w_b, w_o, cw, cb, final_g, *, bsz, n_chunks):
    s = SETS_PER_STEP
    tok_spec = pl.BlockSpec((None, n_chunks, s * D_MODEL), lambda b, tp: (b, 0, tp))
    prev_spec = pl.BlockSpec((None, n_chunks, D_MODEL),
                             lambda b, tp: (b, 0, (tp * s + CHUNK - 1) % CHUNK))
    next_spec = pl.BlockSpec((None, n_chunks, D_MODEL), lambda b, tp: (b, 0, (tp * s + s) % CHUNK))
    kern = functools.partial(_block_kernel, n_chunks=n_chunks)
    return pl.pallas_call(
        kern,
        grid=(bsz, CHUNK // s),
        in_specs=[
            tok_spec,
            pl.BlockSpec((s, D_MODEL, n_chunks), lambda b, tp: (tp, 0, b)),
            tok_spec, prev_spec, next_spec,
            _const_spec((1, D_MODEL)),
            _const_spec((D_MODEL, 5 * D_MODEL)),
            _const_spec((D_MODEL, D_MODEL)),
            _const_spec((D_MODEL, D_MODEL)),
            _const_spec((D_MODEL, D_MODEL)),
            _const_spec((D_MODEL, D_MODEL)),
            _const_spec((3, D_MODEL)),
            _const_spec((1, D_MODEL)),
            _const_spec((1, D_MODEL)),
        ],
        out_specs=tok_spec,
        out_shape=jax.ShapeDtypeStruct(x3.shape, F32),
        scratch_shapes=[pltpu.VMEM((s * n_chunks, D_MODEL), BF16)],
        compiler_params=_params(2),
        name="block_t",
    )(x3, yt, cv3, cv3, cv3, norm_g, w5, w_glu, w_a, w_b, w_o, cw, cb, final_g)


def kernel(x, norm_g, w_in, lam_re, lam_im, log_dt, ssm_b_re, ssm_b_im, ssm_c_re, ssm_c_im, ssm_d,
           w_glu, conv_w, conv_b, w_branch_a, w_branch_b, w_out, final_g):
    bsz, seq_len, d = x.shape
    assert norm_g.shape[0] == 1 and d == D_MODEL and bsz == SUBLANES
    n_chunks = seq_len // CHUNK
    assert seq_len == n_chunks * CHUNK and n_chunks % LANES == 0

    x3 = x.reshape(bsz, n_chunks, CHUNK * d)
    ng = norm_g[0].reshape(1, d)
    w8 = w_in[0].astype(BF16).reshape(d, 8, d)
    wut = w8[:, 0].T
    wvc = w8[:, jnp.array([2, 4])].reshape(d, 2 * d)
    w5 = w8[:, jnp.array([1, 3, 5, 6, 7])].reshape(d, 5 * d)

    ut, cv3 = _in_proj_t(x3, ng, wut, wvc, bsz=bsz, n_chunks=n_chunks)
    mt, wst, woutt, aq = _s5_operators(
        lam_re[0], lam_im[0], log_dt[0], ssm_b_re[0], ssm_b_im[0], ssm_c_re[0], ssm_c_im[0],
        ssm_d[0])
    yt = _s5_t(ut, mt, wst, woutt, aq, n_chunks=n_chunks, batch=bsz)

    out3 = _block_t(
        x3, yt, cv3, ng, w5, w_glu[0].astype(BF16), w_branch_a[0].astype(BF16),
        w_branch_b[0].astype(BF16), w_out[0].astype(BF16), conv_w[0], conv_b[0].reshape(1, d),
        final_g.reshape(1, d), bsz=bsz, n_chunks=n_chunks)
    return out3.reshape(bsz, seq_len, d)
```

```python
import functools
import math

import jax
import jax.numpy as jnp
from jax import lax
from jax.experimental import pallas as pl
from jax.experimental.pallas import tpu as pltpu

F32 = jnp.float32
BF16 = jnp.bfloat16

D_MODEL = 1024
SSM_GROUP = 16
N_GROUPS = D_MODEL // SSM_GROUP
STATE = 64
CHUNK = 16
CW = CHUNK * SSM_GROUP
SW = 4 * STATE
HALF = 2 * STATE
LANES = 128
EPS = 1e-6
SUBLANES = 8
SETS_PER_STEP = 2
VMEM_LIMIT_BYTES = 60 * 1024 * 1024

NT_DIMS = (((1,), (1,)), ((), ()))
TN_DIMS = (((0,), (0,)), ((), ()))


def _rmsnorm(x, g):
    ms = jnp.mean(x * x, axis=-1, keepdims=True)
    return x * lax.rsqrt(ms + EPS) * g


def _sigmoid(x):
    return 0.5 * jnp.tanh(0.5 * x) + 0.5


def _gelu_tanh(x):
    c = math.sqrt(2.0 / math.pi)
    return x * (0.5 * (1.0 + jnp.tanh(c * (x + 0.044715 * (x * x * x)))))


def _params(n_axes):
    return pltpu.CompilerParams(
        dimension_semantics=("arbitrary",) * n_axes, vmem_limit_bytes=VMEM_LIMIT_BYTES)


def _const_spec(shape):
    return pl.BlockSpec(shape, lambda *_: (0,) * len(shape), pipeline_mode=pl.Buffered(1))


def _load_sets(ref):
    return jnp.concatenate(
        [ref[:, k * D_MODEL:(k + 1) * D_MODEL] for k in range(SETS_PER_STEP)], axis=0)


def _in_proj_kernel(x_ref, g_ref, wut_ref, wvc_ref, u_ref, cv_ref, *, n_chunks):
    xn = _rmsnorm(_load_sets(x_ref), g_ref[...]).astype(BF16)
    ut = lax.dot_general(wut_ref[...], xn, NT_DIMS, preferred_element_type=F32)
    vc = jnp.dot(xn, wvc_ref[...], preferred_element_type=F32)
    cv = (vc[:, :D_MODEL] * vc[:, D_MODEL:]).astype(BF16)
    for k in range(SETS_PER_STEP):
        u_ref[k] = ut[:, k * n_chunks:(k + 1) * n_chunks].astype(BF16)
        cv_ref[:, k * D_MODEL:(k + 1) * D_MODEL] = cv[k * n_chunks:(k + 1) * n_chunks]


def _in_proj_t(x3, norm_g, wut, wvc, *, bsz, n_chunks):
    tok_spec = pl.BlockSpec((None, n_chunks, SETS_PER_STEP * D_MODEL), lambda b, tp: (b, 0, tp))
    return pl.pallas_call(
        functools.partial(_in_proj_kernel, n_chunks=n_chunks),
        grid=(bsz, CHUNK // SETS_PER_STEP),
        in_specs=[tok_spec, _const_spec((1, D_MODEL)), _const_spec((D_MODEL, D_MODEL)),
                  _const_spec((D_MODEL, 2 * D_MODEL))],
        out_specs=[pl.BlockSpec((SETS_PER_STEP, D_MODEL, n_chunks), lambda b, tp: (tp, 0, b)),
                   tok_spec],
        out_shape=[jax.ShapeDtypeStruct((CHUNK, D_MODEL, bsz * n_chunks), BF16),
                   jax.ShapeDtypeStruct(x3.shape, BF16)],
        compiler_params=_params(2),
        name="in_proj_t",
    )(x3, norm_g, wut, wvc)


def _s5_kernel(u_ref, mt_ref, wst_ref, woutt_ref, aq_ref, y_ref, sloc_ref, st_ref, prev_ref,
               *, n_chunks, batch):
    n = batch * n_chunks
    ut = u_ref[...].reshape(CW, n)
    wst = wst_ref[0]

    for b in range(batch):
        sl = lax.dot_general(ut[:, b * n_chunks:(b + 1) * n_chunks], wst, TN_DIMS,
                             preferred_element_type=F32)
        sloc_ref[0, pl.ds(b, n_chunks, stride=batch), :] = sl[:, :HALF]
        sloc_ref[1, pl.ds(b, n_chunks, stride=batch), :] = sl[:, HALF:]

    aq = aq_ref[0]
    ar = jnp.broadcast_to(aq[0:1, :], (batch, HALF))
    ai = jnp.broadcast_to(aq[1:2, :], (batch, HALF))
    is_fwd = lax.broadcasted_iota(jnp.int32, (batch, HALF), 1) < STATE

    zero = jnp.zeros((batch, HALF), F32)
    st_ref[0, 0:batch, :] = zero
    st_ref[1, 0:batch, :] = zero

    def scan_step(k, carry):
        sr, si = carry
        rf = pl.multiple_of(k * batch, batch)
        rb = pl.multiple_of((n_chunks - 1 - k) * batch, batch)
        xr = jnp.where(is_fwd, sloc_ref[0, pl.ds(rf, batch), :], sloc_ref[0, pl.ds(rb, batch), :])
        xi = jnp.where(is_fwd, sloc_ref[1, pl.ds(rf, batch), :], sloc_ref[1, pl.ds(rb, batch), :])
        nr = ar * sr - ai * si + xr
        ni = ar * si + ai * sr + xi
        ro = pl.multiple_of((k + 1) * batch, batch)
        st_ref[0, pl.ds(ro, batch), :] = nr
        st_ref[1, pl.ds(ro, batch), :] = ni
        return nr, ni

    lax.fori_loop(0, n_chunks, scan_step, (zero, zero), unroll=4)

    def gather_step(c, carry):
        rf = pl.multiple_of(c * batch, batch)
        rb = pl.multiple_of((n_chunks - 1 - c) * batch, batch)
        for h in range(2):
            prev_ref[h, pl.ds(rf, batch), :] = jnp.where(
                is_fwd, st_ref[h, pl.ds(rf, batch), :], st_ref[h, pl.ds(rb, batch), :])
        return carry

    lax.fori_loop(0, n_chunks, gather_step, 0, unroll=4)

    mt = mt_ref[0]
    woutt = woutt_ref[0]
    for b in range(batch):
        cols = slice(b * n_chunks, (b + 1) * n_chunks)
        pb = jnp.concatenate([prev_ref[0, pl.ds(b, n_chunks, stride=batch), :],
                              prev_ref[1, pl.ds(b, n_chunks, stride=batch), :]], axis=1)
        yb = jnp.dot(mt, ut[:, cols], preferred_element_type=F32)
        yb = yb + lax.dot_general(woutt, pb.astype(BF16), NT_DIMS, preferred_element_type=F32)
        y_ref[:, :, cols] = yb.reshape(CHUNK, SSM_GROUP, n_chunks)


def _s5_t(ut, mt, wst, woutt, aq, *, n_chunks, batch):
    n = batch * n_chunks
    kern = functools.partial(_s5_kernel, n_chunks=n_chunks, batch=batch)
    op_spec = pl.BlockSpec((1, CW, CW), lambda g: (g, 0, 0))
    act_spec = pl.BlockSpec((CHUNK, SSM_GROUP, n), lambda g: (0, g, 0))
    return pl.pallas_call(
        kern,
        grid=(N_GROUPS,),
        in_specs=[act_spec, op_spec, op_spec, op_spec,
                  pl.BlockSpec((1, 2, HALF), lambda g: (g, 0, 0))],
        out_specs=act_spec,
        out_shape=jax.ShapeDtypeStruct((CHUNK, D_MODEL, n), F32),
        scratch_shapes=[
            pltpu.VMEM((2, n, HALF), F32),
            pltpu.VMEM((2, n + batch, HALF), F32),
            pltpu.VMEM((2, n, HALF), F32),
        ],
        compiler_params=_params(1),
        name="s5_t",
    )(ut, mt, wst, woutt, aq)


def _s5_ops_kernel(lam_ref, c_ref, bt_ref, dd_ref, mt_ref, wst_ref, woutt_ref, aq_ref,
                   pwr_ref, pwi_ref):
    lr, li = lam_ref[0, 0:1, :], lam_ref[0, 1:2, :]
    dt = jnp.exp(lam_ref[0, 2:3, :])
    c_re, c_im = c_ref[0, 0], c_ref[0, 1]
    bt_re, bt_im = bt_ref[0, 0], bt_ref[0, 1]

    n_rows = 2 * CHUNK + SUBLANES
    r = lax.broadcasted_iota(jnp.int32, (n_rows, HALF), 0)
    fwd = lax.broadcasted_iota(jnp.int32, (n_rows, HALF), 1) < STATE
    s = r - CHUNK
    e = jnp.where(r < CHUNK, jnp.where(fwd, r + 1, CHUNK - r),
                  jnp.where(r < 2 * CHUNK, jnp.where(fwd, CHUNK - 1 - s, s),
                            jnp.where(r == 2 * CHUNK, CHUNK, 1))).astype(F32)
    mag = jnp.exp(e * (lr * dt))
    ang = e * (li * dt)
    pwr_ref[...] = mag * jnp.cos(ang)
    pwi_ref[...] = mag * jnp.sin(ang)

    def power(row):
        return pwr_ref[row:row + 1, :], pwi_ref[row:row + 1, :]

    a_re, a_im = power(2 * CHUNK + 1)
    den = lr * lr + li * li
    nr, ni = a_re - 1.0, a_im
    coef_re = (nr * lr + ni * li) / den
    coef_im = (ni * lr - nr * li) / den
    bb_re = coef_re * bt_re - coef_im * bt_im
    bb_im = coef_re * bt_im + coef_im * bt_re

    st_re, st_im = [], []
    for k in range(CHUNK):
        p_re, p_im = power(CHUNK + k)
        st_re.append(p_re * bb_re - p_im * bb_im)
        st_im.append(p_re * bb_im + p_im * bb_re)
    wst = jnp.concatenate([jnp.concatenate(st_re, axis=0), jnp.concatenate(st_im, axis=0)], axis=1)
    wst_ref[0] = wst.astype(BF16)

    out_rows = []
    for k in range(CHUNK):
        p_re, p_im = power(k)
        out_rows.append(jnp.concatenate([c_re * p_re - c_im * p_im, -(c_re * p_im + c_im * p_re)],
                                        axis=1))
    woutt_ref[0] = jnp.concatenate(out_rows, axis=0).astype(BF16)

    aq_ref[0, 0:1, :], aq_ref[0, 1:2, :] = power(2 * CHUNK)

    fwd1 = lax.broadcasted_iota(jnp.int32, (SSM_GROUP, HALF), 1) < STATE
    zero = jnp.zeros_like(c_re)
    cmat = jnp.concatenate([
        jnp.concatenate([jnp.where(fwd1, c_re, zero), -jnp.where(fwd1, c_im, zero)], axis=1),
        jnp.concatenate([jnp.where(fwd1, zero, c_re), -jnp.where(fwd1, zero, c_im)], axis=1)], axis=0)
    kslab = lax.dot_general(cmat, wst, NT_DIMS, precision=lax.Precision.HIGHEST,
                            preferred_element_type=F32)
    k_f = kslab[:SSM_GROUP]
    k_b = kslab[SSM_GROUP:] + dd_ref[0]

    lane = lax.broadcasted_iota(jnp.int32, (SSM_GROUP, CW), 1)
    m_rows = []
    for t in range(CHUNK):
        left = SSM_GROUP * (CHUNK - 1 - t)
        right = SSM_GROUP * t
        a = k_f if left == 0 else jnp.where(lane < CW - left, pltpu.roll(k_f, CW - left, 1), 0.0)
        b = k_b if right == 0 else jnp.where(lane >= right, pltpu.roll(k_b, right, 1), 0.0)
        m_rows.append(a + b)
    mt_ref[0] = jnp.concatenate(m_rows, axis=0).astype(BF16)


def _s5_operators(lam_re, lam_im, log_dt, b_re, b_im, c_re, c_im, ssm_d):
    def lanes(x):
        return jnp.concatenate([x[0], x[1]], axis=-1)

    lam = jnp.stack([lanes(lam_re), lanes(lam_im),
                     lanes(jnp.broadcast_to(log_dt[..., None], lam_re.shape))], axis=1)
    c = jnp.stack([lanes(c_re), lanes(c_im)], axis=1)
    bt = jnp.stack([lanes(b_re.transpose(0, 1, 3, 2)), lanes(b_im.transpose(0, 1, 3, 2))], axis=1)
    dd = ssm_d.reshape(N_GROUPS, SSM_GROUP)[:, :, None] * jnp.eye(SSM_GROUP, CW, dtype=F32)

    op_spec = pl.BlockSpec((1, CW, CW), lambda g: (g, 0, 0))
    par_spec = pl.BlockSpec((1, 2, SSM_GROUP, HALF), lambda g: (g, 0, 0, 0))
    op_shape = jax.ShapeDtypeStruct((N_GROUPS, CW, CW), BF16)
    return pl.pallas_call(
        _s5_ops_kernel,
        grid=(N_GROUPS,),
        in_specs=[pl.BlockSpec((1, 3, HALF), lambda g: (g, 0, 0)), par_spec, par_spec,
                  pl.BlockSpec((1, SSM_GROUP, CW), lambda g: (g, 0, 0))],
        out_specs=[op_spec, op_spec, op_spec, pl.BlockSpec((1, 2, HALF), lambda g: (g, 0, 0))],
        out_shape=[op_shape, op_shape, op_shape, jax.ShapeDtypeStruct((N_GROUPS, 2, HALF), F32)],
        scratch_shapes=[pltpu.VMEM((2 * CHUNK + SUBLANES, HALF), F32)] * 2,
        compiler_params=_params(1),
        name="s5_ops",
    )(lam, c, bt, dd)


def _block_kernel(x_ref, yt_ref, cv_ref, cvprev_ref, cvnext_ref, ng_ref, w5_ref, wglu_ref, wa_ref,
                  wb_ref, wo_ref, cw_ref, cb_ref, fg_ref, o_ref, xn_ref, *, n_chunks):
    tp = pl.program_id(1)
    x = _load_sets(x_ref)
    xn_ref[...] = _rmsnorm(x, ng_ref[...]).astype(BF16)

    def proj(k):
        return jnp.dot(xn_ref[...], w5_ref[:, k * D_MODEL:(k + 1) * D_MODEL],
                       preferred_element_type=F32)

    row = lax.broadcasted_iota(jnp.int32, (n_chunks, D_MODEL), 0)
    cvprev = cvprev_ref[...].astype(F32)
    cvnext = cvnext_ref[...].astype(F32)
    cvprev = jnp.where(tp == 0, jnp.where(row == 0, 0.0, pltpu.roll(cvprev, 1, 0)), cvprev)
    cvnext = jnp.where(tp == CHUNK // SETS_PER_STEP - 1,
                       jnp.where(row == n_chunks - 1, 0.0, pltpu.roll(cvnext, n_chunks - 1, 0)),
                       cvnext)
    cvs = [cv_ref[:, k * D_MODEL:(k + 1) * D_MODEL].astype(F32) for k in range(SETS_PER_STEP)]
    cvm = jnp.concatenate([cvprev] + cvs[:-1], axis=0)
    cvp = jnp.concatenate(cvs[1:] + [cvnext], axis=0)
    cw = cw_ref[...]
    conv = cb_ref[...] + cvm * cw[0:1] + jnp.concatenate(cvs, axis=0) * cw[1:2] + cvp * cw[2:3]

    z_b = proj(2)
    y_b = (proj(1) * conv * (z_b * _sigmoid(z_b))).astype(BF16)
    merged = _sigmoid(proj(4)) * jnp.dot(y_b, wb_ref[...], preferred_element_type=F32)

    yg = _gelu_tanh(jnp.concatenate([yt_ref[k].T for k in range(SETS_PER_STEP)], axis=0))
    yg = yg * _sigmoid(jnp.dot(yg.astype(BF16), wglu_ref[...], preferred_element_type=F32))
    z_a = proj(0)
    y_a = (yg * (z_a * _sigmoid(z_a))).astype(BF16)
    merged = merged + _sigmoid(proj(3)) * jnp.dot(y_a, wa_ref[...], preferred_element_type=F32)

    h = x + jnp.dot(merged.astype(BF16), wo_ref[...], preferred_element_type=F32)
    out = _rmsnorm(h, fg_ref[...])
    for k in range(SETS_PER_STEP):
        o_ref[:, k * D_MODEL:(k + 1) * D_MODEL] = out[k * n_chunks:(k + 1) * n_chunks]


def _block_t(x3, yt, cv3, norm_g, w5, w_glu, w_a, w_b, w_o, cw, cb, final_g, *, bsz, n_chunks):
    s = SETS_PER_STEP
    tok_spec = pl.BlockSpec((None, n_chunks, s * D_MODEL), lambda b, tp: (b, 0, tp))
    prev_spec = pl.BlockSpec((None, n_chunks, D_MODEL),
                             lambda b, tp: (b, 0, (tp * s + CHUNK - 1) % CHUNK))
    next_spec = pl.BlockSpec((None, n_chunks, D_MODEL), lambda b, tp: (b, 0, (tp * s + s) % CHUNK))
    kern = functools.partial(_block_kernel, n_chunks=n_chunks)
    return pl.pallas_call(
        kern,
        grid=(bsz, CHUNK // s),
        in_specs=[
            tok_spec,
            pl.BlockSpec((s, D_MODEL, n_chunks), lambda b, tp: (tp, 0, b)),
            tok_spec, prev_spec, next_spec,
            _const_spec((1, D_MODEL)),
            _const_spec((D_MODEL, 5 * D_MODEL)),
            _const_spec((D_MODEL, D_MODEL)),
            _const_spec((D_MODEL, D_MODEL)),
            _const_spec((D_MODEL, D_MODEL)),
            _const_spec((D_MODEL, D_MODEL)),
            _const_spec((3, D_MODEL)),
            _const_spec((1, D_MODEL)),
            _const_spec((1, D_MODEL)),
        ],
        out_specs=tok_spec,
        out_shape=jax.ShapeDtypeStruct(x3.shape, F32),
        scratch_shapes=[pltpu.VMEM((s * n_chunks, D_MODEL), BF16)],
        compiler_params=_params(2),
        name="block_t",
    )(x3, yt, cv3, cv3, cv3, norm_g, w5, w_glu, w_a, w_b, w_o, cw, cb, final_g)


def kernel(x, norm_g, w_in, lam_re, lam_im, log_dt, ssm_b_re, ssm_b_im, ssm_c_re, ssm_c_im, ssm_d,
           w_glu, conv_w, conv_b, w_branch_a, w_branch_b, w_out, final_g):
    bsz, seq_len, d = x.shape
    assert norm_g.shape[0] == 1 and d == D_MODEL and bsz == SUBLANES
    n_chunks = seq_len // CHUNK
    assert seq_len == n_chunks * CHUNK and n_chunks % LANES == 0

    x3 = x.reshape(bsz, n_chunks, CHUNK * d)
    ng = norm_g[0].reshape(1, d)
    w8 = w_in[0].astype(BF16).reshape(d, 8, d)
    wut = w8[:, 0].T
    wvc = w8[:, jnp.array([2, 4])].reshape(d, 2 * d)
    w5 = w8[:, jnp.array([1, 3, 5, 6, 7])].reshape(d, 5 * d)

    ut, cv3 = _in_proj_t(x3, ng, wut, wvc, bsz=bsz, n_chunks=n_chunks)
    mt, wst, woutt, aq = _s5_operators(
        lam_re[0], lam_im[0], log_dt[0], ssm_b_re[0], ssm_b_im[0], ssm_c_re[0], ssm_c_im[0],
        ssm_d[0])
    yt = _s5_t(ut, mt, wst, woutt, aq, n_chunks=n_chunks, batch=bsz)

    out3 = _block_t(
        x3, yt, cv3, ng, w5, w_glu[0].astype(BF16), w_branch_a[0].astype(BF16),
        w_branch_b[0].astype(BF16), w_out[0].astype(BF16), conv_w[0], conv_b[0].reshape(1, d),
        final_g.reshape(1, d), bsz=bsz, n_chunks=n_chunks)
    return out3.reshape(bsz, seq_len, d)
```

```python
import functools
import math

import jax
import jax.numpy as jnp
from jax import lax
from jax.experimental import pallas as pl
from jax.experimental.pallas import tpu as pltpu

F32 = jnp.float32
BF16 = jnp.bfloat16

D_MODEL = 1024
SSM_GROUP = 16
N_GROUPS = D_MODEL // SSM_GROUP
STATE = 64
CHUNK = 16
CW = CHUNK * SSM_GROUP
SW = 4 * STATE
HALF = 2 * STATE
LANES = 128
EPS = 1e-6
SUBLANES = 8
SETS_PER_STEP = 2
VMEM_LIMIT_BYTES = 60 * 1024 * 1024

NT_DIMS = (((1,), (1,)), ((), ()))
TN_DIMS = (((0,), (0,)), ((), ()))


def _rmsnorm(x, g):
    ms = jnp.mean(x * x, axis=-1, keepdims=True)
    return x * lax.rsqrt(ms + EPS) * g


def _sigmoid(x):
    return 0.5 * jnp.tanh(0.5 * x) + 0.5


def _gelu_tanh(x):
    c = math.sqrt(2.0 / math.pi)
    return x * (0.5 * (1.0 + jnp.tanh(c * (x + 0.044715 * (x * x * x)))))


def _params(n_axes):
    return pltpu.CompilerParams(
        dimension_semantics=("arbitrary",) * n_axes, vmem_limit_bytes=VMEM_LIMIT_BYTES)


def _const_spec(shape):
    return pl.BlockSpec(shape, lambda *_: (0,) * len(shape), pipeline_mode=pl.Buffered(1))


def _set_copies(x_hbm, buf, sem, step, slot, *, to_hbm=False):
    steps_per_batch = CHUNK // SETS_PER_STEP
    b = step // steps_per_batch
    tl0 = (step % steps_per_batch) * SETS_PER_STEP
    copies = []
    for k in range(SETS_PER_STEP):
        hbm, vmem = x_hbm.at[b, :, tl0 + k, :], buf.at[slot, k]
        src, dst = (vmem, hbm) if to_hbm else (hbm, vmem)
        copies.append(pltpu.make_async_copy(src, dst, sem.at[slot, k]))
    return copies


def _fetch_sets(x_hbm, xbuf, sem, step, n_steps):
    slot = step % 2

    @pl.when(step == 0)
    def _():
        for c in _set_copies(x_hbm, xbuf, sem, step, slot):
            c.start()

    @pl.when(step + 1 < n_steps)
    def _():
        for c in _set_copies(x_hbm, xbuf, sem, step + 1, 1 - slot):
            c.start()

    for c in _set_copies(x_hbm, xbuf, sem, step, slot):
        c.wait()
    return jnp.concatenate([xbuf[slot, k] for k in range(SETS_PER_STEP)], axis=0)


def _set_buffers(n_chunks):
    return [pltpu.VMEM((2, SETS_PER_STEP, n_chunks, D_MODEL), F32),
            pltpu.SemaphoreType.DMA((2, SETS_PER_STEP))]


def _in_proj_kernel(x_hbm, g_ref, wut_ref, wvc_ref, u_ref, cv_ref, xbuf, xsem, *, n_chunks):
    step = pl.program_id(0) * pl.num_programs(1) + pl.program_id(1)
    n_steps = pl.num_programs(0) * pl.num_programs(1)
    x = _fetch_sets(x_hbm, xbuf, xsem, step, n_steps)
    xn = _rmsnorm(x, g_ref[...]).astype(BF16)
    ut = lax.dot_general(wut_ref[...], xn, NT_DIMS, preferred_element_type=F32)
    vc = jnp.dot(xn, wvc_ref[...], preferred_element_type=F32)
    cv = (vc[:, :D_MODEL] * vc[:, D_MODEL:]).astype(BF16)
    for k in range(SETS_PER_STEP):
        u_ref[k] = ut[:, k * n_chunks:(k + 1) * n_chunks].astype(BF16)
        cv_ref[:, k * D_MODEL:(k + 1) * D_MODEL] = cv[k * n_chunks:(k + 1) * n_chunks]


def _in_proj_t(x4, norm_g, wut, wvc, *, bsz, n_chunks):
    tok_spec = pl.BlockSpec((None, n_chunks, SETS_PER_STEP * D_MODEL), lambda b, tp: (b, 0, tp))
    return pl.pallas_call(
        functools.partial(_in_proj_kernel, n_chunks=n_chunks),
        grid=(bsz, CHUNK // SETS_PER_STEP),
        in_specs=[pl.BlockSpec(memory_space=pl.ANY), _const_spec((1, D_MODEL)),
                  _const_spec((D_MODEL, D_MODEL)), _const_spec((D_MODEL, 2 * D_MODEL))],
        out_specs=[pl.BlockSpec((SETS_PER_STEP, D_MODEL, n_chunks), lambda b, tp: (tp, 0, b)),
                   tok_spec],
        out_shape=[jax.ShapeDtypeStruct((CHUNK, D_MODEL, bsz * n_chunks), BF16),
                   jax.ShapeDtypeStruct((bsz, n_chunks, CHUNK * D_MODEL), BF16)],
        scratch_shapes=_set_buffers(n_chunks),
        compiler_params=_params(2),
        name="in_proj_t",
    )(x4, norm_g, wut, wvc)


def _s5_kernel(u_ref, mt_ref, wst_ref, woutt_ref, aq_ref, y_ref, sloc_ref, st_ref, prev_ref,
               *, n_chunks, batch):
    n = batch * n_chunks
    ut = u_ref[...].reshape(CW, n)
    wst = wst_ref[0]

    for b in range(batch):
        sl = lax.dot_general(ut[:, b * n_chunks:(b + 1) * n_chunks], wst, TN_DIMS,
                             preferred_element_type=F32)
        sloc_ref[0, pl.ds(b, n_chunks, stride=batch), :] = sl[:, :HALF]
        sloc_ref[1, pl.ds(b, n_chunks, stride=batch), :] = sl[:, HALF:]

    aq = aq_ref[0]
    ar = jnp.broadcast_to(aq[0:1, :], (batch, HALF))
    ai = jnp.broadcast_to(aq[1:2, :], (batch, HALF))
    is_fwd = lax.broadcasted_iota(jnp.int32, (batch, HALF), 1) < STATE

    zero = jnp.zeros((batch, HALF), F32)
    st_ref[0, 0:batch, :] = zero
    st_ref[1, 0:batch, :] = zero

    def scan_step(k, carry):
        sr, si = carry
        rf = pl.multiple_of(k * batch, batch)
        rb = pl.multiple_of((n_chunks - 1 - k) * batch, batch)
        xr = jnp.where(is_fwd, sloc_ref[0, pl.ds(rf, batch), :], sloc_ref[0, pl.ds(rb, batch), :])
        xi = jnp.where(is_fwd, sloc_ref[1, pl.ds(rf, batch), :], sloc_ref[1, pl.ds(rb, batch), :])
        nr = ar * sr - ai * si + xr
        ni = ar * si + ai * sr + xi
        ro = pl.multiple_of((k + 1) * batch, batch)
        st_ref[0, pl.ds(ro, batch), :] = nr
        st_ref[1, pl.ds(ro, batch), :] = ni
        return nr, ni

    lax.fori_loop(0, n_chunks, scan_step, (zero, zero), unroll=4)

    def gather_step(c, carry):
        rf = pl.multiple_of(c * batch, batch)
        rb = pl.multiple_of((n_chunks - 1 - c) * batch, batch)
        for h in range(2):
            prev_ref[h, pl.ds(rf, batch), :] = jnp.where(
                is_fwd, st_ref[h, pl.ds(rf, batch), :], st_ref[h, pl.ds(rb, batch), :])
        return carry

    lax.fori_loop(0, n_chunks, gather_step, 0, unroll=4)

    mt = mt_ref[0]
    woutt = woutt_ref[0]
    for b in range(batch):
        cols = slice(b * n_chunks, (b + 1) * n_chunks)
        pb = jnp.concatenate([prev_ref[0, pl.ds(b, n_chunks, stride=batch), :],
                              prev_ref[1, pl.ds(b, n_chunks, stride=batch), :]], axis=1)
        yb = jnp.dot(mt, ut[:, cols], preferred_element_type=F32)
        yb = yb + lax.dot_general(woutt, pb.astype(BF16), NT_DIMS, preferred_element_type=F32)
        y_ref[:, :, cols] = yb.reshape(CHUNK, SSM_GROUP, n_chunks)


def _s5_t(ut, mt, wst, woutt, aq, *, n_chunks, batch):
    n = batch * n_chunks
    kern = functools.partial(_s5_kernel, n_chunks=n_chunks, batch=batch)
    op_spec = pl.BlockSpec((1, CW, CW), lambda g: (g, 0, 0))
    act_spec = pl.BlockSpec((CHUNK, SSM_GROUP, n), lambda g: (0, g, 0))
    return pl.pallas_call(
        kern,
        grid=(N_GROUPS,),
        in_specs=[act_spec, op_spec, op_spec, op_spec,
                  pl.BlockSpec((1, 2, HALF), lambda g: (g, 0, 0))],
        out_specs=act_spec,
        out_shape=jax.ShapeDtypeStruct((CHUNK, D_MODEL, n), F32),
        scratch_shapes=[
            pltpu.VMEM((2, n, HALF), F32),
            pltpu.VMEM((2, n + batch, HALF), F32),
            pltpu.VMEM((2, n, HALF), F32),
        ],
        compiler_params=_params(1),
        name="s5_t",
    )(ut, mt, wst, woutt, aq)


def _s5_ops_kernel(lam_ref, c_ref, bt_ref, dd_ref, mt_ref, wst_ref, woutt_ref, aq_ref,
                   pwr_ref, pwi_ref):
    lr, li = lam_ref[0, 0:1, :], lam_ref[0, 1:2, :]
    dt = jnp.exp(lam_ref[0, 2:3, :])
    c_re, c_im = c_ref[0, 0], c_ref[0, 1]
    bt_re, bt_im = bt_ref[0, 0], bt_ref[0, 1]

    n_rows = 2 * CHUNK + SUBLANES
    r = lax.broadcasted_iota(jnp.int32, (n_rows, HALF), 0)
    fwd = lax.broadcasted_iota(jnp.int32, (n_rows, HALF), 1) < STATE
    s = r - CHUNK
    e = jnp.where(r < CHUNK, jnp.where(fwd, r + 1, CHUNK - r),
                  jnp.where(r < 2 * CHUNK, jnp.where(fwd, CHUNK - 1 - s, s),
                            jnp.where(r == 2 * CHUNK, CHUNK, 1))).astype(F32)
    mag = jnp.exp(e * (lr * dt))
    ang = e * (li * dt)
    pwr_ref[...] = mag * jnp.cos(ang)
    pwi_ref[...] = mag * jnp.sin(ang)

    def power(row):
        return pwr_ref[row:row + 1, :], pwi_ref[row:row + 1, :]

    a_re, a_im = power(2 * CHUNK + 1)
    den = lr * lr + li * li
    nr, ni = a_re - 1.0, a_im
    coef_re = (nr * lr + ni * li) / den
    coef_im = (ni * lr - nr * li) / den
    bb_re = coef_re * bt_re - coef_im * bt_im
    bb_im = coef_re * bt_im + coef_im * bt_re

    st_re, st_im = [], []
    for k in range(CHUNK):
        p_re, p_im = power(CHUNK + k)
        st_re.append(p_re * bb_re - p_im * bb_im)
        st_im.append(p_re * bb_im + p_im * bb_re)
    wst = jnp.concatenate([jnp.concatenate(st_re, axis=0), jnp.concatenate(st_im, axis=0)], axis=1)
    wst_ref[0] = wst.astype(BF16)

    out_rows = []
    for k in range(CHUNK):
        p_re, p_im = power(k)
        out_rows.append(jnp.concatenate([c_re * p_re - c_im * p_im, -(c_re * p_im + c_im * p_re)],
                                        axis=1))
    woutt_ref[0] = jnp.concatenate(out_rows, axis=0).astype(BF16)

    aq_ref[0, 0:1, :], aq_ref[0, 1:2, :] = power(2 * CHUNK)

    fwd1 = lax.broadcasted_iota(jnp.int32, (SSM_GROUP, HALF), 1) < STATE
    zero = jnp.zeros_like(c_re)
    cmat = jnp.concatenate([
        jnp.concatenate([jnp.where(fwd1, c_re, zero), -jnp.where(fwd1, c_im, zero)], axis=1),
        jnp.concatenate([jnp.where(fwd1, zero, c_re), -jnp.where(fwd1, zero, c_im)], axis=1)], axis=0)
    kslab = lax.dot_general(cmat, wst, NT_DIMS, precision=lax.Precision.HIGHEST,
                            preferred_element_type=F32)
    k_f = kslab[:SSM_GROUP]
    k_b = kslab[SSM_GROUP:] + dd_ref[0]

    lane = lax.broadcasted_iota(jnp.int32, (SSM_GROUP, CW), 1)
    m_rows = []
    for t in range(CHUNK):
        left = SSM_GROUP * (CHUNK - 1 - t)
        right = SSM_GROUP * t
        a = k_f if left == 0 else jnp.where(lane < CW - left, pltpu.roll(k_f, CW - left, 1), 0.0)
        b = k_b if right == 0 else jnp.where(lane >= right, pltpu.roll(k_b, right, 1), 0.0)
        m_rows.append(a + b)
    mt_ref[0] = jnp.concatenate(m_rows, axis=0).astype(BF16)


def _s5_operators(lam_re, lam_im, log_dt, b_re, b_im, c_re, c_im, ssm_d):
    def lanes(x):
        return jnp.concatenate([x[0], x[1]], axis=-1)

    lam = jnp.stack([lanes(lam_re), lanes(lam_im),
                     lanes(jnp.broadcast_to(log_dt[..., None], lam_re.shape))], axis=1)
    c = jnp.stack([lanes(c_re), lanes(c_im)], axis=1)
    bt = jnp.stack([lanes(b_re.transpose(0, 1, 3, 2)), lanes(b_im.transpose(0, 1, 3, 2))], axis=1)
    dd = ssm_d.reshape(N_GROUPS, SSM_GROUP)[:, :, None] * jnp.eye(SSM_GROUP, CW, dtype=F32)

    op_spec = pl.BlockSpec((1, CW, CW), lambda g: (g, 0, 0))
    par_spec = pl.BlockSpec((1, 2, SSM_GROUP, HALF), lambda g: (g, 0, 0, 0))
    op_shape = jax.ShapeDtypeStruct((N_GROUPS, CW, CW), BF16)
    return pl.pallas_call(
        _s5_ops_kernel,
        grid=(N_GROUPS,),
        in_specs=[pl.BlockSpec((1, 3, HALF), lambda g: (g, 0, 0)), par_spec, par_spec,
                  pl.BlockSpec((1, SSM_GROUP, CW), lambda g: (g, 0, 0))],
        out_specs=[op_spec, op_spec, op_spec, pl.BlockSpec((1, 2, HALF), lambda g: (g, 0, 0))],
        out_shape=[op_shape, op_shape, op_shape, jax.ShapeDtypeStruct((N_GROUPS, 2, HALF), F32)],
        scratch_shapes=[pltpu.VMEM((2 * CHUNK + SUBLANES, HALF), F32)] * 2,
        compiler_params=_params(1),
        name="s5_ops",
    )(lam, c, bt, dd)


def _block_kernel(x_hbm, yt_ref, cv_ref, cvprev_ref, cvnext_ref, ng_ref, w5_ref, wglu_ref, wa_ref,
                  wb_ref, wo_ref, cw_ref, cb_ref, fg_ref, o_hbm, xn_ref, xbuf, xsem, obuf, osem,
                  *, n_chunks):
    tp = pl.program_id(1)
    step = pl.program_id(0) * pl.num_programs(1) + tp
    n_steps = pl.num_programs(0) * pl.num_programs(1)
    x = _fetch_sets(x_hbm, xbuf, xsem, step, n_steps)
    xn_ref[...] = _rmsnorm(x, ng_ref[...]).astype(BF16)

    def proj(k):
        return jnp.dot(xn_ref[...], w5_ref[:, k * D_MODEL:(k + 1) * D_MODEL],
                       preferred_element_type=F32)

    row = lax.broadcasted_iota(jnp.int32, (n_chunks, D_MODEL), 0)
    cvprev = cvprev_ref[...].astype(F32)
    cvnext = cvnext_ref[...].astype(F32)
    cvprev = jnp.where(tp == 0, jnp.where(row == 0, 0.0, pltpu.roll(cvprev, 1, 0)), cvprev)
    cvnext = jnp.where(tp == CHUNK // SETS_PER_STEP - 1,
                       jnp.where(row == n_chunks - 1, 0.0, pltpu.roll(cvnext, n_chunks - 1, 0)),
                       cvnext)
    cvs = [cv_ref[:, k * D_MODEL:(k + 1) * D_MODEL].astype(F32) for k in range(SETS_PER_STEP)]
    cvm = jnp.concatenate([cvprev] + cvs[:-1], axis=0)
    cvp = jnp.concatenate(cvs[1:] + [cvnext], axis=0)
    cw = cw_ref[...]
    conv = cb_ref[...] + cvm * cw[0:1] + jnp.concatenate(cvs, axis=0) * cw[1:2] + cvp * cw[2:3]

    z_b = proj(2)
    y_b = (proj(1) * conv * (z_b * _sigmoid(z_b))).astype(BF16)
    merged = _sigmoid(proj(4)) * jnp.dot(y_b, wb_ref[...], preferred_element_type=F32)

    yg = _gelu_tanh(jnp.concatenate([yt_ref[k].T for k in range(SETS_PER_STEP)], axis=0))
    yg = yg * _sigmoid(jnp.dot(yg.astype(BF16), wglu_ref[...], preferred_element_type=F32))
    z_a = proj(0)
    y_a = (yg * (z_a * _sigmoid(z_a))).astype(BF16)
    merged = merged + _sigmoid(proj(3)) * jnp.dot(y_a, wa_ref[...], preferred_element_type=F32)

    h = x + jnp.dot(merged.astype(BF16), wo_ref[...], preferred_element_type=F32)
    out = _rmsnorm(h, fg_ref[...])

    slot = step % 2

    @pl.when(step >= 2)
    def _():
        for c in _set_copies(o_hbm, obuf, osem, step - 2, slot, to_hbm=True):
            c.wait()

    for k in range(SETS_PER_STEP):
        obuf[slot, k] = out[k * n_chunks:(k + 1) * n_chunks]
    for c in _set_copies(o_hbm, obuf, osem, step, slot, to_hbm=True):
        c.start()

    @pl.when(step == n_steps - 1)
    def _():
        @pl.when(step >= 1)
        def _():
            for c in _set_copies(o_hbm, obuf, osem, step - 1, 1 - slot, to_hbm=True):
                c.wait()
        for c in _set_copies(o_hbm, obuf, osem, step, slot, to_hbm=True):
            c.wait()


def _block_t(x4, yt, cv3, norm_g, w5, w_glu, w_a, w_b, w_o, cw, cb, final_g, *, bsz, n_chunks):
    s = SETS_PER_STEP
    tok_spec = pl.BlockSpec((None, n_chunks, s * D_MODEL), lambda b, tp: (b, 0, tp))
    prev_spec = pl.BlockSpec((None, n_chunks, D_MODEL),
                             lambda b, tp: (b, 0, (tp * s + CHUNK - 1) % CHUNK))
    next_spec = pl.BlockSpec((None, n_chunks, D_MODEL), lambda b, tp: (b, 0, (tp * s + s) % CHUNK))
    kern = functools.partial(_block_kernel, n_chunks=n_chunks)
    return pl.pallas_call(
        kern,
        grid=(bsz, CHUNK // s),
        in_specs=[
            pl.BlockSpec(memory_space=pl.ANY),
            pl.BlockSpec((s, D_MODEL, n_chunks), lambda b, tp: (tp, 0, b)),
            tok_spec, prev_spec, next_spec,
            _const_spec((1, D_MODEL)),
            _const_spec((D_MODEL, 5 * D_MODEL)),
            _const_spec((D_MODEL, D_MODEL)),
            _const_spec((D_MODEL, D_MODEL)),
            _const_spec((D_MODEL, D_MODEL)),
            _const_spec((D_MODEL, D_MODEL)),
            _const_spec((3, D_MODEL)),
            _const_spec((1, D_MODEL)),
            _const_spec((1, D_MODEL)),
        ],
        out_specs=pl.BlockSpec(memory_space=pl.ANY),
        out_shape=jax.ShapeDtypeStruct(x4.shape, F32),
        scratch_shapes=([pltpu.VMEM((s * n_chunks, D_MODEL), BF16)]
                        + _set_buffers(n_chunks) + _set_buffers(n_chunks)),
        compiler_params=_params(2),
        name="block_t",
    )(x4, yt, cv3, cv3, cv3, norm_g, w5, w_glu, w_a, w_b, w_o, cw, cb, final_g)


def kernel(x, norm_g, w_in, lam_re, lam_im, log_dt, ssm_b_re, ssm_b_im, ssm_c_re, ssm_c_im, ssm_d,
           w_glu, conv_w, conv_b, w_branch_a, w_branch_b, w_out, final_g):
    bsz, seq_len, d = x.shape
    assert norm_g.shape[0] == 1 and d == D_MODEL and bsz == SUBLANES
    n_chunks = seq_len // CHUNK
    assert seq_len == n_chunks * CHUNK and n_chunks % LANES == 0

    x4 = x.reshape(bsz, n_chunks, CHUNK, d)
    ng = norm_g[0].reshape(1, d)
    w8 = w_in[0].astype(BF16).reshape(d, 8, d)
    wut = w8[:, 0].T
    wvc = w8[:, jnp.array([2, 4])].reshape(d, 2 * d)
    w5 = w8[:, jnp.array([1, 3, 5, 6, 7])].reshape(d, 5 * d)

    ut, cv3 = _in_proj_t(x4, ng, wut, wvc, bsz=bsz, n_chunks=n_chunks)
    mt, wst, woutt, aq = _s5_operators(
        lam_re[0], lam_im[0], log_dt[0], ssm_b_re[0], ssm_b_im[0], ssm_c_re[0], ssm_c_im[0],
        ssm_d[0])
    yt = _s5_t(ut, mt, wst, woutt, aq, n_chunks=n_chunks, batch=bsz)

    out4 = _block_t(
        x4, yt, cv3, ng, w5, w_glu[0].astype(BF16), w_branch_a[0].astype(BF16),
        w_branch_b[0].astype(BF16), w_out[0].astype(BF16), conv_w[0], conv_b[0].reshape(1, d),
        final_g.reshape(1, d), bsz=bsz, n_chunks=n_chunks)
    return out4.reshape(bsz, seq_len, d)
```

```python
import functools
import math

import jax
import jax.numpy as jnp
from jax import lax
from jax.experimental import pallas as pl
from jax.experimental.pallas import tpu as pltpu

F32 = jnp.float32
BF16 = jnp.bfloat16

D_MODEL = 1024
SSM_GROUP = 16
N_GROUPS = D_MODEL // SSM_GROUP
STATE = 64
CHUNK = 16
CW = CHUNK * SSM_GROUP
SW = 4 * STATE
HALF = 2 * STATE
LANES = 128
EPS = 1e-6
SUBLANES = 8
SETS_PER_STEP = 2
GROUPS_PER_STEP = 2
VMEM_LIMIT_BYTES = 60 * 1024 * 1024

NT_DIMS = (((1,), (1,)), ((), ()))
TN_DIMS = (((0,), (0,)), ((), ()))


def _rmsnorm(x, g):
    ms = jnp.mean(x * x, axis=-1, keepdims=True)
    return x * lax.rsqrt(ms + EPS) * g


def _sigmoid(x):
    return 0.5 * jnp.tanh(0.5 * x) + 0.5


def _gelu_tanh(x):
    c = math.sqrt(2.0 / math.pi)
    return x * (0.5 * (1.0 + jnp.tanh(c * (x + 0.044715 * (x * x * x)))))


def _params(n_axes):
    return pltpu.CompilerParams(
        dimension_semantics=("arbitrary",) * n_axes, vmem_limit_bytes=VMEM_LIMIT_BYTES)


def _const_spec(shape):
    return pl.BlockSpec(shape, lambda *_: (0,) * len(shape), pipeline_mode=pl.Buffered(1))


COL_U, COL_ZA, COL_V, COL_B, COL_C, COL_ZB, COL_GA, COL_GB = range(8)


def _col_spec(col):
    return pl.BlockSpec((D_MODEL, D_MODEL), lambda *_: (0, col), pipeline_mode=pl.Buffered(1))


def _set_copies(x_hbm, buf, sem, step, slot, *, to_hbm=False):
    steps_per_batch = CHUNK // SETS_PER_STEP
    b = step // steps_per_batch
    tl0 = (step % steps_per_batch) * SETS_PER_STEP
    copies = []
    for k in range(SETS_PER_STEP):
        hbm, vmem = x_hbm.at[b, :, tl0 + k, :], buf.at[slot, k]
        src, dst = (vmem, hbm) if to_hbm else (hbm, vmem)
        copies.append(pltpu.make_async_copy(src, dst, sem.at[slot, k]))
    return copies


def _fetch_sets(x_hbm, xbuf, sem, step, n_steps):
    slot = step % 2

    @pl.when(step == 0)
    def _():
        for c in _set_copies(x_hbm, xbuf, sem, step, slot):
            c.start()

    @pl.when(step + 1 < n_steps)
    def _():
        for c in _set_copies(x_hbm, xbuf, sem, step + 1, 1 - slot):
            c.start()

    for c in _set_copies(x_hbm, xbuf, sem, step, slot):
        c.wait()
    return jnp.concatenate([xbuf[slot, k] for k in range(SETS_PER_STEP)], axis=0)


def _set_buffers(n_chunks):
    return [pltpu.VMEM((2, SETS_PER_STEP, n_chunks, D_MODEL), F32),
            pltpu.SemaphoreType.DMA((2, SETS_PER_STEP))]


def _in_proj_kernel(x_hbm, g_ref, wut_ref, wv_ref, wc_ref, u_ref, cv_ref, xbuf, xsem, *, n_chunks):
    step = pl.program_id(0) * pl.num_programs(1) + pl.program_id(1)
    n_steps = pl.num_programs(0) * pl.num_programs(1)
    x = _fetch_sets(x_hbm, xbuf, xsem, step, n_steps)
    xn = _rmsnorm(x, g_ref[...]).astype(BF16)
    ut = lax.dot_general(wut_ref[...], xn, NT_DIMS, preferred_element_type=F32)
    cv = (jnp.dot(xn, wv_ref[...], preferred_element_type=F32)
          * jnp.dot(xn, wc_ref[...], preferred_element_type=F32)).astype(BF16)
    for k in range(SETS_PER_STEP):
        u_ref[k] = ut[:, k * n_chunks:(k + 1) * n_chunks].astype(BF16)
        cv_ref[:, k * D_MODEL:(k + 1) * D_MODEL] = cv[k * n_chunks:(k + 1) * n_chunks]


def _in_proj_t(x4, norm_g, wut, w_in, *, bsz, n_chunks):
    tok_spec = pl.BlockSpec((None, n_chunks, SETS_PER_STEP * D_MODEL), lambda b, tp: (b, 0, tp))
    return pl.pallas_call(
        functools.partial(_in_proj_kernel, n_chunks=n_chunks),
        grid=(bsz, CHUNK // SETS_PER_STEP),
        in_specs=[pl.BlockSpec(memory_space=pl.ANY), _const_spec((1, D_MODEL)),
                  _const_spec((D_MODEL, D_MODEL)), _col_spec(COL_V), _col_spec(COL_C)],
        out_specs=[pl.BlockSpec((SETS_PER_STEP, D_MODEL, n_chunks), lambda b, tp: (tp, 0, b)),
                   tok_spec],
        out_shape=[jax.ShapeDtypeStruct((CHUNK, D_MODEL, bsz * n_chunks), BF16),
                   jax.ShapeDtypeStruct((bsz, n_chunks, CHUNK * D_MODEL), BF16)],
        scratch_shapes=_set_buffers(n_chunks),
        compiler_params=_params(2),
        name="in_proj_t",
    )(x4, norm_g, wut, w_in, w_in)


def _s5_kernel(u_ref, mt_ref, wst_ref, woutt_ref, aq_ref, y_ref, sloc_ref, pf_ref, pb_ref,
               *, n_chunks, batch):
    n = batch * n_chunks
    groups = range(GROUPS_PER_STEP)
    uts = [u_ref[:, j * SSM_GROUP:(j + 1) * SSM_GROUP, :].reshape(CW, n) for j in groups]

    for j in groups:
        wst = wst_ref[j]
        for b in range(batch):
            sl = lax.dot_general(uts[j][:, b * n_chunks:(b + 1) * n_chunks], wst, TN_DIMS,
                                 preferred_element_type=F32)
            sloc_ref[j, 0, pl.ds(b, n_chunks, stride=batch), :] = sl[:, :HALF]
            sloc_ref[j, 1, pl.ds(b, n_chunks, stride=batch), :] = sl[:, HALF:]

    is_fwd = lax.broadcasted_iota(jnp.int32, (batch, HALF), 1) < STATE
    ar = [jnp.broadcast_to(aq_ref[j, 0:1, :], (batch, HALF)) for j in groups]
    ai = [jnp.broadcast_to(aq_ref[j, 1:2, :], (batch, HALF)) for j in groups]

    zero = jnp.zeros((batch, HALF), F32)
    last = n_chunks * batch
    for j in groups:
        for h in range(2):
            pf_ref[j, h, 0:batch, :] = zero
            pb_ref[j, h, last:last + batch, :] = zero

    def scan_step(k, carry):
        rf = pl.multiple_of(k * batch, batch)
        rb = pl.multiple_of((n_chunks - 1 - k) * batch, batch)
        ro = pl.multiple_of((k + 1) * batch, batch)
        new = []
        for j in groups:
            sr, si = carry[j]
            xr = jnp.where(is_fwd, sloc_ref[j, 0, pl.ds(rf, batch), :],
                           sloc_ref[j, 0, pl.ds(rb, batch), :])
            xi = jnp.where(is_fwd, sloc_ref[j, 1, pl.ds(rf, batch), :],
                           sloc_ref[j, 1, pl.ds(rb, batch), :])
            nr = ar[j] * sr - ai[j] * si + xr
            ni = ar[j] * si + ai[j] * sr + xi
            pf_ref[j, 0, pl.ds(ro, batch), :] = nr
            pf_ref[j, 1, pl.ds(ro, batch), :] = ni
            pb_ref[j, 0, pl.ds(rb, batch), :] = nr
            pb_ref[j, 1, pl.ds(rb, batch), :] = ni
            new.append((nr, ni))
        return tuple(new)

    lax.fori_loop(0, n_chunks, scan_step, tuple((zero, zero) for _ in groups), unroll=4)

    fwd_lanes = lax.broadcasted_iota(jnp.int32, (CW, SW), 1) % HALF < STATE
    for j in groups:
        mt = mt_ref[j]
        woutt = woutt_ref[j]
        woutt_f = jnp.where(fwd_lanes, woutt, jnp.zeros_like(woutt))
        woutt_b = jnp.where(fwd_lanes, jnp.zeros_like(woutt), woutt)
        for b in range(batch):
            cols = slice(b * n_chunks, (b + 1) * n_chunks)
            sf = jnp.concatenate([pf_ref[j, h, pl.ds(b, n_chunks, stride=batch), :] for h in range(2)],
                                 axis=1)
            sb = jnp.concatenate(
                [pb_ref[j, h, pl.ds(batch + b, n_chunks, stride=batch), :] for h in range(2)], axis=1)
            yb = jnp.dot(mt, uts[j][:, cols], preferred_element_type=F32)
            yb = yb + lax.dot_general(woutt_f, sf.astype(BF16), NT_DIMS, preferred_element_type=F32)
            yb = yb + lax.dot_general(woutt_b, sb.astype(BF16), NT_DIMS, preferred_element_type=F32)
            y_ref[:, j * SSM_GROUP:(j + 1) * SSM_GROUP, cols] = yb.reshape(CHUNK, SSM_GROUP, n_chunks)


def _s5_t(ut, mt, wst, woutt, aq, *, n_chunks, batch):
    n = batch * n_chunks
    gps = GROUPS_PER_STEP
    kern = functools.partial(_s5_kernel, n_chunks=n_chunks, batch=batch)
    op_spec = pl.BlockSpec((gps, CW, CW), lambda g: (g, 0, 0))
    act_spec = pl.BlockSpec((CHUNK, gps * SSM_GROUP, n), lambda g: (0, g, 0))
    state_rows = pltpu.VMEM((gps, 2, n + batch, HALF), F32)
    return pl.pallas_call(
        kern,
        grid=(N_GROUPS // gps,),
        in_specs=[act_spec, op_spec, op_spec, op_spec,
                  pl.BlockSpec((gps, 2, HALF), lambda g: (g, 0, 0))],
        out_specs=act_spec,
        out_shape=jax.ShapeDtypeStruct((CHUNK, D_MODEL, n), F32),
        scratch_shapes=[pltpu.VMEM((gps, 2, n, HALF), F32), state_rows, state_rows],
        compiler_params=_params(1),
        name="s5_t",
    )(ut, mt, wst, woutt, aq)


def _s5_ops_kernel(lam_ref, c_ref, bt_ref, dd_ref, mt_ref, wst_ref, woutt_ref, aq_ref,
                   pwr_ref, pwi_ref):
    lr, li = lam_ref[0, 0:1, :], lam_ref[0, 1:2, :]
    dt = jnp.exp(lam_ref[0, 2:3, :])
    c_re, c_im = c_ref[0, 0], c_ref[0, 1]
    bt_re, bt_im = bt_ref[0, 0], bt_ref[0, 1]

    n_rows = 2 * CHUNK + SUBLANES
    r = lax.broadcasted_iota(jnp.int32, (n_rows, HALF), 0)
    fwd = lax.broadcasted_iota(jnp.int32, (n_rows, HALF), 1) < STATE
    s = r - CHUNK
    e = jnp.where(r < CHUNK, jnp.where(fwd, r + 1, CHUNK - r),
                  jnp.where(r < 2 * CHUNK, jnp.where(fwd, CHUNK - 1 - s, s),
                            jnp.where(r == 2 * CHUNK, CHUNK, 1))).astype(F32)
    mag = jnp.exp(e * (lr * dt))
    ang = e * (li * dt)
    pwr_ref[...] = mag * jnp.cos(ang)
    pwi_ref[...] = mag * jnp.sin(ang)

    def power(row):
        return pwr_ref[row:row + 1, :], pwi_ref[row:row + 1, :]

    a_re, a_im = power(2 * CHUNK + 1)
    den = lr * lr + li * li
    nr, ni = a_re - 1.0, a_im
    coef_re = (nr * lr + ni * li) / den
    coef_im = (ni * lr - nr * li) / den
    bb_re = coef_re * bt_re - coef_im * bt_im
    bb_im = coef_re * bt_im + coef_im * bt_re

    st_re, st_im = [], []
    for k in range(CHUNK):
        p_re, p_im = power(CHUNK + k)
        st_re.append(p_re * bb_re - p_im * bb_im)
        st_im.append(p_re * bb_im + p_im * bb_re)
    wst = jnp.concatenate([jnp.concatenate(st_re, axis=0), jnp.concatenate(st_im, axis=0)], axis=1)
    wst_ref[0] = wst.astype(BF16)

    out_rows = []
    for k in range(CHUNK):
        p_re, p_im = power(k)
        out_rows.append(jnp.concatenate([c_re * p_re - c_im * p_im, -(c_re * p_im + c_im * p_re)],
                                        axis=1))
    woutt_ref[0] = jnp.concatenate(out_rows, axis=0).astype(BF16)

    aq_ref[0, 0:1, :], aq_ref[0, 1:2, :] = power(2 * CHUNK)

    fwd1 = lax.broadcasted_iota(jnp.int32, (SSM_GROUP, HALF), 1) < STATE
    zero = jnp.zeros_like(c_re)
    cmat = jnp.concatenate([
        jnp.concatenate([jnp.where(fwd1, c_re, zero), -jnp.where(fwd1, c_im, zero)], axis=1),
        jnp.concatenate([jnp.where(fwd1, zero, c_re), -jnp.where(fwd1, zero, c_im)], axis=1)], axis=0)
    kslab = lax.dot_general(cmat, wst, NT_DIMS, precision=lax.Precision.HIGHEST,
                            preferred_element_type=F32)
    k_f = kslab[:SSM_GROUP]
    k_b = kslab[SSM_GROUP:] + dd_ref[0]

    lane = lax.broadcasted_iota(jnp.int32, (SSM_GROUP, CW), 1)
    m_rows = []
    for t in range(CHUNK):
        left = SSM_GROUP * (CHUNK - 1 - t)
        right = SSM_GROUP * t
        a = k_f if left == 0 else jnp.where(lane < CW - left, pltpu.roll(k_f, CW - left, 1), 0.0)
        b = k_b if right == 0 else jnp.where(lane >= right, pltpu.roll(k_b, right, 1), 0.0)
        m_rows.append(a + b)
    mt_ref[0] = jnp.concatenate(m_rows, axis=0).astype(BF16)


def _s5_operators(lam_re, lam_im, log_dt, b_re, b_im, c_re, c_im, ssm_d):
    def lanes(x):
        return jnp.concatenate([x[0], x[1]], axis=-1)

    lam = jnp.stack([lanes(lam_re), lanes(lam_im),
                     lanes(jnp.broadcast_to(log_dt[..., None], lam_re.shape))], axis=1)
    c = jnp.stack([lanes(c_re), lanes(c_im)], axis=1)
    bt = jnp.stack([lanes(b_re.transpose(0, 1, 3, 2)), lanes(b_im.transpose(0, 1, 3, 2))], axis=1)
    dd = ssm_d.reshape(N_GROUPS, SSM_GROUP)[:, :, None] * jnp.eye(SSM_GROUP, CW, dtype=F32)

    op_spec = pl.BlockSpec((1, CW, CW), lambda g: (g, 0, 0))
    par_spec = pl.BlockSpec((1, 2, SSM_GROUP, HALF), lambda g: (g, 0, 0, 0))
    op_shape = jax.ShapeDtypeStruct((N_GROUPS, CW, CW), BF16)
    return pl.pallas_call(
        _s5_ops_kernel,
        grid=(N_GROUPS,),
        in_specs=[pl.BlockSpec((1, 3, HALF), lambda g: (g, 0, 0)), par_spec, par_spec,
                  pl.BlockSpec((1, SSM_GROUP, CW), lambda g: (g, 0, 0))],
        out_specs=[op_spec, op_spec, op_spec, pl.BlockSpec((1, 2, HALF), lambda g: (g, 0, 0))],
        out_shape=[op_shape, op_shape, op_shape, jax.ShapeDtypeStruct((N_GROUPS, 2, HALF), F32)],
        scratch_shapes=[pltpu.VMEM((2 * CHUNK + SUBLANES, HALF), F32)] * 2,
        compiler_params=_params(1),
        name="s5_ops",
    )(lam, c, bt, dd)


def _block_kernel(x_hbm, yt_ref, cv_ref, cvprev_ref, cvnext_ref, ng_ref, pza_ref, pb_ref, pzb_ref,
                  pga_ref, pgb_ref, wglu_ref, wa_ref, wb_ref, wo_ref, cw_ref, cb_ref, fg_ref,
                  o_hbm, xn_ref, xbuf, xsem, obuf, osem, *, n_chunks):
    tp = pl.program_id(1)
    step = pl.program_id(0) * pl.num_programs(1) + tp
    n_steps = pl.num_programs(0) * pl.num_programs(1)
    x = _fetch_sets(x_hbm, xbuf, xsem, step, n_steps)
    xn_ref[...] = _rmsnorm(x, ng_ref[...]).astype(BF16)

    def proj(w_ref):
        return jnp.dot(xn_ref[...], w_ref[...], preferred_element_type=F32)

    row = lax.broadcasted_iota(jnp.int32, (n_chunks, D_MODEL), 0)
    cvprev = cvprev_ref[...].astype(F32)
    cvnext = cvnext_ref[...].astype(F32)
    cvprev = jnp.where(tp == 0, jnp.where(row == 0, 0.0, pltpu.roll(cvprev, 1, 0)), cvprev)
    cvnext = jnp.where(tp == CHUNK // SETS_PER_STEP - 1,
                       jnp.where(row == n_chunks - 1, 0.0, pltpu.roll(cvnext, n_chunks - 1, 0)),
                       cvnext)
    cvs = [cv_ref[:, k * D_MODEL:(k + 1) * D_MODEL].astype(F32) for k in range(SETS_PER_STEP)]
    cvm = jnp.concatenate([cvprev] + cvs[:-1], axis=0)
    cvp = jnp.concatenate(cvs[1:] + [cvnext], axis=0)
    cw = cw_ref[...]
    conv = cb_ref[...] + cvm * cw[0:1] + jnp.concatenate(cvs, axis=0) * cw[1:2] + cvp * cw[2:3]

    z_b = proj(pzb_ref)
    y_b = (proj(pb_ref) * conv * (z_b * _sigmoid(z_b))).astype(BF16)
    merged = _sigmoid(proj(pgb_ref)) * jnp.dot(y_b, wb_ref[...], preferred_element_type=F32)

    yg = _gelu_tanh(jnp.concatenate([yt_ref[k].T for k in range(SETS_PER_STEP)], axis=0))
    yg = yg * _sigmoid(jnp.dot(yg.astype(BF16), wglu_ref[...], preferred_element_type=F32))
    z_a = proj(pza_ref)
    y_a = (yg * (z_a * _sigmoid(z_a))).astype(BF16)
    merged = merged + _sigmoid(proj(pga_ref)) * jnp.dot(y_a, wa_ref[...], preferred_element_type=F32)

    h = x + jnp.dot(merged.astype(BF16), wo_ref[...], preferred_element_type=F32)
    out = _rmsnorm(h, fg_ref[...])

    slot = step % 2

    @pl.when(step >= 2)
    def _():
        for c in _set_copies(o_hbm, obuf, osem, step - 2, slot, to_hbm=True):
            c.wait()

    for k in range(SETS_PER_STEP):
        obuf[slot, k] = out[k * n_chunks:(k + 1) * n_chunks]
    for c in _set_copies(o_hbm, obuf, osem, step, slot, to_hbm=True):
        c.start()

    @pl.when(step == n_steps - 1)
    def _():
        @pl.when(step >= 1)
        def _():
            for c in _set_copies(o_hbm, obuf, osem, step - 1, 1 - slot, to_hbm=True):
                c.wait()
        for c in _set_copies(o_hbm, obuf, osem, step, slot, to_hbm=True):
            c.wait()


def _block_t(x4, yt, cv3, norm_g, w_in, w_glu, w_a, w_b, w_o, cw, cb, final_g, *, bsz, n_chunks):
    s = SETS_PER_STEP
    tok_spec = pl.BlockSpec((None, n_chunks, s * D_MODEL), lambda b, tp: (b, 0, tp))
    prev_spec = pl.BlockSpec((None, n_chunks, D_MODEL),
                             lambda b, tp: (b, 0, (tp * s + CHUNK - 1) % CHUNK))
    next_spec = pl.BlockSpec((None, n_chunks, D_MODEL), lambda b, tp: (b, 0, (tp * s + s) % CHUNK))
    kern = functools.partial(_block_kernel, n_chunks=n_chunks)
    return pl.pallas_call(
        kern,
        grid=(bsz, CHUNK // s),
        in_specs=[
            pl.BlockSpec(memory_space=pl.ANY),
            pl.BlockSpec((s, D_MODEL, n_chunks), lambda b, tp: (tp, 0, b)),
            tok_spec, prev_spec, next_spec,
            _const_spec((1, D_MODEL)),
            _col_spec(COL_ZA), _col_spec(COL_B), _col_spec(COL_ZB), _col_spec(COL_GA), _col_spec(COL_GB),
            _const_spec((D_MODEL, D_MODEL)),
            _const_spec((D_MODEL, D_MODEL)),
            _const_spec((D_MODEL, D_MODEL)),
            _const_spec((D_MODEL, D_MODEL)),
            _const_spec((3, D_MODEL)),
            _const_spec((1, D_MODEL)),
            _const_spec((1, D_MODEL)),
        ],
        out_specs=pl.BlockSpec(memory_space=pl.ANY),
        out_shape=jax.ShapeDtypeStruct(x4.shape, F32),
        scratch_shapes=([pltpu.VMEM((s * n_chunks, D_MODEL), BF16)]
                        + _set_buffers(n_chunks) + _set_buffers(n_chunks)),
        compiler_params=_params(2),
        name="block_t",
    )(x4, yt, cv3, cv3, cv3, norm_g, w_in, w_in, w_in, w_in, w_in, w_glu, w_a, w_b, w_o, cw, cb,
      final_g)


def kernel(x, norm_g, w_in, lam_re, lam_im, log_dt, ssm_b_re, ssm_b_im, ssm_c_re, ssm_c_im, ssm_d,
           w_glu, conv_w, conv_b, w_branch_a, w_branch_b, w_out, final_g):
    bsz, seq_len, d = x.shape
    assert norm_g.shape[0] == 1 and d == D_MODEL and bsz == SUBLANES
    n_chunks = seq_len // CHUNK
    assert seq_len == n_chunks * CHUNK and n_chunks % LANES == 0

    x4 = x.reshape(bsz, n_chunks, CHUNK, d)
    ng = norm_g[0].reshape(1, d)
    w_bf = w_in[0].astype(BF16)
    wut = w_bf[:, COL_U * d:(COL_U + 1) * d].T

    ut, cv3 = _in_proj_t(x4, ng, wut, w_bf, bsz=bsz, n_chunks=n_chunks)
    mt, wst, woutt, aq = _s5_operators(
        lam_re[0], lam_im[0], log_dt[0], ssm_b_re[0], ssm_b_im[0], ssm_c_re[0], ssm_c_im[0],
        ssm_d[0])
    yt = _s5_t(ut, mt, wst, woutt, aq, n_chunks=n_chunks, batch=bsz)

    out4 = _block_t(
        x4, yt, cv3, ng, w_bf, w_glu[0].astype(BF16), w_branch_a[0].astype(BF16),
        w_branch_b[0].astype(BF16), w_out[0].astype(BF16), conv_w[0], conv_b[0].reshape(1, d),
        final_g.reshape(1, d), bsz=bsz, n_chunks=n_chunks)
    return out4.reshape(bsz, seq_len, d)
```

```python
import functools
import math

import jax
import jax.numpy as jnp
from jax import lax
from jax.experimental import pallas as pl
from jax.experimental.pallas import tpu as pltpu

F32 = jnp.float32
BF16 = jnp.bfloat16

D_MODEL = 1024
SSM_GROUP = 16
N_GROUPS = D_MODEL // SSM_GROUP
STATE = 64
CHUNK = 16
CW = CHUNK * SSM_GROUP
SW = 4 * STATE
HALF = 2 * STATE
LANES = 128
EPS = 1e-6
SUBLANES = 8
SETS_PER_STEP = 2
GROUPS_PER_STEP = 2
OPS_GROUPS_PER_STEP = 8
VMEM_LIMIT_BYTES = 60 * 1024 * 1024

NT_DIMS = (((1,), (1,)), ((), ()))
TN_DIMS = (((0,), (0,)), ((), ()))


def _rmsnorm(x, g):
    ms = jnp.mean(x * x, axis=-1, keepdims=True)
    return x * lax.rsqrt(ms + EPS) * g


def _sigmoid(x):
    return 0.5 * jnp.tanh(0.5 * x) + 0.5


def _gelu_tanh(x):
    c = math.sqrt(2.0 / math.pi)
    return x * (0.5 * (1.0 + jnp.tanh(c * (x + 0.044715 * (x * x * x)))))


def _params(n_axes):
    return pltpu.CompilerParams(
        dimension_semantics=("arbitrary",) * n_axes, vmem_limit_bytes=VMEM_LIMIT_BYTES)


def _const_spec(shape):
    return pl.BlockSpec(shape, lambda *_: (0,) * len(shape), pipeline_mode=pl.Buffered(1))


COL_U, COL_ZA, COL_V, COL_B, COL_C, COL_ZB, COL_GA, COL_GB = range(8)


def _col_spec(col):
    return pl.BlockSpec((D_MODEL, D_MODEL), lambda *_: (0, col), pipeline_mode=pl.Buffered(1))


def _set_copies(x_hbm, buf, sem, step, slot, *, to_hbm=False):
    steps_per_batch = CHUNK // SETS_PER_STEP
    b = step // steps_per_batch
    tl0 = (step % steps_per_batch) * SETS_PER_STEP
    copies = []
    for k in range(SETS_PER_STEP):
        hbm, vmem = x_hbm.at[b, :, tl0 + k, :], buf.at[slot, k]
        src, dst = (vmem, hbm) if to_hbm else (hbm, vmem)
        copies.append(pltpu.make_async_copy(src, dst, sem.at[slot, k]))
    return copies


def _fetch_sets(x_hbm, xbuf, sem, step, n_steps):
    slot = step % 2

    @pl.when(step == 0)
    def _():
        for c in _set_copies(x_hbm, xbuf, sem, step, slot):
            c.start()

    @pl.when(step + 1 < n_steps)
    def _():
        for c in _set_copies(x_hbm, xbuf, sem, step + 1, 1 - slot):
            c.start()

    for c in _set_copies(x_hbm, xbuf, sem, step, slot):
        c.wait()
    return jnp.concatenate([xbuf[slot, k] for k in range(SETS_PER_STEP)], axis=0)


def _set_buffers(n_chunks):
    return [pltpu.VMEM((2, SETS_PER_STEP, n_chunks, D_MODEL), F32),
            pltpu.SemaphoreType.DMA((2, SETS_PER_STEP))]


def _in_proj_kernel(x_hbm, g_ref, wut_ref, wv_ref, wc_ref, u_ref, cv_ref, xbuf, xsem, *, n_chunks):
    step = pl.program_id(0) * pl.num_programs(1) + pl.program_id(1)
    n_steps = pl.num_programs(0) * pl.num_programs(1)
    x = _fetch_sets(x_hbm, xbuf, xsem, step, n_steps)
    xn = _rmsnorm(x, g_ref[...]).astype(BF16)
    ut = lax.dot_general(wut_ref[...], xn, NT_DIMS, preferred_element_type=F32)
    cv = (jnp.dot(xn, wv_ref[...], preferred_element_type=F32)
          * jnp.dot(xn, wc_ref[...], preferred_element_type=F32)).astype(BF16)
    for k in range(SETS_PER_STEP):
        u_ref[k] = ut[:, k * n_chunks:(k + 1) * n_chunks].astype(BF16)
        cv_ref[:, k * D_MODEL:(k + 1) * D_MODEL] = cv[k * n_chunks:(k + 1) * n_chunks]


def _in_proj_t(x4, norm_g, wut, w_in, *, bsz, n_chunks):
    tok_spec = pl.BlockSpec((None, n_chunks, SETS_PER_STEP * D_MODEL), lambda b, tp: (b, 0, tp))
    return pl.pallas_call(
        functools.partial(_in_proj_kernel, n_chunks=n_chunks),
        grid=(bsz, CHUNK // SETS_PER_STEP),
        in_specs=[pl.BlockSpec(memory_space=pl.ANY), _const_spec((1, D_MODEL)),
                  _const_spec((D_MODEL, D_MODEL)), _col_spec(COL_V), _col_spec(COL_C)],
        out_specs=[pl.BlockSpec((SETS_PER_STEP, D_MODEL, n_chunks), lambda b, tp: (tp, 0, b)),
                   tok_spec],
        out_shape=[jax.ShapeDtypeStruct((CHUNK, D_MODEL, bsz * n_chunks), BF16),
                   jax.ShapeDtypeStruct((bsz, n_chunks, CHUNK * D_MODEL), BF16)],
        scratch_shapes=_set_buffers(n_chunks),
        compiler_params=_params(2),
        name="in_proj_t",
    )(x4, norm_g, wut, w_in, w_in)


def _s5_kernel(u_ref, mt_ref, wst_ref, woutt_ref, aq_ref, y_ref, sloc_ref, pf_ref, pb_ref,
               *, n_chunks, batch):
    n = batch * n_chunks
    groups = range(GROUPS_PER_STEP)
    uts = [u_ref[:, j * SSM_GROUP:(j + 1) * SSM_GROUP, :].reshape(CW, n) for j in groups]

    for j in groups:
        wst = wst_ref[j]
        for b in range(batch):
            sl = lax.dot_general(uts[j][:, b * n_chunks:(b + 1) * n_chunks], wst, TN_DIMS,
                                 preferred_element_type=F32)
            sloc_ref[j, 0, pl.ds(b, n_chunks, stride=batch), :] = sl[:, :HALF]
            sloc_ref[j, 1, pl.ds(b, n_chunks, stride=batch), :] = sl[:, HALF:]

    is_fwd = lax.broadcasted_iota(jnp.int32, (batch, HALF), 1) < STATE
    ar = [jnp.broadcast_to(aq_ref[j, 0:1, :], (batch, HALF)) for j in groups]
    ai = [jnp.broadcast_to(aq_ref[j, 1:2, :], (batch, HALF)) for j in groups]

    zero = jnp.zeros((batch, HALF), F32)
    last = n_chunks * batch
    for j in groups:
        for h in range(2):
            pf_ref[j, h, 0:batch, :] = zero
            pb_ref[j, h, last:last + batch, :] = zero

    def scan_step(k, carry):
        rf = pl.multiple_of(k * batch, batch)
        rb = pl.multiple_of((n_chunks - 1 - k) * batch, batch)
        ro = pl.multiple_of((k + 1) * batch, batch)
        new = []
        for j in groups:
            sr, si = carry[j]
            xr = jnp.where(is_fwd, sloc_ref[j, 0, pl.ds(rf, batch), :],
                           sloc_ref[j, 0, pl.ds(rb, batch), :])
            xi = jnp.where(is_fwd, sloc_ref[j, 1, pl.ds(rf, batch), :],
                           sloc_ref[j, 1, pl.ds(rb, batch), :])
            nr = ar[j] * sr - ai[j] * si + xr
            ni = ar[j] * si + ai[j] * sr + xi
            pf_ref[j, 0, pl.ds(ro, batch), :] = nr
            pf_ref[j, 1, pl.ds(ro, batch), :] = ni
            pb_ref[j, 0, pl.ds(rb, batch), :] = nr
            pb_ref[j, 1, pl.ds(rb, batch), :] = ni
            new.append((nr, ni))
        return tuple(new)

    lax.fori_loop(0, n_chunks, scan_step, tuple((zero, zero) for _ in groups), unroll=4)

    fwd_rows = lax.broadcasted_iota(jnp.int32, (n_chunks, HALF), 1) < STATE
    for j in groups:
        mt = mt_ref[j]
        woutt = woutt_ref[j]
        for b in range(batch):
            cols = slice(b * n_chunks, (b + 1) * n_chunks)
            prev = jnp.concatenate(
                [jnp.where(fwd_rows, pf_ref[j, h, pl.ds(b, n_chunks, stride=batch), :],
                           pb_ref[j, h, pl.ds(batch + b, n_chunks, stride=batch), :])
                 for h in range(2)], axis=1)
            yb = jnp.dot(mt, uts[j][:, cols], preferred_element_type=F32)
            yb = yb + lax.dot_general(woutt, prev.astype(BF16), NT_DIMS, preferred_element_type=F32)
            y_ref[:, j * SSM_GROUP:(j + 1) * SSM_GROUP, cols] = yb.reshape(CHUNK, SSM_GROUP, n_chunks)


def _s5_t(ut, mt, wst, woutt, aq, *, n_chunks, batch):
    n = batch * n_chunks
    gps = GROUPS_PER_STEP
    kern = functools.partial(_s5_kernel, n_chunks=n_chunks, batch=batch)
    op_spec = pl.BlockSpec((gps, CW, CW), lambda g: (g, 0, 0))
    act_spec = pl.BlockSpec((CHUNK, gps * SSM_GROUP, n), lambda g: (0, g, 0))
    state_rows = pltpu.VMEM((gps, 2, n + batch, HALF), F32)
    return pl.pallas_call(
        kern,
        grid=(N_GROUPS // gps,),
        in_specs=[act_spec, op_spec, op_spec, op_spec,
                  pl.BlockSpec((gps, 2, HALF), lambda g: (g, 0, 0))],
        out_specs=act_spec,
        out_shape=jax.ShapeDtypeStruct((CHUNK, D_MODEL, n), F32),
        scratch_shapes=[pltpu.VMEM((gps, 2, n, HALF), F32), state_rows, state_rows],
        compiler_params=_params(1),
        name="s5_t",
    )(ut, mt, wst, woutt, aq)


def _s5_ops_kernel(*refs):
    for g in range(OPS_GROUPS_PER_STEP):
        _s5_ops_group(g, *refs)


def _s5_ops_group(g, lam_ref, c_ref, bt_ref, dd_ref, mt_ref, wst_ref, woutt_ref, aq_ref,
                  pwr_ref, pwi_ref):
    lr, li = lam_ref[g, 0:1, :], lam_ref[g, 1:2, :]
    dt = jnp.exp(lam_ref[g, 2:3, :])
    c_re, c_im = c_ref[g, 0], c_ref[g, 1]
    bt_re, bt_im = bt_ref[g, 0], bt_ref[g, 1]

    n_rows = 2 * CHUNK + SUBLANES
    r = lax.broadcasted_iota(jnp.int32, (n_rows, HALF), 0)
    fwd = lax.broadcasted_iota(jnp.int32, (n_rows, HALF), 1) < STATE
    s = r - CHUNK
    e = jnp.where(r < CHUNK, jnp.where(fwd, r + 1, CHUNK - r),
                  jnp.where(r < 2 * CHUNK, jnp.where(fwd, CHUNK - 1 - s, s),
                            jnp.where(r == 2 * CHUNK, CHUNK, 1))).astype(F32)
    mag = jnp.exp(e * (lr * dt))
    ang = e * (li * dt)
    pwr_ref[g] = mag * jnp.cos(ang)
    pwi_ref[g] = mag * jnp.sin(ang)

    def power(row):
        return pwr_ref[g, row:row + 1, :], pwi_ref[g, row:row + 1, :]

    a_re, a_im = power(2 * CHUNK + 1)
    den = lr * lr + li * li
    nr, ni = a_re - 1.0, a_im
    coef_re = (nr * lr + ni * li) / den
    coef_im = (ni * lr - nr * li) / den
    bb_re = coef_re * bt_re - coef_im * bt_im
    bb_im = coef_re * bt_im + coef_im * bt_re

    st_re, st_im = [], []
    for k in range(CHUNK):
        p_re, p_im = power(CHUNK + k)
        st_re.append(p_re * bb_re - p_im * bb_im)
        st_im.append(p_re * bb_im + p_im * bb_re)
    wst = jnp.concatenate([jnp.concatenate(st_re, axis=0), jnp.concatenate(st_im, axis=0)], axis=1)
    wst_ref[g] = wst.astype(BF16)

    out_rows = []
    for k in range(CHUNK):
        p_re, p_im = power(k)
        out_rows.append(jnp.concatenate([c_re * p_re - c_im * p_im, -(c_re * p_im + c_im * p_re)],
                                        axis=1))
    woutt_ref[g] = jnp.concatenate(out_rows, axis=0).astype(BF16)

    aq_ref[g, 0:1, :], aq_ref[g, 1:2, :] = power(2 * CHUNK)

    fwd1 = lax.broadcasted_iota(jnp.int32, (SSM_GROUP, HALF), 1) < STATE
    zero = jnp.zeros_like(c_re)
    cmat = jnp.concatenate([
        jnp.concatenate([jnp.where(fwd1, c_re, zero), -jnp.where(fwd1, c_im, zero)], axis=1),
        jnp.concatenate([jnp.where(fwd1, zero, c_re), -jnp.where(fwd1, zero, c_im)], axis=1)], axis=0)
    kslab = lax.dot_general(cmat, wst, NT_DIMS, precision=lax.Precision.HIGHEST,
                            preferred_element_type=F32)
    k_f = kslab[:SSM_GROUP]
    k_b = kslab[SSM_GROUP:] + dd_ref[g]

    lane = lax.broadcasted_iota(jnp.int32, (SSM_GROUP, CW), 1)
    m_rows = []
    for t in range(CHUNK):
        left = SSM_GROUP * (CHUNK - 1 - t)
        right = SSM_GROUP * t
        a = k_f if left == 0 else jnp.where(lane < CW - left, pltpu.roll(k_f, CW - left, 1), 0.0)
        b = k_b if right == 0 else jnp.where(lane >= right, pltpu.roll(k_b, right, 1), 0.0)
        m_rows.append(a + b)
    mt_ref[g] = jnp.concatenate(m_rows, axis=0).astype(BF16)


def _s5_operators(lam_re, lam_im, log_dt, b_re, b_im, c_re, c_im, ssm_d):
    def lanes(x):
        return jnp.concatenate([x[0], x[1]], axis=-1)

    lam = jnp.stack([lanes(lam_re), lanes(lam_im),
                     lanes(jnp.broadcast_to(log_dt[..., None], lam_re.shape))], axis=1)
    c = jnp.stack([lanes(c_re), lanes(c_im)], axis=1)
    bt = jnp.stack([lanes(b_re.transpose(0, 1, 3, 2)), lanes(b_im.transpose(0, 1, 3, 2))], axis=1)
    dd = ssm_d.reshape(N_GROUPS, SSM_GROUP)[:, :, None] * jnp.eye(SSM_GROUP, CW, dtype=F32)

    gps = OPS_GROUPS_PER_STEP
    op_spec = pl.BlockSpec((gps, CW, CW), lambda g: (g, 0, 0))
    par_spec = pl.BlockSpec((gps, 2, SSM_GROUP, HALF), lambda g: (g, 0, 0, 0))
    op_shape = jax.ShapeDtypeStruct((N_GROUPS, CW, CW), BF16)
    return pl.pallas_call(
        _s5_ops_kernel,
        grid=(N_GROUPS // gps,),
        in_specs=[pl.BlockSpec((gps, 3, HALF), lambda g: (g, 0, 0)), par_spec, par_spec,
                  pl.BlockSpec((gps, SSM_GROUP, CW), lambda g: (g, 0, 0))],
        out_specs=[op_spec, op_spec, op_spec, pl.BlockSpec((gps, 2, HALF), lambda g: (g, 0, 0))],
        out_shape=[op_shape, op_shape, op_shape, jax.ShapeDtypeStruct((N_GROUPS, 2, HALF), F32)],
        scratch_shapes=[pltpu.VMEM((gps, 2 * CHUNK + SUBLANES, HALF), F32)] * 2,
        compiler_params=_params(1),
        name="s5_ops",
    )(lam, c, bt, dd)


def _block_kernel(x_hbm, yt_ref, cv_ref, cvprev_ref, cvnext_ref, ng_ref, pza_ref, pb_ref, pzb_ref,
                  pga_ref, pgb_ref, wglu_ref, wa_ref, wb_ref, wo_ref, cw_ref, cb_ref, fg_ref,
                  o_hbm, xn_ref, xbuf, xsem, obuf, osem, *, n_chunks):
    tp = pl.program_id(1)
    step = pl.program_id(0) * pl.num_programs(1) + tp
    n_steps = pl.num_programs(0) * pl.num_programs(1)
    x = _fetch_sets(x_hbm, xbuf, xsem, step, n_steps)
    xn_ref[...] = _rmsnorm(x, ng_ref[...]).astype(BF16)

    def proj(w_ref):
        return jnp.dot(xn_ref[...], w_ref[...], preferred_element_type=F32)

    row = lax.broadcasted_iota(jnp.int32, (n_chunks, D_MODEL), 0)
    cvprev = cvprev_ref[...].astype(F32)
    cvnext = cvnext_ref[...].astype(F32)
    cvprev = jnp.where(tp == 0, jnp.where(row == 0, 0.0, pltpu.roll(cvprev, 1, 0)), cvprev)
    cvnext = jnp.where(tp == CHUNK // SETS_PER_STEP - 1,
                       jnp.where(row == n_chunks - 1, 0.0, pltpu.roll(cvnext, n_chunks - 1, 0)),
                       cvnext)
    cvs = [cv_ref[:, k * D_MODEL:(k + 1) * D_MODEL].astype(F32) for k in range(SETS_PER_STEP)]
    cvm = jnp.concatenate([cvprev] + cvs[:-1], axis=0)
    cvp = jnp.concatenate(cvs[1:] + [cvnext], axis=0)
    cw = cw_ref[...]
    conv = cb_ref[...] + cvm * cw[0:1] + jnp.concatenate(cvs, axis=0) * cw[1:2] + cvp * cw[2:3]

    z_b = proj(pzb_ref)
    y_b = (proj(pb_ref) * conv * (z_b * _sigmoid(z_b))).astype(BF16)
    merged = _sigmoid(proj(pgb_ref)) * jnp.dot(y_b, wb_ref[...], preferred_element_type=F32)

    yg = _gelu_tanh(jnp.concatenate([yt_ref[k].T for k in range(SETS_PER_STEP)], axis=0))
    yg = yg * _sigmoid(jnp.dot(yg.astype(BF16), wglu_ref[...], preferred_element_type=F32))
    z_a = proj(pza_ref)
    y_a = (yg * (z_a * _sigmoid(z_a))).astype(BF16)
    merged = merged + _sigmoid(proj(pga_ref)) * jnp.dot(y_a, wa_ref[...], preferred_element_type=F32)

    h = x + jnp.dot(merged.astype(BF16), wo_ref[...], preferred_element_type=F32)
    out = _rmsnorm(h, fg_ref[...])

    slot = step % 2

    @pl.when(step >= 2)
    def _():
        for c in _set_copies(o_hbm, obuf, osem, step - 2, slot, to_hbm=True):
            c.wait()

    for k in range(SETS_PER_STEP):
        obuf[slot, k] = out[k * n_chunks:(k + 1) * n_chunks]
    for c in _set_copies(o_hbm, obuf, osem, step, slot, to_hbm=True):
        c.start()

    @pl.when(step == n_steps - 1)
    def _():
        @pl.when(step >= 1)
        def _():
            for c in _set_copies(o_hbm, obuf, osem, step - 1, 1 - slot, to_hbm=True):
                c.wait()
        for c in _set_copies(o_hbm, obuf, osem, step, slot, to_hbm=True):
            c.wait()


def _block_t(x4, yt, cv3, norm_g, w_in, w_glu, w_a, w_b, w_o, cw, cb, final_g, *, bsz, n_chunks):
    s = SETS_PER_STEP
    tok_spec = pl.BlockSpec((None, n_chunks, s * D_MODEL), lambda b, tp: (b, 0, tp))
    prev_spec = pl.BlockSpec((None, n_chunks, D_MODEL),
                             lambda b, tp: (b, 0, (tp * s + CHUNK - 1) % CHUNK))
    next_spec = pl.BlockSpec((None, n_chunks, D_MODEL), lambda b, tp: (b, 0, (tp * s + s) % CHUNK))
    kern = functools.partial(_block_kernel, n_chunks=n_chunks)
    return pl.pallas_call(
        kern,
        grid=(bsz, CHUNK // s),
        in_specs=[
            pl.BlockSpec(memory_space=pl.ANY),
            pl.BlockSpec((s, D_MODEL, n_chunks), lambda b, tp: (tp, 0, b)),
            tok_spec, prev_spec, next_spec,
            _const_spec((1, D_MODEL)),
            _col_spec(COL_ZA), _col_spec(COL_B), _col_spec(COL_ZB), _col_spec(COL_GA), _col_spec(COL_GB),
            _const_spec((D_MODEL, D_MODEL)),
            _const_spec((D_MODEL, D_MODEL)),
            _const_spec((D_MODEL, D_MODEL)),
            _const_spec((D_MODEL, D_MODEL)),
            _const_spec((3, D_MODEL)),
            _const_spec((1, D_MODEL)),
            _const_spec((1, D_MODEL)),
        ],
        out_specs=pl.BlockSpec(memory_space=pl.ANY),
        out_shape=jax.ShapeDtypeStruct(x4.shape, F32),
        scratch_shapes=([pltpu.VMEM((s * n_chunks, D_MODEL), BF16)]
                        + _set_buffers(n_chunks) + _set_buffers(n_chunks)),
        compiler_params=_params(2),
        name="block_t",
    )(x4, yt, cv3, cv3, cv3, norm_g, w_in, w_in, w_in, w_in, w_in, w_glu, w_a, w_b, w_o, cw, cb,
      final_g)


def kernel(x, norm_g, w_in, lam_re, lam_im, log_dt, ssm_b_re, ssm_b_im, ssm_c_re, ssm_c_im, ssm_d,
           w_glu, conv_w, conv_b, w_branch_a, w_branch_b, w_out, final_g):
    bsz, seq_len, d = x.shape
    assert norm_g.shape[0] == 1 and d == D_MODEL and bsz == SUBLANES
    n_chunks = seq_len // CHUNK
    assert seq_len == n_chunks * CHUNK and n_chunks % LANES == 0

    x4 = x.reshape(bsz, n_chunks, CHUNK, d)
    ng = norm_g[0].reshape(1, d)
    w_bf = w_in[0].astype(BF16)
    wut = w_bf[:, COL_U * d:(COL_U + 1) * d].T

    ut, cv3 = _in_proj_t(x4, ng, wut, w_bf, bsz=bsz, n_chunks=n_chunks)
    mt, wst, woutt, aq = _s5_operators(
        lam_re[0], lam_im[0], log_dt[0], ssm_b_re[0], ssm_b_im[0], ssm_c_re[0], ssm_c_im[0],
        ssm_d[0])
    yt = _s5_t(ut, mt, wst, woutt, aq, n_chunks=n_chunks, batch=bsz)

    out4 = _block_t(
        x4, yt, cv3, ng, w_bf, w_glu[0].astype(BF16), w_branch_a[0].astype(BF16),
        w_branch_b[0].astype(BF16), w_out[0].astype(BF16), conv_w[0], conv_b[0].reshape(1, d),
        final_g.reshape(1, d), bsz=bsz, n_chunks=n_chunks)
    return out4.reshape(bsz, seq_len, d)
```

```python
import functools
import math

import jax
import jax.numpy as jnp
from jax import lax
from jax.experimental import pallas as pl
from jax.experimental.pallas import tpu as pltpu

F32 = jnp.float32
BF16 = jnp.bfloat16

D_MODEL = 1024
SSM_GROUP = 16
N_GROUPS = D_MODEL // SSM_GROUP
STATE = 64
CHUNK = 16
CW = CHUNK * SSM_GROUP
SW = 4 * STATE
HALF = 2 * STATE
LANES = 128
EPS = 1e-6
SUBLANES = 8
SETS_PER_STEP = 2
GROUPS_PER_STEP = 2
OPS_GROUPS_PER_STEP = 8
VMEM_LIMIT_BYTES = 60 * 1024 * 1024

NT_DIMS = (((1,), (1,)), ((), ()))
TN_DIMS = (((0,), (0,)), ((), ()))


def _rmsnorm(x, g):
    ms = jnp.mean(x * x, axis=-1, keepdims=True)
    return x * lax.rsqrt(ms + EPS) * g


def _sigmoid(x):
    return 0.5 * jnp.tanh(0.5 * x) + 0.5


def _gelu_tanh(x):
    c = math.sqrt(2.0 / math.pi)
    return x * (0.5 * (1.0 + jnp.tanh(c * (x + 0.044715 * (x * x * x)))))


def _params(n_axes):
    return pltpu.CompilerParams(
        dimension_semantics=("arbitrary",) * n_axes, vmem_limit_bytes=VMEM_LIMIT_BYTES)


def _const_spec(shape):
    return pl.BlockSpec(shape, lambda *_: (0,) * len(shape), pipeline_mode=pl.Buffered(1))


COL_U, COL_ZA, COL_V, COL_B, COL_C, COL_ZB, COL_GA, COL_GB = range(8)


def _col_spec(col):
    return pl.BlockSpec((D_MODEL, D_MODEL), lambda *_: (0, col), pipeline_mode=pl.Buffered(1))


def _set_copies(x_hbm, buf, sem, step, slot, *, to_hbm=False):
    steps_per_batch = CHUNK // SETS_PER_STEP
    b = step // steps_per_batch
    tl0 = (step % steps_per_batch) * SETS_PER_STEP
    copies = []
    for k in range(SETS_PER_STEP):
        hbm, vmem = x_hbm.at[b, :, tl0 + k, :], buf.at[slot, k]
        src, dst = (vmem, hbm) if to_hbm else (hbm, vmem)
        copies.append(pltpu.make_async_copy(src, dst, sem.at[slot, k]))
    return copies


def _start(copies):
    for c in copies:
        c.start()


def _wait(copies):
    for c in copies:
        c.wait()


def _ring_begin(x_hbm, xbuf, sem, step):
    @pl.when(step == 0)
    def _():
        _start(_set_copies(x_hbm, xbuf, sem, 0, 0))
        _start(_set_copies(x_hbm, xbuf, sem, 1, 1))
        _wait(_set_copies(x_hbm, xbuf, sem, 0, 0))


def _ring_advance(x_hbm, xbuf, sem, step, n_steps, slots):
    @pl.when(step + 2 < n_steps)
    def _():
        _start(_set_copies(x_hbm, xbuf, sem, step + 2, (step + 2) % slots))

    @pl.when(step + 1 < n_steps)
    def _():
        _wait(_set_copies(x_hbm, xbuf, sem, step + 1, (step + 1) % slots))


def _load_sets(buf, slot):
    return jnp.concatenate([buf[slot, k] for k in range(SETS_PER_STEP)], axis=0)


def _set_buffers(n_chunks, slots):
    return [pltpu.VMEM((slots, SETS_PER_STEP, n_chunks, D_MODEL), F32),
            pltpu.SemaphoreType.DMA((slots, SETS_PER_STEP))]


def _in_proj_kernel(x_hbm, g_ref, wut_ref, wv_ref, wc_ref, u_ref, cv_ref, xn_ref, xbuf, xsem,
                    *, n_chunks):
    step = pl.program_id(0) * pl.num_programs(1) + pl.program_id(1)
    n_steps = pl.num_programs(0) * pl.num_programs(1)
    _ring_begin(x_hbm, xbuf, xsem, step)

    @pl.when(step == 0)
    def _():
        xn_ref[0] = _rmsnorm(_load_sets(xbuf, 0), g_ref[...]).astype(BF16)

    _ring_advance(x_hbm, xbuf, xsem, step, n_steps, slots=2)

    nxt = (step + 1) % 2
    xn_ref[nxt] = _rmsnorm(_load_sets(xbuf, nxt), g_ref[...]).astype(BF16)
    xn = xn_ref[step % 2]
    ut = lax.dot_general(wut_ref[...], xn, NT_DIMS, preferred_element_type=F32)
    cv = (jnp.dot(xn, wv_ref[...], preferred_element_type=F32)
          * jnp.dot(xn, wc_ref[...], preferred_element_type=F32)).astype(BF16)
    for k in range(SETS_PER_STEP):
        u_ref[k] = ut[:, k * n_chunks:(k + 1) * n_chunks].astype(BF16)
        cv_ref[:, k * D_MODEL:(k + 1) * D_MODEL] = cv[k * n_chunks:(k + 1) * n_chunks]


def _in_proj_t(x4, norm_g, wut, w_in, *, bsz, n_chunks):
    tok_spec = pl.BlockSpec((None, n_chunks, SETS_PER_STEP * D_MODEL), lambda b, tp: (b, 0, tp))
    return pl.pallas_call(
        functools.partial(_in_proj_kernel, n_chunks=n_chunks),
        grid=(bsz, CHUNK // SETS_PER_STEP),
        in_specs=[pl.BlockSpec(memory_space=pl.ANY), _const_spec((1, D_MODEL)),
                  _const_spec((D_MODEL, D_MODEL)), _col_spec(COL_V), _col_spec(COL_C)],
        out_specs=[pl.BlockSpec((SETS_PER_STEP, D_MODEL, n_chunks), lambda b, tp: (tp, 0, b)),
                   tok_spec],
        out_shape=[jax.ShapeDtypeStruct((CHUNK, D_MODEL, bsz * n_chunks), BF16),
                   jax.ShapeDtypeStruct((bsz, n_chunks, CHUNK * D_MODEL), BF16)],
        scratch_shapes=([pltpu.VMEM((2, SETS_PER_STEP * n_chunks, D_MODEL), BF16)]
                        + _set_buffers(n_chunks, 2)),
        compiler_params=_params(2),
        name="in_proj_t",
    )(x4, norm_g, wut, w_in, w_in)


def _s5_kernel(u_ref, mt_ref, wst_ref, woutt_ref, aq_ref, y_ref, sloc_ref, pf_ref, pb_ref,
               *, n_chunks, batch):
    n = batch * n_chunks
    groups = range(GROUPS_PER_STEP)
    uts = [u_ref[:, j * SSM_GROUP:(j + 1) * SSM_GROUP, :].reshape(CW, n) for j in groups]

    for j in groups:
        wst = wst_ref[j]
        for b in range(batch):
            sl = lax.dot_general(uts[j][:, b * n_chunks:(b + 1) * n_chunks], wst, TN_DIMS,
                                 preferred_element_type=F32)
            sloc_ref[j, 0, pl.ds(b, n_chunks, stride=batch), :] = sl[:, :HALF]
            sloc_ref[j, 1, pl.ds(b, n_chunks, stride=batch), :] = sl[:, HALF:]

    is_fwd = lax.broadcasted_iota(jnp.int32, (batch, HALF), 1) < STATE
    ar = [jnp.broadcast_to(aq_ref[j, 0:1, :], (batch, HALF)) for j in groups]
    ai = [jnp.broadcast_to(aq_ref[j, 1:2, :], (batch, HALF)) for j in groups]

    zero = jnp.zeros((batch, HALF), F32)
    last = n_chunks * batch
    for j in groups:
        for h in range(2):
            pf_ref[j, h, 0:batch, :] = zero
            pb_ref[j, h, last:last + batch, :] = zero

    def scan_step(k, carry):
        rf = pl.multiple_of(k * batch, batch)
        rb = pl.multiple_of((n_chunks - 1 - k) * batch, batch)
        ro = pl.multiple_of((k + 1) * batch, batch)
        new = []
        for j in groups:
            sr, si = carry[j]
            xr = jnp.where(is_fwd, sloc_ref[j, 0, pl.ds(rf, batch), :],
                           sloc_ref[j, 0, pl.ds(rb, batch), :])
            xi = jnp.where(is_fwd, sloc_ref[j, 1, pl.ds(rf, batch), :],
                           sloc_ref[j, 1, pl.ds(rb, batch), :])
            nr = ar[j] * sr - ai[j] * si + xr
            ni = ar[j] * si + ai[j] * sr + xi
            pf_ref[j, 0, pl.ds(ro, batch), :] = nr
            pf_ref[j, 1, pl.ds(ro, batch), :] = ni
            pb_ref[j, 0, pl.ds(rb, batch), :] = nr
            pb_ref[j, 1, pl.ds(rb, batch), :] = ni
            new.append((nr, ni))
        return tuple(new)

    lax.fori_loop(0, n_chunks, scan_step, tuple((zero, zero) for _ in groups), unroll=4)

    fwd_rows = lax.broadcasted_iota(jnp.int32, (n_chunks, HALF), 1) < STATE
    for j in groups:
        mt = mt_ref[j]
        woutt = woutt_ref[j]
        for b in range(batch):
            cols = slice(b * n_chunks, (b + 1) * n_chunks)
            prev = jnp.concatenate(
                [jnp.where(fwd_rows, pf_ref[j, h, pl.ds(b, n_chunks, stride=batch), :],
                           pb_ref[j, h, pl.ds(batch + b, n_chunks, stride=batch), :])
                 for h in range(2)], axis=1)
            yb = jnp.dot(mt, uts[j][:, cols], preferred_element_type=F32)
            yb = yb + lax.dot_general(woutt, prev.astype(BF16), NT_DIMS, preferred_element_type=F32)
            y_ref[:, j * SSM_GROUP:(j + 1) * SSM_GROUP, cols] = yb.reshape(CHUNK, SSM_GROUP, n_chunks)


def _s5_t(ut, mt, wst, woutt, aq, *, n_chunks, batch):
    n = batch * n_chunks
    gps = GROUPS_PER_STEP
    kern = functools.partial(_s5_kernel, n_chunks=n_chunks, batch=batch)
    op_spec = pl.BlockSpec((gps, CW, CW), lambda g: (g, 0, 0))
    act_spec = pl.BlockSpec((CHUNK, gps * SSM_GROUP, n), lambda g: (0, g, 0))
    state_rows = pltpu.VMEM((gps, 2, n + batch, HALF), F32)
    return pl.pallas_call(
        kern,
        grid=(N_GROUPS // gps,),
        in_specs=[act_spec, op_spec, op_spec, op_spec,
                  pl.BlockSpec((gps, 2, HALF), lambda g: (g, 0, 0))],
        out_specs=act_spec,
        out_shape=jax.ShapeDtypeStruct((CHUNK, D_MODEL, n), F32),
        scratch_shapes=[pltpu.VMEM((gps, 2, n, HALF), F32), state_rows, state_rows],
        compiler_params=_params(1),
        name="s5_t",
    )(ut, mt, wst, woutt, aq)


def _s5_ops_kernel(*refs):
    for g in range(OPS_GROUPS_PER_STEP):
        _s5_ops_group(g, *refs)


def _s5_ops_group(g, lam_ref, c_ref, bt_ref, dd_ref, mt_ref, wst_ref, woutt_ref, aq_ref,
                  pwr_ref, pwi_ref):
    lr, li = lam_ref[g, 0:1, :], lam_ref[g, 1:2, :]
    dt = jnp.exp(lam_ref[g, 2:3, :])
    c_re, c_im = c_ref[g, 0], c_ref[g, 1]
    bt_re, bt_im = bt_ref[g, 0], bt_ref[g, 1]

    n_rows = 2 * CHUNK + SUBLANES
    r = lax.broadcasted_iota(jnp.int32, (n_rows, HALF), 0)
    fwd = lax.broadcasted_iota(jnp.int32, (n_rows, HALF), 1) < STATE
    s = r - CHUNK
    e = jnp.where(r < CHUNK, jnp.where(fwd, r + 1, CHUNK - r),
                  jnp.where(r < 2 * CHUNK, jnp.where(fwd, CHUNK - 1 - s, s),
                            jnp.where(r == 2 * CHUNK, CHUNK, 1))).astype(F32)
    mag = jnp.exp(e * (lr * dt))
    ang = e * (li * dt)
    pwr_ref[g] = mag * jnp.cos(ang)
    pwi_ref[g] = mag * jnp.sin(ang)

    def power(row):
        return pwr_ref[g, row:row + 1, :], pwi_ref[g, row:row + 1, :]

    a_re, a_im = power(2 * CHUNK + 1)
    den = lr * lr + li * li
    nr, ni = a_re - 1.0, a_im
    coef_re = (nr * lr + ni * li) / den
    coef_im = (ni * lr - nr * li) / den
    bb_re = coef_re * bt_re - coef_im * bt_im
    bb_im = coef_re * bt_im + coef_im * bt_re

    st_re, st_im = [], []
    for k in range(CHUNK):
        p_re, p_im = power(CHUNK + k)
        st_re.append(p_re * bb_re - p_im * bb_im)
        st_im.append(p_re * bb_im + p_im * bb_re)
    wst = jnp.concatenate([jnp.concatenate(st_re, axis=0), jnp.concatenate(st_im, axis=0)], axis=1)
    wst_ref[g] = wst.astype(BF16)

    out_rows = []
    for k in range(CHUNK):
        p_re, p_im = power(k)
        out_rows.append(jnp.concatenate([c_re * p_re - c_im * p_im, -(c_re * p_im + c_im * p_re)],
                                        axis=1))
    woutt_ref[g] = jnp.concatenate(out_rows, axis=0).astype(BF16)

    aq_ref[g, 0:1, :], aq_ref[g, 1:2, :] = power(2 * CHUNK)

    fwd1 = lax.broadcasted_iota(jnp.int32, (SSM_GROUP, HALF), 1) < STATE
    zero = jnp.zeros_like(c_re)
    cmat = jnp.concatenate([
        jnp.concatenate([jnp.where(fwd1, c_re, zero), -jnp.where(fwd1, c_im, zero)], axis=1),
        jnp.concatenate([jnp.where(fwd1, zero, c_re), -jnp.where(fwd1, zero, c_im)], axis=1)], axis=0)
    kslab = lax.dot_general(cmat, wst, NT_DIMS, precision=lax.Precision.HIGHEST,
                            preferred_element_type=F32)
    k_f = kslab[:SSM_GROUP]
    k_b = kslab[SSM_GROUP:] + dd_ref[g]

    lane = lax.broadcasted_iota(jnp.int32, (SSM_GROUP, CW), 1)
    m_rows = []
    for t in range(CHUNK):
        left = SSM_GROUP * (CHUNK - 1 - t)
        right = SSM_GROUP * t
        a = k_f if left == 0 else jnp.where(lane < CW - left, pltpu.roll(k_f, CW - left, 1), 0.0)
        b = k_b if right == 0 else jnp.where(lane >= right, pltpu.roll(k_b, right, 1), 0.0)
        m_rows.append(a + b)
    mt_ref[g] = jnp.concatenate(m_rows, axis=0).astype(BF16)


def _s5_operators(lam_re, lam_im, log_dt, b_re, b_im, c_re, c_im, ssm_d):
    def lanes(x):
        return jnp.concatenate([x[0], x[1]], axis=-1)

    lam = jnp.stack([lanes(lam_re), lanes(lam_im),
                     lanes(jnp.broadcast_to(log_dt[..., None], lam_re.shape))], axis=1)
    c = jnp.stack([lanes(c_re), lanes(c_im)], axis=1)
    bt = jnp.stack([lanes(b_re.transpose(0, 1, 3, 2)), lanes(b_im.transpose(0, 1, 3, 2))], axis=1)
    dd = ssm_d.reshape(N_GROUPS, SSM_GROUP)[:, :, None] * jnp.eye(SSM_GROUP, CW, dtype=F32)

    gps = OPS_GROUPS_PER_STEP
    op_spec = pl.BlockSpec((gps, CW, CW), lambda g: (g, 0, 0))
    par_spec = pl.BlockSpec((gps, 2, SSM_GROUP, HALF), lambda g: (g, 0, 0, 0))
    op_shape = jax.ShapeDtypeStruct((N_GROUPS, CW, CW), BF16)
    return pl.pallas_call(
        _s5_ops_kernel,
        grid=(N_GROUPS // gps,),
        in_specs=[pl.BlockSpec((gps, 3, HALF), lambda g: (g, 0, 0)), par_spec, par_spec,
                  pl.BlockSpec((gps, SSM_GROUP, CW), lambda g: (g, 0, 0))],
        out_specs=[op_spec, op_spec, op_spec, pl.BlockSpec((gps, 2, HALF), lambda g: (g, 0, 0))],
        out_shape=[op_shape, op_shape, op_shape, jax.ShapeDtypeStruct((N_GROUPS, 2, HALF), F32)],
        scratch_shapes=[pltpu.VMEM((gps, 2 * CHUNK + SUBLANES, HALF), F32)] * 2,
        compiler_params=_params(1),
        name="s5_ops",
    )(lam, c, bt, dd)


def _block_kernel(x_hbm, yt_ref, cv_ref, cvprev_ref, cvnext_ref, ng_ref, pza_ref, pb_ref, pzb_ref,
                  pga_ref, pgb_ref, wglu_ref, wa_ref, wb_ref, wo_ref, cw_ref, cb_ref, fg_ref,
                  o_hbm, xn_ref, h_ref, xbuf, xsem, obuf, osem, *, n_chunks):
    tp = pl.program_id(1)
    step = pl.program_id(0) * pl.num_programs(1) + tp
    n_steps = pl.num_programs(0) * pl.num_programs(1)
    _ring_begin(x_hbm, xbuf, xsem, step)

    @pl.when(step == 0)
    def _():
        xn_ref[0] = _rmsnorm(_load_sets(xbuf, 0), ng_ref[...]).astype(BF16)
        h_ref[...] = jnp.zeros_like(h_ref)

    _ring_advance(x_hbm, xbuf, xsem, step, n_steps, slots=3)

    @pl.when(step >= 3)
    def _():
        _wait(_set_copies(o_hbm, obuf, osem, step - 3, (step - 1) % 2, to_hbm=True))

    prev_out = _rmsnorm(h_ref[...], fg_ref[...])
    for k in range(SETS_PER_STEP):
        obuf[(step + 1) % 2, k] = prev_out[k * n_chunks:(k + 1) * n_chunks]
    nxt = (step + 1) % 2
    xn_ref[nxt] = _rmsnorm(_load_sets(xbuf, (step + 1) % 3), ng_ref[...]).astype(BF16)
    x = _load_sets(xbuf, step % 3)

    def proj(w_ref):
        return jnp.dot(xn_ref[step % 2], w_ref[...], preferred_element_type=F32)

    row = lax.broadcasted_iota(jnp.int32, (n_chunks, D_MODEL), 0)
    cvprev = cvprev_ref[...].astype(F32)
    cvnext = cvnext_ref[...].astype(F32)
    cvprev = jnp.where(tp == 0, jnp.where(row == 0, 0.0, pltpu.roll(cvprev, 1, 0)), cvprev)
    cvnext = jnp.where(tp == CHUNK // SETS_PER_STEP - 1,
                       jnp.where(row == n_chunks - 1, 0.0, pltpu.roll(cvnext, n_chunks - 1, 0)),
                       cvnext)
    cvs = [cv_ref[:, k * D_MODEL:(k + 1) * D_MODEL].astype(F32) for k in range(SETS_PER_STEP)]
    cvm = jnp.concatenate([cvprev] + cvs[:-1], axis=0)
    cvp = jnp.concatenate(cvs[1:] + [cvnext], axis=0)
    cw = cw_ref[...]
    conv = cb_ref[...] + cvm * cw[0:1] + jnp.concatenate(cvs, axis=0) * cw[1:2] + cvp * cw[2:3]

    z_b = proj(pzb_ref)
    y_b = (proj(pb_ref) * conv * (z_b * _sigmoid(z_b))).astype(BF16)
    merged = _sigmoid(proj(pgb_ref)) * jnp.dot(y_b, wb_ref[...], preferred_element_type=F32)

    yg = _gelu_tanh(jnp.concatenate([yt_ref[k].T for k in range(SETS_PER_STEP)], axis=0))
    yg = yg * _sigmoid(jnp.dot(yg.astype(BF16), wglu_ref[...], preferred_element_type=F32))
    z_a = proj(pza_ref)
    y_a = (yg * (z_a * _sigmoid(z_a))).astype(BF16)
    merged = merged + _sigmoid(proj(pga_ref)) * jnp.dot(y_a, wa_ref[...], preferred_element_type=F32)

    h_ref[...] = x + jnp.dot(merged.astype(BF16), wo_ref[...], preferred_element_type=F32)

    @pl.when(step >= 1)
    def _():
        _start(_set_copies(o_hbm, obuf, osem, step - 1, (step - 1) % 2, to_hbm=True))

    @pl.when(step == n_steps - 1)
    def _():
        _wait(_set_copies(o_hbm, obuf, osem, step - 2, step % 2, to_hbm=True))
        out = _rmsnorm(h_ref[...], fg_ref[...])
        for k in range(SETS_PER_STEP):
            obuf[step % 2, k] = out[k * n_chunks:(k + 1) * n_chunks]
        _start(_set_copies(o_hbm, obuf, osem, step, step % 2, to_hbm=True))
        _wait(_set_copies(o_hbm, obuf, osem, step - 1, (step - 1) % 2, to_hbm=True))
        _wait(_set_copies(o_hbm, obuf, osem, step, step % 2, to_hbm=True))


def _block_t(x4, yt, cv3, norm_g, w_in, w_glu, w_a, w_b, w_o, cw, cb, final_g, *, bsz, n_chunks):
    s = SETS_PER_STEP
    tok_spec = pl.BlockSpec((None, n_chunks, s * D_MODEL), lambda b, tp: (b, 0, tp))
    prev_spec = pl.BlockSpec((None, n_chunks, D_MODEL),
                             lambda b, tp: (b, 0, (tp * s + CHUNK - 1) % CHUNK))
    next_spec = pl.BlockSpec((None, n_chunks, D_MODEL), lambda b, tp: (b, 0, (tp * s + s) % CHUNK))
    kern = functools.partial(_block_kernel, n_chunks=n_chunks)
    return pl.pallas_call(
        kern,
        grid=(bsz, CHUNK // s),
        in_specs=[
            pl.BlockSpec(memory_space=pl.ANY),
            pl.BlockSpec((s, D_MODEL, n_chunks), lambda b, tp: (tp, 0, b)),
            tok_spec, prev_spec, next_spec,
            _const_spec((1, D_MODEL)),
            _col_spec(COL_ZA), _col_spec(COL_B), _col_spec(COL_ZB), _col_spec(COL_GA), _col_spec(COL_GB),
            _const_spec((D_MODEL, D_MODEL)),
            _const_spec((D_MODEL, D_MODEL)),
            _const_spec((D_MODEL, D_MODEL)),
            _const_spec((D_MODEL, D_MODEL)),
            _const_spec((3, D_MODEL)),
            _const_spec((1, D_MODEL)),
            _const_spec((1, D_MODEL)),
        ],
        out_specs=pl.BlockSpec(memory_space=pl.ANY),
        out_shape=jax.ShapeDtypeStruct(x4.shape, F32),
        scratch_shapes=([pltpu.VMEM((2, s * n_chunks, D_MODEL), BF16),
                         pltpu.VMEM((s * n_chunks, D_MODEL), F32)]
                        + _set_buffers(n_chunks, 3) + _set_buffers(n_chunks, 2)),
        compiler_params=_params(2),
        name="block_t",
    )(x4, yt, cv3, cv3, cv3, norm_g, w_in, w_in, w_in, w_in, w_in, w_glu, w_a, w_b, w_o, cw, cb,
      final_g)


def kernel(x, norm_g, w_in, lam_re, lam_im, log_dt, ssm_b_re, ssm_b_im, ssm_c_re, ssm_c_im, ssm_d,
           w_glu, conv_w, conv_b, w_branch_a, w_branch_b, w_out, final_g):
    bsz, seq_len, d = x.shape
    assert norm_g.shape[0] == 1 and d == D_MODEL and bsz == SUBLANES
    n_chunks = seq_len // CHUNK
    assert seq_len == n_chunks * CHUNK and n_chunks % LANES == 0
    assert bsz * (CHUNK // SETS_PER_STEP) >= 3

    x4 = x.reshape(bsz, n_chunks, CHUNK, d)
    ng = norm_g[0].reshape(1, d)
    w_bf = w_in[0].astype(BF16)
    wut = w_bf[:, COL_U * d:(COL_U + 1) * d].T

    ut, cv3 = _in_proj_t(x4, ng, wut, w_bf, bsz=bsz, n_chunks=n_chunks)
    mt, wst, woutt, aq = _s5_operators(
        lam_re[0], lam_im[0], log_dt[0], ssm_b_re[0], ssm_b_im[0], ssm_c_re[0], ssm_c_im[0],
        ssm_d[0])
    yt = _s5_t(ut, mt, wst, woutt, aq, n_chunks=n_chunks, batch=bsz)

    out4 = _block_t(
        x4, yt, cv3, ng, w_bf, w_glu[0].astype(BF16), w_branch_a[0].astype(BF16),
        w_branch_b[0].astype(BF16), w_out[0].astype(BF16), conv_w[0], conv_b[0].reshape(1, d),
        final_g.reshape(1, d), bsz=bsz, n_chunks=n_chunks)
    return out4.reshape(bsz, seq_len, d)
```

```python
import functools
import math

import jax
import jax.numpy as jnp
from jax import lax
from jax.experimental import pallas as pl
from jax.experimental.pallas import tpu as pltpu

F32 = jnp.float32
BF16 = jnp.bfloat16

D_MODEL = 1024
SSM_GROUP = 16
N_GROUPS = D_MODEL // SSM_GROUP
STATE = 64
CHUNK = 16
CW = CHUNK * SSM_GROUP
SW = 4 * STATE
HALF = 2 * STATE
LANES = 128
EPS = 1e-6
SUBLANES = 8
SETS_PER_STEP = 2
GROUPS_PER_STEP = 4
OPS_GROUPS_PER_STEP = 8
VMEM_LIMIT_BYTES = 60 * 1024 * 1024

NT_DIMS = (((1,), (1,)), ((), ()))
TN_DIMS = (((0,), (0,)), ((), ()))


def _rmsnorm(x, g):
    ms = jnp.mean(x * x, axis=-1, keepdims=True)
    return x * lax.rsqrt(ms + EPS) * g


def _sigmoid(x):
    return 0.5 * jnp.tanh(0.5 * x) + 0.5


def _gelu_tanh(x):
    c = math.sqrt(2.0 / math.pi)
    return x * (0.5 * (1.0 + jnp.tanh(c * (x + 0.044715 * (x * x * x)))))


def _params(n_axes):
    return pltpu.CompilerParams(
        dimension_semantics=("arbitrary",) * n_axes, vmem_limit_bytes=VMEM_LIMIT_BYTES)


def _const_spec(shape):
    return pl.BlockSpec(shape, lambda *_: (0,) * len(shape), pipeline_mode=pl.Buffered(1))


COL_U, COL_ZA, COL_V, COL_B, COL_C, COL_ZB, COL_GA, COL_GB = range(8)


def _col_spec(col):
    return pl.BlockSpec((D_MODEL, D_MODEL), lambda *_: (0, col), pipeline_mode=pl.Buffered(1))


def _set_copies(x_hbm, buf, sem, step, slot, *, to_hbm=False):
    steps_per_batch = CHUNK // SETS_PER_STEP
    b = step // steps_per_batch
    tl0 = (step % steps_per_batch) * SETS_PER_STEP
    copies = []
    for k in range(SETS_PER_STEP):
        hbm, vmem = x_hbm.at[b, :, tl0 + k, :], buf.at[slot, k]
        src, dst = (vmem, hbm) if to_hbm else (hbm, vmem)
        copies.append(pltpu.make_async_copy(src, dst, sem.at[slot, k]))
    return copies


def _fetch_sets(x_hbm, xbuf, sem, step, n_steps):
    slot = step % 2

    @pl.when(step == 0)
    def _():
        for c in _set_copies(x_hbm, xbuf, sem, step, slot):
            c.start()

    @pl.when(step + 1 < n_steps)
    def _():
        for c in _set_copies(x_hbm, xbuf, sem, step + 1, 1 - slot):
            c.start()

    for c in _set_copies(x_hbm, xbuf, sem, step, slot):
        c.wait()
    return jnp.concatenate([xbuf[slot, k] for k in range(SETS_PER_STEP)], axis=0)


def _set_buffers(n_chunks):
    return [pltpu.VMEM((2, SETS_PER_STEP, n_chunks, D_MODEL), F32),
            pltpu.SemaphoreType.DMA((2, SETS_PER_STEP))]


def _in_proj_kernel(x_hbm, g_ref, wut_ref, wv_ref, wc_ref, u_ref, cv_ref, xbuf, xsem, *, n_chunks):
    step = pl.program_id(0) * pl.num_programs(1) + pl.program_id(1)
    n_steps = pl.num_programs(0) * pl.num_programs(1)
    x = _fetch_sets(x_hbm, xbuf, xsem, step, n_steps)
    xn = _rmsnorm(x, g_ref[...]).astype(BF16)
    ut = lax.dot_general(wut_ref[...], xn, NT_DIMS, preferred_element_type=F32)
    cv = (jnp.dot(xn, wv_ref[...], preferred_element_type=F32)
          * jnp.dot(xn, wc_ref[...], preferred_element_type=F32)).astype(BF16)
    for k in range(SETS_PER_STEP):
        u_ref[k] = ut[:, k * n_chunks:(k + 1) * n_chunks].astype(BF16)
        cv_ref[:, k * D_MODEL:(k + 1) * D_MODEL] = cv[k * n_chunks:(k + 1) * n_chunks]


def _in_proj_t(x4, norm_g, wut, w_in, *, bsz, n_chunks):
    tok_spec = pl.BlockSpec((None, n_chunks, SETS_PER_STEP * D_MODEL), lambda b, tp: (b, 0, tp))
    return pl.pallas_call(
        functools.partial(_in_proj_kernel, n_chunks=n_chunks),
        grid=(bsz, CHUNK // SETS_PER_STEP),
        in_specs=[pl.BlockSpec(memory_space=pl.ANY), _const_spec((1, D_MODEL)),
                  _const_spec((D_MODEL, D_MODEL)), _col_spec(COL_V), _col_spec(COL_C)],
        out_specs=[pl.BlockSpec((SETS_PER_STEP, D_MODEL, n_chunks), lambda b, tp: (tp, 0, b)),
                   tok_spec],
        out_shape=[jax.ShapeDtypeStruct((CHUNK, D_MODEL, bsz * n_chunks), BF16),
                   jax.ShapeDtypeStruct((bsz, n_chunks, CHUNK * D_MODEL), BF16)],
        scratch_shapes=_set_buffers(n_chunks),
        compiler_params=_params(2),
        name="in_proj_t",
    )(x4, norm_g, wut, w_in, w_in)


def _s5_kernel(u_ref, mt_ref, wst_ref, woutt_ref, aq_ref, y_ref, sloc_ref, pf_ref, pb_ref,
               *, n_chunks, batch):
    n = batch * n_chunks
    groups = range(GROUPS_PER_STEP)
    uts = [u_ref[:, j * SSM_GROUP:(j + 1) * SSM_GROUP, :].reshape(CW, n) for j in groups]

    for j in groups:
        wst = wst_ref[j]
        for b in range(batch):
            sl = lax.dot_general(uts[j][:, b * n_chunks:(b + 1) * n_chunks], wst, TN_DIMS,
                                 preferred_element_type=F32)
            sloc_ref[j, 0, pl.ds(b, n_chunks, stride=batch), :] = sl[:, :HALF]
            sloc_ref[j, 1, pl.ds(b, n_chunks, stride=batch), :] = sl[:, HALF:]

    is_fwd = lax.broadcasted_iota(jnp.int32, (batch, HALF), 1) < STATE
    ar = [jnp.broadcast_to(aq_ref[j, 0:1, :], (batch, HALF)) for j in groups]
    ai = [jnp.broadcast_to(aq_ref[j, 1:2, :], (batch, HALF)) for j in groups]

    zero = jnp.zeros((batch, HALF), F32)
    last = n_chunks * batch
    for j in groups:
        for h in range(2):
            pf_ref[j, h, 0:batch, :] = zero
            pb_ref[j, h, last:last + batch, :] = zero

    def scan_step(k, carry):
        rf = k * batch
        rb = (n_chunks - 1 - k) * batch
        ro = (k + 1) * batch
        new = []
        for j in groups:
            sr, si = carry[j]
            xr = jnp.where(is_fwd, sloc_ref[j, 0, pl.ds(rf, batch), :],
                           sloc_ref[j, 0, pl.ds(rb, batch), :])
            xi = jnp.where(is_fwd, sloc_ref[j, 1, pl.ds(rf, batch), :],
                           sloc_ref[j, 1, pl.ds(rb, batch), :])
            nr = ar[j] * sr - ai[j] * si + xr
            ni = ar[j] * si + ai[j] * sr + xi
            pf_ref[j, 0, pl.ds(ro, batch), :] = nr
            pf_ref[j, 1, pl.ds(ro, batch), :] = ni
            pb_ref[j, 0, pl.ds(rb, batch), :] = nr
            pb_ref[j, 1, pl.ds(rb, batch), :] = ni
            new.append((nr, ni))
        return tuple(new)

    carry = tuple((zero, zero) for _ in groups)
    for k in range(n_chunks):
        carry = scan_step(k, carry)

    fwd_rows = lax.broadcasted_iota(jnp.int32, (n_chunks, HALF), 1) < STATE
    for j in groups:
        mt = mt_ref[j]
        woutt = woutt_ref[j]
        for b in range(batch):
            cols = slice(b * n_chunks, (b + 1) * n_chunks)
            prev = jnp.concatenate(
                [jnp.where(fwd_rows, pf_ref[j, h, pl.ds(b, n_chunks, stride=batch), :],
                           pb_ref[j, h, pl.ds(batch + b, n_chunks, stride=batch), :])
                 for h in range(2)], axis=1)
            yb = jnp.dot(mt, uts[j][:, cols], preferred_element_type=F32)
            yb = yb + lax.dot_general(woutt, prev.astype(BF16), NT_DIMS, preferred_element_type=F32)
            y_ref[:, j * SSM_GROUP:(j + 1) * SSM_GROUP, cols] = yb.reshape(CHUNK, SSM_GROUP, n_chunks)


def _s5_t(ut, mt, wst, woutt, aq, *, n_chunks, batch):
    n = batch * n_chunks
    gps = GROUPS_PER_STEP
    kern = functools.partial(_s5_kernel, n_chunks=n_chunks, batch=batch)
    op_spec = pl.BlockSpec((gps, CW, CW), lambda g: (g, 0, 0))
    act_spec = pl.BlockSpec((CHUNK, gps * SSM_GROUP, n), lambda g: (0, g, 0))
    state_rows = pltpu.VMEM((gps, 2, n + batch, HALF), F32)
    return pl.pallas_call(
        kern,
        grid=(N_GROUPS // gps,),
        in_specs=[act_spec, op_spec, op_spec, op_spec,
                  pl.BlockSpec((gps, 2, HALF), lambda g: (g, 0, 0))],
        out_specs=act_spec,
        out_shape=jax.ShapeDtypeStruct((CHUNK, D_MODEL, n), F32),
        scratch_shapes=[pltpu.VMEM((gps, 2, n, HALF), F32), state_rows, state_rows],
        compiler_params=_params(1),
        name="s5_t",
    )(ut, mt, wst, woutt, aq)


def _s5_ops_kernel(*refs):
    for g in range(OPS_GROUPS_PER_STEP):
        _s5_ops_group(g, *refs)


def _s5_ops_group(g, lam_ref, c_ref, bt_ref, dd_ref, mt_ref, wst_ref, woutt_ref, aq_ref,
                  pwr_ref, pwi_ref):
    lr, li = lam_ref[g, 0:1, :], lam_ref[g, 1:2, :]
    dt = jnp.exp(lam_ref[g, 2:3, :])
    c_re, c_im = c_ref[g, 0], c_ref[g, 1]
    bt_re, bt_im = bt_ref[g, 0], bt_ref[g, 1]

    n_rows = 2 * CHUNK + SUBLANES
    r = lax.broadcasted_iota(jnp.int32, (n_rows, HALF), 0)
    fwd = lax.broadcasted_iota(jnp.int32, (n_rows, HALF), 1) < STATE
    s = r - CHUNK
    e = jnp.where(r < CHUNK, jnp.where(fwd, r + 1, CHUNK - r),
                  jnp.where(r < 2 * CHUNK, jnp.where(fwd, CHUNK - 1 - s, s),
                            jnp.where(r == 2 * CHUNK, CHUNK, 1))).astype(F32)
    mag = jnp.exp(e * (lr * dt))
    ang = e * (li * dt)
    pwr_ref[g] = mag * jnp.cos(ang)
    pwi_ref[g] = mag * jnp.sin(ang)

    def power(row):
        return pwr_ref[g, row:row + 1, :], pwi_ref[g, row:row + 1, :]

    a_re, a_im = power(2 * CHUNK + 1)
    den = lr * lr + li * li
    nr, ni = a_re - 1.0, a_im
    coef_re = (nr * lr + ni * li) / den
    coef_im = (ni * lr - nr * li) / den
    bb_re = coef_re * bt_re - coef_im * bt_im
    bb_im = coef_re * bt_im + coef_im * bt_re

    st_re, st_im = [], []
    for k in range(CHUNK):
        p_re, p_im = power(CHUNK + k)
        st_re.append(p_re * bb_re - p_im * bb_im)
        st_im.append(p_re * bb_im + p_im * bb_re)
    wst = jnp.concatenate([jnp.concatenate(st_re, axis=0), jnp.concatenate(st_im, axis=0)], axis=1)
    wst_ref[g] = wst.astype(BF16)

    out_rows = []
    for k in range(CHUNK):
        p_re, p_im = power(k)
        out_rows.append(jnp.concatenate([c_re * p_re - c_im * p_im, -(c_re * p_im + c_im * p_re)],
                                        axis=1))
    woutt_ref[g] = jnp.concatenate(out_rows, axis=0).astype(BF16)

    aq_ref[g, 0:1, :], aq_ref[g, 1:2, :] = power(2 * CHUNK)

    fwd1 = lax.broadcasted_iota(jnp.int32, (SSM_GROUP, HALF), 1) < STATE
    zero = jnp.zeros_like(c_re)
    cmat = jnp.concatenate([
        jnp.concatenate([jnp.where(fwd1, c_re, zero), -jnp.where(fwd1, c_im, zero)], axis=1),
        jnp.concatenate([jnp.where(fwd1, zero, c_re), -jnp.where(fwd1, zero, c_im)], axis=1)], axis=0)
    kslab = lax.dot_general(cmat, wst, NT_DIMS, precision=lax.Precision.HIGHEST,
                            preferred_element_type=F32)
    k_f = kslab[:SSM_GROUP]
    k_b = kslab[SSM_GROUP:] + dd_ref[g]

    lane = lax.broadcasted_iota(jnp.int32, (SSM_GROUP, CW), 1)
    m_rows = []
    for t in range(CHUNK):
        left = SSM_GROUP * (CHUNK - 1 - t)
        right = SSM_GROUP * t
        a = k_f if left == 0 else jnp.where(lane < CW - left, pltpu.roll(k_f, CW - left, 1), 0.0)
        b = k_b if right == 0 else jnp.where(lane >= right, pltpu.roll(k_b, right, 1), 0.0)
        m_rows.append(a + b)
    mt_ref[g] = jnp.concatenate(m_rows, axis=0).astype(BF16)


def _s5_operators(lam_re, lam_im, log_dt, b_re, b_im, c_re, c_im, ssm_d):
    def lanes(x):
        return jnp.concatenate([x[0], x[1]], axis=-1)

    lam = jnp.stack([lanes(lam_re), lanes(lam_im),
                     lanes(jnp.broadcast_to(log_dt[..., None], lam_re.shape))], axis=1)
    c = jnp.stack([lanes(c_re), lanes(c_im)], axis=1)
    bt = jnp.stack([lanes(b_re.transpose(0, 1, 3, 2)), lanes(b_im.transpose(0, 1, 3, 2))], axis=1)
    dd = ssm_d.reshape(N_GROUPS, SSM_GROUP)[:, :, None] * jnp.eye(SSM_GROUP, CW, dtype=F32)

    gps = OPS_GROUPS_PER_STEP
    op_spec = pl.BlockSpec((gps, CW, CW), lambda g: (g, 0, 0))
    par_spec = pl.BlockSpec((gps, 2, SSM_GROUP, HALF), lambda g: (g, 0, 0, 0))
    op_shape = jax.ShapeDtypeStruct((N_GROUPS, CW, CW), BF16)
    return pl.pallas_call(
        _s5_ops_kernel,
        grid=(N_GROUPS // gps,),
        in_specs=[pl.BlockSpec((gps, 3, HALF), lambda g: (g, 0, 0)), par_spec, par_spec,
                  pl.BlockSpec((gps, SSM_GROUP, CW), lambda g: (g, 0, 0))],
        out_specs=[op_spec, op_spec, op_spec, pl.BlockSpec((gps, 2, HALF), lambda g: (g, 0, 0))],
        out_shape=[op_shape, op_shape, op_shape, jax.ShapeDtypeStruct((N_GROUPS, 2, HALF), F32)],
        scratch_shapes=[pltpu.VMEM((gps, 2 * CHUNK + SUBLANES, HALF), F32)] * 2,
        compiler_params=_params(1),
        name="s5_ops",
    )(lam, c, bt, dd)


def _block_kernel(x_hbm, yt_ref, cv_ref, cvprev_ref, cvnext_ref, ng_ref, pza_ref, pb_ref, pzb_ref,
                  pga_ref, pgb_ref, wglu_ref, wa_ref, wb_ref, wo_ref, cw_ref, cb_ref, fg_ref,
                  o_hbm, xn_ref, xbuf, xsem, obuf, osem, *, n_chunks):
    tp = pl.program_id(1)
    step = pl.program_id(0) * pl.num_programs(1) + tp
    n_steps = pl.num_programs(0) * pl.num_programs(1)
    x = _fetch_sets(x_hbm, xbuf, xsem, step, n_steps)
    xn_ref[...] = _rmsnorm(x, ng_ref[...]).astype(BF16)

    def proj(w_ref):
        return jnp.dot(xn_ref[...], w_ref[...], preferred_element_type=F32)

    row = lax.broadcasted_iota(jnp.int32, (n_chunks, D_MODEL), 0)
    cvprev = cvprev_ref[...].astype(F32)
    cvnext = cvnext_ref[...].astype(F32)
    cvprev = jnp.where(tp == 0, jnp.where(row == 0, 0.0, pltpu.roll(cvprev, 1, 0)), cvprev)
    cvnext = jnp.where(tp == CHUNK // SETS_PER_STEP - 1,
                       jnp.where(row == n_chunks - 1, 0.0, pltpu.roll(cvnext, n_chunks - 1, 0)),
                       cvnext)
    cvs = [cv_ref[:, k * D_MODEL:(k + 1) * D_MODEL].astype(F32) for k in range(SETS_PER_STEP)]
    cvm = jnp.concatenate([cvprev] + cvs[:-1], axis=0)
    cvp = jnp.concatenate(cvs[1:] + [cvnext], axis=0)
    cw = cw_ref[...]
    conv = cb_ref[...] + cvm * cw[0:1] + jnp.concatenate(cvs, axis=0) * cw[1:2] + cvp * cw[2:3]

    z_b = proj(pzb_ref)
    y_b = (proj(pb_ref) * conv * (z_b * _sigmoid(z_b))).astype(BF16)
    merged = _sigmoid(proj(pgb_ref)) * jnp.dot(y_b, wb_ref[...], preferred_element_type=F32)

    yg = _gelu_tanh(jnp.concatenate([yt_ref[k].T for k in range(SETS_PER_STEP)], axis=0))
    yg = yg * _sigmoid(jnp.dot(yg.astype(BF16), wglu_ref[...], preferred_element_type=F32))
    z_a = proj(pza_ref)
    y_a = (yg * (z_a * _sigmoid(z_a))).astype(BF16)
    merged = merged + _sigmoid(proj(pga_ref)) * jnp.dot(y_a, wa_ref[...], preferred_element_type=F32)

    h = x + jnp.dot(merged.astype(BF16), wo_ref[...], preferred_element_type=F32)
    out = _rmsnorm(h, fg_ref[...])

    slot = step % 2

    @pl.when(step >= 2)
    def _():
        for c in _set_copies(o_hbm, obuf, osem, step - 2, slot, to_hbm=True):
            c.wait()

    for k in range(SETS_PER_STEP):
        obuf[slot, k] = out[k * n_chunks:(k + 1) * n_chunks]
    for c in _set_copies(o_hbm, obuf, osem, step, slot, to_hbm=True):
        c.start()

    @pl.when(step == n_steps - 1)
    def _():
        @pl.when(step >= 1)
        def _():
            for c in _set_copies(o_hbm, obuf, osem, step - 1, 1 - slot, to_hbm=True):
                c.wait()
        for c in _set_copies(o_hbm, obuf, osem, step, slot, to_hbm=True):
            c.wait()


def _block_t(x4, yt, cv3, norm_g, w_in, w_glu, w_a, w_b, w_o, cw, cb, final_g, *, bsz, n_chunks):
    s = SETS_PER_STEP
    tok_spec = pl.BlockSpec((None, n_chunks, s * D_MODEL), lambda b, tp: (b, 0, tp))
    prev_spec = pl.BlockSpec((None, n_chunks, D_MODEL),
                             lambda b, tp: (b, 0, (tp * s + CHUNK - 1) % CHUNK))
    next_spec = pl.BlockSpec((None, n_chunks, D_MODEL), lambda b, tp: (b, 0, (tp * s + s) % CHUNK))
    kern = functools.partial(_block_kernel, n_chunks=n_chunks)
    return pl.pallas_call(
        kern,
        grid=(bsz, CHUNK // s),
        in_specs=[
            pl.BlockSpec(memory_space=pl.ANY),
            pl.BlockSpec((s, D_MODEL, n_chunks), lambda b, tp: (tp, 0, b)),
            tok_spec, prev_spec, next_spec,
            _const_spec((1, D_MODEL)),
            _col_spec(COL_ZA), _col_spec(COL_B), _col_spec(COL_ZB), _col_spec(COL_GA), _col_spec(COL_GB),
            _const_spec((D_MODEL, D_MODEL)),
            _const_spec((D_MODEL, D_MODEL)),
            _const_spec((D_MODEL, D_MODEL)),
            _const_spec((D_MODEL, D_MODEL)),
            _const_spec((3, D_MODEL)),
            _const_spec((1, D_MODEL)),
            _const_spec((1, D_MODEL)),
        ],
        out_specs=pl.BlockSpec(memory_space=pl.ANY),
        out_shape=jax.ShapeDtypeStruct(x4.shape, F32),
        scratch_shapes=([pltpu.VMEM((s * n_chunks, D_MODEL), BF16)]
                        + _set_buffers(n_chunks) + _set_buffers(n_chunks)),
        compiler_params=_params(2),
        name="block_t",
    )(x4, yt, cv3, cv3, cv3, norm_g, w_in, w_in, w_in, w_in, w_in, w_glu, w_a, w_b, w_o, cw, cb,
      final_g)


def kernel(x, norm_g, w_in, lam_re, lam_im, log_dt, ssm_b_re, ssm_b_im, ssm_c_re, ssm_c_im, ssm_d,
           w_glu, conv_w, conv_b, w_branch_a, w_branch_b, w_out, final_g):
    bsz, seq_len, d = x.shape
    assert norm_g.shape[0] == 1 and d == D_MODEL and bsz == SUBLANES
    n_chunks = seq_len // CHUNK
    assert seq_len == n_chunks * CHUNK and n_chunks % LANES == 0

    x4 = x.reshape(bsz, n_chunks, CHUNK, d)
    ng = norm_g[0].reshape(1, d)
    w_bf = w_in[0].astype(BF16)
    wut = w_bf[:, COL_U * d:(COL_U + 1) * d].T

    ut, cv3 = _in_proj_t(x4, ng, wut, w_bf, bsz=bsz, n_chunks=n_chunks)
    mt, wst, woutt, aq = _s5_operators(
        lam_re[0], lam_im[0], log_dt[0], ssm_b_re[0], ssm_b_im[0], ssm_c_re[0], ssm_c_im[0],
        ssm_d[0])
    yt = _s5_t(ut, mt, wst, woutt, aq, n_chunks=n_chunks, batch=bsz)

    out4 = _block_t(
        x4, yt, cv3, ng, w_bf, w_glu[0].astype(BF16), w_branch_a[0].astype(BF16),
        w_branch_b[0].astype(BF16), w_out[0].astype(BF16), conv_w[0], conv_b[0].reshape(1, d),
        final_g.reshape(1, d), bsz=bsz, n_chunks=n_chunks)
    return out4.reshape(bsz, seq_len, d)
```

```python
import functools
import math

import jax
import jax.numpy as jnp
from jax import lax
from jax.experimental import pallas as pl
from jax.experimental.pallas import tpu as pltpu

F32 = jnp.float32
BF16 = jnp.bfloat16

D_MODEL = 1024
SSM_GROUP = 16
N_GROUPS = D_MODEL // SSM_GROUP
STATE = 64
CHUNK = 16
CW = CHUNK * SSM_GROUP
SW = 4 * STATE
HALF = 2 * STATE
LANES = 128
EPS = 1e-6
SUBLANES = 8
IN_SETS_PER_STEP = 4
SETS_PER_STEP = 2
GROUPS_PER_STEP = 4
OPS_GROUPS_PER_STEP = 8
VMEM_LIMIT_BYTES = 60 * 1024 * 1024

NT_DIMS = (((1,), (1,)), ((), ()))
TN_DIMS = (((0,), (0,)), ((), ()))


def _rmsnorm(x, g):
    ms = jnp.mean(x * x, axis=-1, keepdims=True)
    return x * lax.rsqrt(ms + EPS) * g


def _sigmoid(x):
    return 0.5 * jnp.tanh(0.5 * x) + 0.5


def _gelu_tanh(x):
    c = math.sqrt(2.0 / math.pi)
    return x * (0.5 * (1.0 + jnp.tanh(c * (x + 0.044715 * (x * x * x)))))


def _params(n_axes):
    return pltpu.CompilerParams(
        dimension_semantics=("arbitrary",) * n_axes, vmem_limit_bytes=VMEM_LIMIT_BYTES)


def _const_spec(shape):
    return pl.BlockSpec(shape, lambda *_: (0,) * len(shape), pipeline_mode=pl.Buffered(1))


COL_U, COL_ZA, COL_V, COL_B, COL_C, COL_ZB, COL_GA, COL_GB = range(8)


def _col_spec(col):
    return pl.BlockSpec((D_MODEL, D_MODEL), lambda *_: (0, col), pipeline_mode=pl.Buffered(1))


def _set_copies(x_hbm, buf, sem, step, slot, sets, *, to_hbm=False):
    steps_per_batch = CHUNK // sets
    b = step // steps_per_batch
    tl0 = (step % steps_per_batch) * sets
    copies = []
    for k in range(sets):
        hbm, vmem = x_hbm.at[b, :, tl0 + k, :], buf.at[slot, k]
        src, dst = (vmem, hbm) if to_hbm else (hbm, vmem)
        copies.append(pltpu.make_async_copy(src, dst, sem.at[slot, k]))
    return copies


def _fetch_sets(x_hbm, xbuf, sem, step, n_steps, sets):
    slot = step % 2

    @pl.when(step == 0)
    def _():
        for c in _set_copies(x_hbm, xbuf, sem, step, slot, sets):
            c.start()

    @pl.when(step + 1 < n_steps)
    def _():
        for c in _set_copies(x_hbm, xbuf, sem, step + 1, 1 - slot, sets):
            c.start()

    for c in _set_copies(x_hbm, xbuf, sem, step, slot, sets):
        c.wait()
    return jnp.concatenate([xbuf[slot, k] for k in range(sets)], axis=0)


def _set_buffers(n_chunks, sets):
    return [pltpu.VMEM((2, sets, n_chunks, D_MODEL), F32), pltpu.SemaphoreType.DMA((2, sets))]


def _in_proj_kernel(x_hbm, g_ref, wut_ref, wv_ref, wc_ref, u_ref, cv_ref, xbuf, xsem, *, n_chunks):
    step = pl.program_id(0) * pl.num_programs(1) + pl.program_id(1)
    n_steps = pl.num_programs(0) * pl.num_programs(1)
    x = _fetch_sets(x_hbm, xbuf, xsem, step, n_steps, IN_SETS_PER_STEP)
    xn = _rmsnorm(x, g_ref[...]).astype(BF16)
    ut = lax.dot_general(wut_ref[...], xn, NT_DIMS, preferred_element_type=F32)
    cv = (jnp.dot(xn, wv_ref[...], preferred_element_type=F32)
          * jnp.dot(xn, wc_ref[...], preferred_element_type=F32)).astype(BF16)
    for k in range(IN_SETS_PER_STEP):
        u_ref[k] = ut[:, k * n_chunks:(k + 1) * n_chunks].astype(BF16)
        cv_ref[:, k * D_MODEL:(k + 1) * D_MODEL] = cv[k * n_chunks:(k + 1) * n_chunks]


def _in_proj_t(x4, norm_g, wut, w_in, *, bsz, n_chunks):
    s = IN_SETS_PER_STEP
    tok_spec = pl.BlockSpec((None, n_chunks, s * D_MODEL), lambda b, tp: (b, 0, tp))
    return pl.pallas_call(
        functools.partial(_in_proj_kernel, n_chunks=n_chunks),
        grid=(bsz, CHUNK // s),
        in_specs=[pl.BlockSpec(memory_space=pl.ANY), _const_spec((1, D_MODEL)),
                  _const_spec((D_MODEL, D_MODEL)), _col_spec(COL_V), _col_spec(COL_C)],
        out_specs=[pl.BlockSpec((s, D_MODEL, n_chunks), lambda b, tp: (tp, 0, b)), tok_spec],
        out_shape=[jax.ShapeDtypeStruct((CHUNK, D_MODEL, bsz * n_chunks), BF16),
                   jax.ShapeDtypeStruct((bsz, n_chunks, CHUNK * D_MODEL), BF16)],
        scratch_shapes=_set_buffers(n_chunks, s),
        compiler_params=_params(2),
        name="in_proj_t",
    )(x4, norm_g, wut, w_in, w_in)


def _s5_kernel(u_ref, mt_ref, wst_ref, woutt_ref, aq_ref, y_ref, sloc_ref, pf_ref, pb_ref,
               *, n_chunks, batch):
    n = batch * n_chunks
    groups = range(GROUPS_PER_STEP)
    uts = [u_ref[:, j * SSM_GROUP:(j + 1) * SSM_GROUP, :].reshape(CW, n) for j in groups]

    for j in groups:
        wst = wst_ref[j]
        for b in range(batch):
            sl = lax.dot_general(uts[j][:, b * n_chunks:(b + 1) * n_chunks], wst, TN_DIMS,
                                 preferred_element_type=F32)
            sloc_ref[j, 0, pl.ds(b, n_chunks, stride=batch), :] = sl[:, :HALF]
            sloc_ref[j, 1, pl.ds(b, n_chunks, stride=batch), :] = sl[:, HALF:]

    is_fwd = lax.broadcasted_iota(jnp.int32, (batch, HALF), 1) < STATE
    ar = [jnp.broadcast_to(aq_ref[j, 0:1, :], (batch, HALF)) for j in groups]
    ai = [jnp.broadcast_to(aq_ref[j, 1:2, :], (batch, HALF)) for j in groups]

    zero = jnp.zeros((batch, HALF), F32)
    last = n_chunks * batch
    for j in groups:
        for h in range(2):
            pf_ref[j, h, 0:batch, :] = zero
            pb_ref[j, h, last:last + batch, :] = zero

    def scan_step(k, carry):
        rf = k * batch
        rb = (n_chunks - 1 - k) * batch
        ro = (k + 1) * batch
        new = []
        for j in groups:
            sr, si = carry[j]
            xr = jnp.where(is_fwd, sloc_ref[j, 0, pl.ds(rf, batch), :],
                           sloc_ref[j, 0, pl.ds(rb, batch), :])
            xi = jnp.where(is_fwd, sloc_ref[j, 1, pl.ds(rf, batch), :],
                           sloc_ref[j, 1, pl.ds(rb, batch), :])
            nr = ar[j] * sr - ai[j] * si + xr
            ni = ar[j] * si + ai[j] * sr + xi
            pf_ref[j, 0, pl.ds(ro, batch), :] = nr
            pf_ref[j, 1, pl.ds(ro, batch), :] = ni
            pb_ref[j, 0, pl.ds(rb, batch), :] = nr
            pb_ref[j, 1, pl.ds(rb, batch), :] = ni
            new.append((nr, ni))
        return tuple(new)

    carry = tuple((zero, zero) for _ in groups)
    for k in range(n_chunks):
        carry = scan_step(k, carry)

    fwd_rows = lax.broadcasted_iota(jnp.int32, (n_chunks, HALF), 1) < STATE
    for j in groups:
        mt = mt_ref[j]
        woutt = woutt_ref[j]
        for b in range(batch):
            cols = slice(b * n_chunks, (b + 1) * n_chunks)
            prev = jnp.concatenate(
                [jnp.where(fwd_rows, pf_ref[j, h, pl.ds(b, n_chunks, stride=batch), :],
                           pb_ref[j, h, pl.ds(batch + b, n_chunks, stride=batch), :])
                 for h in range(2)], axis=1)
            yb = jnp.dot(mt, uts[j][:, cols], preferred_element_type=F32)
            yb = yb + lax.dot_general(woutt, prev.astype(BF16), NT_DIMS, preferred_element_type=F32)
            y_ref[:, j * SSM_GROUP:(j + 1) * SSM_GROUP, cols] = yb.reshape(CHUNK, SSM_GROUP, n_chunks)


def _s5_t(ut, mt, wst, woutt, aq, *, n_chunks, batch):
    n = batch * n_chunks
    gps = GROUPS_PER_STEP
    kern = functools.partial(_s5_kernel, n_chunks=n_chunks, batch=batch)
    op_spec = pl.BlockSpec((gps, CW, CW), lambda g: (g, 0, 0))
    act_spec = pl.BlockSpec((CHUNK, gps * SSM_GROUP, n), lambda g: (0, g, 0))
    state_rows = pltpu.VMEM((gps, 2, n + batch, HALF), F32)
    return pl.pallas_call(
        kern,
        grid=(N_GROUPS // gps,),
        in_specs=[act_spec, op_spec, op_spec, op_spec,
                  pl.BlockSpec((gps, 2, HALF), lambda g: (g, 0, 0))],
        out_specs=act_spec,
        out_shape=jax.ShapeDtypeStruct((CHUNK, D_MODEL, n), F32),
        scratch_shapes=[pltpu.VMEM((gps, 2, n, HALF), F32), state_rows, state_rows],
        compiler_params=_params(1),
        name="s5_t",
    )(ut, mt, wst, woutt, aq)


def _s5_ops_kernel(*refs):
    for g in range(OPS_GROUPS_PER_STEP):
        _s5_ops_group(g, *refs)


def _s5_ops_group(g, lam_ref, c_ref, bt_ref, dd_ref, mt_ref, wst_ref, woutt_ref, aq_ref,
                  pwr_ref, pwi_ref):
    lr, li = lam_ref[g, 0:1, :], lam_ref[g, 1:2, :]
    dt = jnp.exp(lam_ref[g, 2:3, :])
    c_re, c_im = c_ref[g, 0], c_ref[g, 1]
    bt_re, bt_im = bt_ref[g, 0], bt_ref[g, 1]

    n_rows = 2 * CHUNK + SUBLANES
    r = lax.broadcasted_iota(jnp.int32, (n_rows, HALF), 0)
    fwd = lax.broadcasted_iota(jnp.int32, (n_rows, HALF), 1) < STATE
    s = r - CHUNK
    e = jnp.where(r < CHUNK, jnp.where(fwd, r + 1, CHUNK - r),
                  jnp.where(r < 2 * CHUNK, jnp.where(fwd, CHUNK - 1 - s, s),
                            jnp.where(r == 2 * CHUNK, CHUNK, 1))).astype(F32)
    mag = jnp.exp(e * (lr * dt))
    ang = e * (li * dt)
    pwr_ref[g] = mag * jnp.cos(ang)
    pwi_ref[g] = mag * jnp.sin(ang)

    def power(row):
        return pwr_ref[g, row:row + 1, :], pwi_ref[g, row:row + 1, :]

    a_re, a_im = power(2 * CHUNK + 1)
    den = lr * lr + li * li
    nr, ni = a_re - 1.0, a_im
    coef_re = (nr * lr + ni * li) / den
    coef_im = (ni * lr - nr * li) / den
    bb_re = coef_re * bt_re - coef_im * bt_im
    bb_im = coef_re * bt_im + coef_im * bt_re

    st_re, st_im = [], []
    for k in range(CHUNK):
        p_re, p_im = power(CHUNK + k)
        st_re.append(p_re * bb_re - p_im * bb_im)
        st_im.append(p_re * bb_im + p_im * bb_re)
    wst = jnp.concatenate([jnp.concatenate(st_re, axis=0), jnp.concatenate(st_im, axis=0)], axis=1)
    wst_ref[g] = wst.astype(BF16)

    out_rows = []
    for k in range(CHUNK):
        p_re, p_im = power(k)
        out_rows.append(jnp.concatenate([c_re * p_re - c_im * p_im, -(c_re * p_im + c_im * p_re)],
                                        axis=1))
    woutt_ref[g] = jnp.concatenate(out_rows, axis=0).astype(BF16)

    aq_ref[g, 0:1, :], aq_ref[g, 1:2, :] = power(2 * CHUNK)

    fwd1 = lax.broadcasted_iota(jnp.int32, (SSM_GROUP, HALF), 1) < STATE
    zero = jnp.zeros_like(c_re)
    cmat = jnp.concatenate([
        jnp.concatenate([jnp.where(fwd1, c_re, zero), -jnp.where(fwd1, c_im, zero)], axis=1),
        jnp.concatenate([jnp.where(fwd1, zero, c_re), -jnp.where(fwd1, zero, c_im)], axis=1)], axis=0)
    kslab = lax.dot_general(cmat, wst, NT_DIMS, precision=lax.Precision.HIGHEST,
                            preferred_element_type=F32)
    k_f = kslab[:SSM_GROUP]
    k_b = kslab[SSM_GROUP:] + dd_ref[g]

    lane = lax.broadcasted_iota(jnp.int32, (SSM_GROUP, CW), 1)
    m_rows = []
    for t in range(CHUNK):
        left = SSM_GROUP * (CHUNK - 1 - t)
        right = SSM_GROUP * t
        a = k_f if left == 0 else jnp.where(lane < CW - left, pltpu.roll(k_f, CW - left, 1), 0.0)
        b = k_b if right == 0 else jnp.where(lane >= right, pltpu.roll(k_b, right, 1), 0.0)
        m_rows.append(a + b)
    mt_ref[g] = jnp.concatenate(m_rows, axis=0).astype(BF16)


def _s5_operators(lam_re, lam_im, log_dt, b_re, b_im, c_re, c_im, ssm_d):
    def lanes(x):
        return jnp.concatenate([x[0], x[1]], axis=-1)

    lam = jnp.stack([lanes(lam_re), lanes(lam_im),
                     lanes(jnp.broadcast_to(log_dt[..., None], lam_re.shape))], axis=1)
    c = jnp.stack([lanes(c_re), lanes(c_im)], axis=1)
    bt = jnp.stack([lanes(b_re.transpose(0, 1, 3, 2)), lanes(b_im.transpose(0, 1, 3, 2))], axis=1)
    dd = ssm_d.reshape(N_GROUPS, SSM_GROUP)[:, :, None] * jnp.eye(SSM_GROUP, CW, dtype=F32)

    gps = OPS_GROUPS_PER_STEP
    op_spec = pl.BlockSpec((gps, CW, CW), lambda g: (g, 0, 0))
    par_spec = pl.BlockSpec((gps, 2, SSM_GROUP, HALF), lambda g: (g, 0, 0, 0))
    op_shape = jax.ShapeDtypeStruct((N_GROUPS, CW, CW), BF16)
    return pl.pallas_call(
        _s5_ops_kernel,
        grid=(N_GROUPS // gps,),
        in_specs=[pl.BlockSpec((gps, 3, HALF), lambda g: (g, 0, 0)), par_spec, par_spec,
                  pl.BlockSpec((gps, SSM_GROUP, CW), lambda g: (g, 0, 0))],
        out_specs=[op_spec, op_spec, op_spec, pl.BlockSpec((gps, 2, HALF), lambda g: (g, 0, 0))],
        out_shape=[op_shape, op_shape, op_shape, jax.ShapeDtypeStruct((N_GROUPS, 2, HALF), F32)],
        scratch_shapes=[pltpu.VMEM((gps, 2 * CHUNK + SUBLANES, HALF), F32)] * 2,
        compiler_params=_params(1),
        name="s5_ops",
    )(lam, c, bt, dd)


def _block_kernel(x_hbm, yt_ref, cv_ref, cvprev_ref, cvnext_ref, ng_ref, pza_ref, pb_ref, pzb_ref,
                  pga_ref, pgb_ref, wglu_ref, wa_ref, wb_ref, wo_ref, cw_ref, cb_ref, fg_ref,
                  o_hbm, xn_ref, xbuf, xsem, obuf, osem, *, n_chunks):
    tp = pl.program_id(1)
    step = pl.program_id(0) * pl.num_programs(1) + tp
    n_steps = pl.num_programs(0) * pl.num_programs(1)
    x = _fetch_sets(x_hbm, xbuf, xsem, step, n_steps, SETS_PER_STEP)
    xn_ref[...] = _rmsnorm(x, ng_ref[...]).astype(BF16)

    def proj(w_ref):
        return jnp.dot(xn_ref[...], w_ref[...], preferred_element_type=F32)

    row = lax.broadcasted_iota(jnp.int32, (n_chunks, D_MODEL), 0)
    cvprev = cvprev_ref[...].astype(F32)
    cvnext = cvnext_ref[...].astype(F32)
    cvprev = jnp.where(tp == 0, jnp.where(row == 0, 0.0, pltpu.roll(cvprev, 1, 0)), cvprev)
    cvnext = jnp.where(tp == CHUNK // SETS_PER_STEP - 1,
                       jnp.where(row == n_chunks - 1, 0.0, pltpu.roll(cvnext, n_chunks - 1, 0)),
                       cvnext)
    cvs = [cv_ref[:, k * D_MODEL:(k + 1) * D_MODEL].astype(F32) for k in range(SETS_PER_STEP)]
    cvm = jnp.concatenate([cvprev] + cvs[:-1], axis=0)
    cvp = jnp.concatenate(cvs[1:] + [cvnext], axis=0)
    cw = cw_ref[...]
    conv = cb_ref[...] + cvm * cw[0:1] + jnp.concatenate(cvs, axis=0) * cw[1:2] + cvp * cw[2:3]

    z_b = proj(pzb_ref)
    y_b = (proj(pb_ref) * conv * (z_b * _sigmoid(z_b))).astype(BF16)
    merged = _sigmoid(proj(pgb_ref)) * jnp.dot(y_b, wb_ref[...], preferred_element_type=F32)

    yg = _gelu_tanh(jnp.concatenate([yt_ref[k].T for k in range(SETS_PER_STEP)], axis=0))
    yg = yg * _sigmoid(jnp.dot(yg.astype(BF16), wglu_ref[...], preferred_element_type=F32))
    z_a = proj(pza_ref)
    y_a = (yg * (z_a * _sigmoid(z_a))).astype(BF16)
    merged = merged + _sigmoid(proj(pga_ref)) * jnp.dot(y_a, wa_ref[...], preferred_element_type=F32)

    h = x + jnp.dot(merged.astype(BF16), wo_ref[...], preferred_element_type=F32)
    out = _rmsnorm(h, fg_ref[...])

    slot = step % 2

    @pl.when(step >= 2)
    def _():
        for c in _set_copies(o_hbm, obuf, osem, step - 2, slot, SETS_PER_STEP, to_hbm=True):
            c.wait()

    for k in range(SETS_PER_STEP):
        obuf[slot, k] = out[k * n_chunks:(k + 1) * n_chunks]
    for c in _set_copies(o_hbm, obuf, osem, step, slot, SETS_PER_STEP, to_hbm=True):
        c.start()

    @pl.when(step == n_steps - 1)
    def _():
        @pl.when(step >= 1)
        def _():
            for c in _set_copies(o_hbm, obuf, osem, step - 1, 1 - slot, SETS_PER_STEP, to_hbm=True):
                c.wait()
        for c in _set_copies(o_hbm, obuf, osem, step, slot, SETS_PER_STEP, to_hbm=True):
            c.wait()


def _block_t(x4, yt, cv3, norm_g, w_in, w_glu, w_a, w_b, w_o, cw, cb, final_g, *, bsz, n_chunks):
    s = SETS_PER_STEP
    tok_spec = pl.BlockSpec((None, n_chunks, s * D_MODEL), lambda b, tp: (b, 0, tp))
    prev_spec = pl.BlockSpec((None, n_chunks, D_MODEL),
                             lambda b, tp: (b, 0, (tp * s + CHUNK - 1) % CHUNK))
    next_spec = pl.BlockSpec((None, n_chunks, D_MODEL), lambda b, tp: (b, 0, (tp * s + s) % CHUNK))
    kern = functools.partial(_block_kernel, n_chunks=n_chunks)
    return pl.pallas_call(
        kern,
        grid=(bsz, CHUNK // s),
        in_specs=[
            pl.BlockSpec(memory_space=pl.ANY),
            pl.BlockSpec((s, D_MODEL, n_chunks), lambda b, tp: (tp, 0, b)),
            tok_spec, prev_spec, next_spec,
            _const_spec((1, D_MODEL)),
            _col_spec(COL_ZA), _col_spec(COL_B), _col_spec(COL_ZB), _col_spec(COL_GA), _col_spec(COL_GB),
            _const_spec((D_MODEL, D_MODEL)),
            _const_spec((D_MODEL, D_MODEL)),
            _const_spec((D_MODEL, D_MODEL)),
            _const_spec((D_MODEL, D_MODEL)),
            _const_spec((3, D_MODEL)),
            _const_spec((1, D_MODEL)),
            _const_spec((1, D_MODEL)),
        ],
        out_specs=pl.BlockSpec(memory_space=pl.ANY),
        out_shape=jax.ShapeDtypeStruct(x4.shape, F32),
        scratch_shapes=([pltpu.VMEM((s * n_chunks, D_MODEL), BF16)]
                        + _set_buffers(n_chunks, s) + _set_buffers(n_chunks, s)),
        compiler_params=_params(2),
        name="block_t",
    )(x4, yt, cv3, cv3, cv3, norm_g, w_in, w_in, w_in, w_in, w_in, w_glu, w_a, w_b, w_o, cw, cb,
      final_g)


def kernel(x, norm_g, w_in, lam_re, lam_im, log_dt, ssm_b_re, ssm_b_im, ssm_c_re, ssm_c_im, ssm_d,
           w_glu, conv_w, conv_b, w_branch_a, w_branch_b, w_out, final_g):
    bsz, seq_len, d = x.shape
    assert norm_g.shape[0] == 1 and d == D_MODEL and bsz == SUBLANES
    n_chunks = seq_len // CHUNK
    assert seq_len == n_chunks * CHUNK and n_chunks % LANES == 0

    x4 = x.reshape(bsz, n_chunks, CHUNK, d)
    ng = norm_g[0].reshape(1, d)
    w_bf = w_in[0].astype(BF16)
    wut = w_bf[:, COL_U * d:(COL_U + 1) * d].T

    ut, cv3 = _in_proj_t(x4, ng, wut, w_bf, bsz=bsz, n_chunks=n_chunks)
    mt, wst, woutt, aq = _s5_operators(
        lam_re[0], lam_im[0], log_dt[0], ssm_b_re[0], ssm_b_im[0], ssm_c_re[0], ssm_c_im[0],
        ssm_d[0])
    yt = _s5_t(ut, mt, wst, woutt, aq, n_chunks=n_chunks, batch=bsz)

    out4 = _block_t(
        x4, yt, cv3, ng, w_bf, w_glu[0].astype(BF16), w_branch_a[0].astype(BF16),
        w_branch_b[0].astype(BF16), w_out[0].astype(BF16), conv_w[0], conv_b[0].reshape(1, d),
        final_g.reshape(1, d), bsz=bsz, n_chunks=n_chunks)
    return out4.reshape(bsz, seq_len, d)
```

```python
import functools
import math

import jax
import jax.numpy as jnp
from jax import lax
from jax.experimental import pallas as pl
from jax.experimental.pallas import tpu as pltpu

F32 = jnp.float32
BF16 = jnp.bfloat16

D_MODEL = 1024
SSM_GROUP = 16
N_GROUPS = D_MODEL // SSM_GROUP
STATE = 64
CHUNK = 16
CW = CHUNK * SSM_GROUP
CHUNK_BLOCK = 32
SW = 4 * STATE
HALF = 2 * STATE
LANES = 128
EPS = 1e-6
SUBLANES = 8
IN_SETS_PER_STEP = 4
SETS_PER_STEP = 2
GROUPS_PER_STEP = 4
OPS_GROUPS_PER_STEP = 8
VMEM_LIMIT_BYTES = 60 * 1024 * 1024

NT_DIMS = (((1,), (1,)), ((), ()))
TN_DIMS = (((0,), (0,)), ((), ()))


def _rmsnorm(x, g):
    ms = jnp.mean(x * x, axis=-1, keepdims=True)
    return x * lax.rsqrt(ms + EPS) * g


def _sigmoid(x):
    return 0.5 * jnp.tanh(0.5 * x) + 0.5


def _gelu_tanh(x):
    c = math.sqrt(2.0 / math.pi)
    return x * (0.5 * (1.0 + jnp.tanh(c * (x + 0.044715 * (x * x * x)))))


def _params(n_axes):
    return pltpu.CompilerParams(
        dimension_semantics=("arbitrary",) * n_axes, vmem_limit_bytes=VMEM_LIMIT_BYTES)


def _const_spec(shape):
    return pl.BlockSpec(shape, lambda *_: (0,) * len(shape), pipeline_mode=pl.Buffered(1))


COL_U, COL_ZA, COL_V, COL_B, COL_C, COL_ZB, COL_GA, COL_GB = range(8)


def _col_spec(col):
    return pl.BlockSpec((D_MODEL, D_MODEL), lambda *_: (0, col), pipeline_mode=pl.Buffered(1))


def _set_copies(x_hbm, buf, sem, step, slot, sets, *, to_hbm=False):
    steps_per_block = CHUNK // sets
    c0 = (step // steps_per_block) * CHUNK_BLOCK
    tl0 = (step % steps_per_block) * sets
    copies = []
    for k in range(sets):
        for b in range(x_hbm.shape[0]):
            hbm = x_hbm.at[b, pl.ds(c0, CHUNK_BLOCK), tl0 + k, :]
            vmem = buf.at[slot, k, :, b, :]
            src, dst = (vmem, hbm) if to_hbm else (hbm, vmem)
            copies.append(pltpu.make_async_copy(src, dst, sem.at[slot, k]))
    return copies


def _fetch_sets(x_hbm, xbuf, sem, step, n_steps, sets):
    slot = step % 2

    @pl.when(step == 0)
    def _():
        for c in _set_copies(x_hbm, xbuf, sem, step, slot, sets):
            c.start()

    @pl.when(step + 1 < n_steps)
    def _():
        for c in _set_copies(x_hbm, xbuf, sem, step + 1, 1 - slot, sets):
            c.start()

    for c in _set_copies(x_hbm, xbuf, sem, step, slot, sets):
        c.wait()
    rows = xbuf.shape[2] * xbuf.shape[3]
    return jnp.concatenate([xbuf[slot, k].reshape(rows, D_MODEL) for k in range(sets)], axis=0)


def _set_buffers(batch, sets):
    return [pltpu.VMEM((2, sets, CHUNK_BLOCK, batch, D_MODEL), F32),
            pltpu.SemaphoreType.DMA((2, sets))]


def _in_proj_kernel(x_hbm, g_ref, wut_ref, wv_ref, wc_ref, u_ref, cv_ref, xbuf, xsem, *, set_rows):
    step = pl.program_id(0) * pl.num_programs(1) + pl.program_id(1)
    n_steps = pl.num_programs(0) * pl.num_programs(1)
    x = _fetch_sets(x_hbm, xbuf, xsem, step, n_steps, IN_SETS_PER_STEP)
    xn = _rmsnorm(x, g_ref[...]).astype(BF16)
    ut = lax.dot_general(wut_ref[...], xn, NT_DIMS, preferred_element_type=F32)
    cv = (jnp.dot(xn, wv_ref[...], preferred_element_type=F32)
          * jnp.dot(xn, wc_ref[...], preferred_element_type=F32)).astype(BF16)
    for k in range(IN_SETS_PER_STEP):
        u_ref[k] = ut[:, k * set_rows:(k + 1) * set_rows].astype(BF16)
        cv_ref[:, k * D_MODEL:(k + 1) * D_MODEL] = cv[k * set_rows:(k + 1) * set_rows]


def _in_proj_t(x4, norm_g, wut, w_in, *, bsz, n_chunks):
    s = IN_SETS_PER_STEP
    n_blocks = n_chunks // CHUNK_BLOCK
    set_rows = CHUNK_BLOCK * bsz
    tok_spec = pl.BlockSpec((None, set_rows, s * D_MODEL), lambda cb, tp: (cb, 0, tp))
    return pl.pallas_call(
        functools.partial(_in_proj_kernel, set_rows=set_rows),
        grid=(n_blocks, CHUNK // s),
        in_specs=[pl.BlockSpec(memory_space=pl.ANY), _const_spec((1, D_MODEL)),
                  _const_spec((D_MODEL, D_MODEL)), _col_spec(COL_V), _col_spec(COL_C)],
        out_specs=[pl.BlockSpec((s, D_MODEL, set_rows), lambda cb, tp: (tp, 0, cb)), tok_spec],
        out_shape=[jax.ShapeDtypeStruct((CHUNK, D_MODEL, bsz * n_chunks), BF16),
                   jax.ShapeDtypeStruct((n_blocks, set_rows, CHUNK * D_MODEL), BF16)],
        scratch_shapes=_set_buffers(bsz, s),
        compiler_params=_params(2),
        name="in_proj_t",
    )(x4, norm_g, wut, w_in, w_in)


def _s5_kernel(u_ref, mt_ref, wst_ref, woutt_ref, aq_ref, y_ref, sloc_ref, pf_ref, pb_ref,
               *, n_chunks, batch):
    n = batch * n_chunks
    groups = range(GROUPS_PER_STEP)
    uts = [u_ref[:, j * SSM_GROUP:(j + 1) * SSM_GROUP, :].reshape(CW, n) for j in groups]

    for j in groups:
        sl = lax.dot_general(uts[j], wst_ref[j], TN_DIMS, preferred_element_type=F32)
        sloc_ref[j, 0] = sl[:, :HALF]
        sloc_ref[j, 1] = sl[:, HALF:]

    is_fwd = lax.broadcasted_iota(jnp.int32, (batch, HALF), 1) < STATE
    ar = [jnp.broadcast_to(aq_ref[j, 0:1, :], (batch, HALF)) for j in groups]
    ai = [jnp.broadcast_to(aq_ref[j, 1:2, :], (batch, HALF)) for j in groups]

    zero = jnp.zeros((batch, HALF), F32)
    last = n_chunks * batch
    for j in groups:
        for h in range(2):
            pf_ref[j, h, 0:batch, :] = zero
            pb_ref[j, h, last:last + batch, :] = zero

    def scan_step(k, carry):
        rf = k * batch
        rb = (n_chunks - 1 - k) * batch
        ro = (k + 1) * batch
        new = []
        for j in groups:
            sr, si = carry[j]
            xr = jnp.where(is_fwd, sloc_ref[j, 0, pl.ds(rf, batch), :],
                           sloc_ref[j, 0, pl.ds(rb, batch), :])
            xi = jnp.where(is_fwd, sloc_ref[j, 1, pl.ds(rf, batch), :],
                           sloc_ref[j, 1, pl.ds(rb, batch), :])
            nr = ar[j] * sr - ai[j] * si + xr
            ni = ar[j] * si + ai[j] * sr + xi
            pf_ref[j, 0, pl.ds(ro, batch), :] = nr
            pf_ref[j, 1, pl.ds(ro, batch), :] = ni
            pb_ref[j, 0, pl.ds(rb, batch), :] = nr
            pb_ref[j, 1, pl.ds(rb, batch), :] = ni
            new.append((nr, ni))
        return tuple(new)

    carry = tuple((zero, zero) for _ in groups)
    for k in range(n_chunks):
        carry = scan_step(k, carry)

    fwd_rows = lax.broadcasted_iota(jnp.int32, (n, HALF), 1) < STATE
    for j in groups:
        prev = jnp.concatenate(
            [jnp.where(fwd_rows, pf_ref[j, h, 0:n, :], pb_ref[j, h, batch:batch + n, :])
             for h in range(2)], axis=1)
        y = jnp.dot(mt_ref[j], uts[j], preferred_element_type=F32)
        y = y + lax.dot_general(woutt_ref[j], prev.astype(BF16), NT_DIMS, preferred_element_type=F32)
        y_ref[:, j * SSM_GROUP:(j + 1) * SSM_GROUP, :] = y.reshape(CHUNK, SSM_GROUP, n)


def _s5_t(ut, mt, wst, woutt, aq, *, n_chunks, batch):
    n = batch * n_chunks
    gps = GROUPS_PER_STEP
    kern = functools.partial(_s5_kernel, n_chunks=n_chunks, batch=batch)
    op_spec = pl.BlockSpec((gps, CW, CW), lambda g: (g, 0, 0))
    act_spec = pl.BlockSpec((CHUNK, gps * SSM_GROUP, n), lambda g: (0, g, 0))
    state_rows = pltpu.VMEM((gps, 2, n + batch, HALF), F32)
    return pl.pallas_call(
        kern,
        grid=(N_GROUPS // gps,),
        in_specs=[act_spec, op_spec, op_spec, op_spec,
                  pl.BlockSpec((gps, 2, HALF), lambda g: (g, 0, 0))],
        out_specs=act_spec,
        out_shape=jax.ShapeDtypeStruct((CHUNK, D_MODEL, n), F32),
        scratch_shapes=[pltpu.VMEM((gps, 2, n, HALF), F32), state_rows, state_rows],
        compiler_params=_params(1),
        name="s5_t",
    )(ut, mt, wst, woutt, aq)


def _s5_ops_kernel(*refs):
    for g in range(OPS_GROUPS_PER_STEP):
        _s5_ops_group(g, *refs)


def _s5_ops_group(g, lam_ref, c_ref, bt_ref, dd_ref, mt_ref, wst_ref, woutt_ref, aq_ref,
                  pwr_ref, pwi_ref):
    lr, li = lam_ref[g, 0:1, :], lam_ref[g, 1:2, :]
    dt = jnp.exp(lam_ref[g, 2:3, :])
    c_re, c_im = c_ref[g, 0], c_ref[g, 1]
    bt_re, bt_im = bt_ref[g, 0], bt_ref[g, 1]

    n_rows = 2 * CHUNK + SUBLANES
    r = lax.broadcasted_iota(jnp.int32, (n_rows, HALF), 0)
    fwd = lax.broadcasted_iota(jnp.int32, (n_rows, HALF), 1) < STATE
    s = r - CHUNK
    e = jnp.where(r < CHUNK, jnp.where(fwd, r + 1, CHUNK - r),
                  jnp.where(r < 2 * CHUNK, jnp.where(fwd, CHUNK - 1 - s, s),
                            jnp.where(r == 2 * CHUNK, CHUNK, 1))).astype(F32)
    mag = jnp.exp(e * (lr * dt))
    ang = e * (li * dt)
    pwr_ref[g] = mag * jnp.cos(ang)
    pwi_ref[g] = mag * jnp.sin(ang)

    def power(row):
        return pwr_ref[g, row:row + 1, :], pwi_ref[g, row:row + 1, :]

    a_re, a_im = power(2 * CHUNK + 1)
    den = lr * lr + li * li
    nr, ni = a_re - 1.0, a_im
    coef_re = (nr * lr + ni * li) / den
    coef_im = (ni * lr - nr * li) / den
    bb_re = coef_re * bt_re - coef_im * bt_im
    bb_im = coef_re * bt_im + coef_im * bt_re

    st_re, st_im = [], []
    for k in range(CHUNK):
        p_re, p_im = power(CHUNK + k)
        st_re.append(p_re * bb_re - p_im * bb_im)
        st_im.append(p_re * bb_im + p_im * bb_re)
    wst = jnp.concatenate([jnp.concatenate(st_re, axis=0), jnp.concatenate(st_im, axis=0)], axis=1)
    wst_ref[g] = wst.astype(BF16)

    out_rows = []
    for k in range(CHUNK):
        p_re, p_im = power(k)
        out_rows.append(jnp.concatenate([c_re * p_re - c_im * p_im, -(c_re * p_im + c_im * p_re)],
                                        axis=1))
    woutt_ref[g] = jnp.concatenate(out_rows, axis=0).astype(BF16)

    aq_ref[g, 0:1, :], aq_ref[g, 1:2, :] = power(2 * CHUNK)

    fwd1 = lax.broadcasted_iota(jnp.int32, (SSM_GROUP, HALF), 1) < STATE
    zero = jnp.zeros_like(c_re)
    cmat = jnp.concatenate([
        jnp.concatenate([jnp.where(fwd1, c_re, zero), -jnp.where(fwd1, c_im, zero)], axis=1),
        jnp.concatenate([jnp.where(fwd1, zero, c_re), -jnp.where(fwd1, zero, c_im)], axis=1)], axis=0)
    kslab = lax.dot_general(cmat, wst, NT_DIMS, precision=lax.Precision.HIGHEST,
                            preferred_element_type=F32)
    k_f = kslab[:SSM_GROUP]
    k_b = kslab[SSM_GROUP:] + dd_ref[g]

    lane = lax.broadcasted_iota(jnp.int32, (SSM_GROUP, CW), 1)
    m_rows = []
    for t in range(CHUNK):
        left = SSM_GROUP * (CHUNK - 1 - t)
        right = SSM_GROUP * t
        a = k_f if left == 0 else jnp.where(lane < CW - left, pltpu.roll(k_f, CW - left, 1), 0.0)
        b = k_b if right == 0 else jnp.where(lane >= right, pltpu.roll(k_b, right, 1), 0.0)
        m_rows.append(a + b)
    mt_ref[g] = jnp.concatenate(m_rows, axis=0).astype(BF16)


def _s5_operators(lam_re, lam_im, log_dt, b_re, b_im, c_re, c_im, ssm_d):
    def lanes(x):
        return jnp.concatenate([x[0], x[1]], axis=-1)

    lam = jnp.stack([lanes(lam_re), lanes(lam_im),
                     lanes(jnp.broadcast_to(log_dt[..., None], lam_re.shape))], axis=1)
    c = jnp.stack([lanes(c_re), lanes(c_im)], axis=1)
    bt = jnp.stack([lanes(b_re.transpose(0, 1, 3, 2)), lanes(b_im.transpose(0, 1, 3, 2))], axis=1)
    dd = ssm_d.reshape(N_GROUPS, SSM_GROUP)[:, :, None] * jnp.eye(SSM_GROUP, CW, dtype=F32)

    gps = OPS_GROUPS_PER_STEP
    op_spec = pl.BlockSpec((gps, CW, CW), lambda g: (g, 0, 0))
    par_spec = pl.BlockSpec((gps, 2, SSM_GROUP, HALF), lambda g: (g, 0, 0, 0))
    op_shape = jax.ShapeDtypeStruct((N_GROUPS, CW, CW), BF16)
    return pl.pallas_call(
        _s5_ops_kernel,
        grid=(N_GROUPS // gps,),
        in_specs=[pl.BlockSpec((gps, 3, HALF), lambda g: (g, 0, 0)), par_spec, par_spec,
                  pl.BlockSpec((gps, SSM_GROUP, CW), lambda g: (g, 0, 0))],
        out_specs=[op_spec, op_spec, op_spec, pl.BlockSpec((gps, 2, HALF), lambda g: (g, 0, 0))],
        out_shape=[op_shape, op_shape, op_shape, jax.ShapeDtypeStruct((N_GROUPS, 2, HALF), F32)],
        scratch_shapes=[pltpu.VMEM((gps, 2 * CHUNK + SUBLANES, HALF), F32)] * 2,
        compiler_params=_params(1),
        name="s5_ops",
    )(lam, c, bt, dd)


def _block_kernel(x_hbm, yt_ref, cv_ref, cvprev_ref, cvnext_ref, haloprev_ref, halonext_ref, ng_ref,
                  pza_ref, pb_ref, pzb_ref, pga_ref, pgb_ref, wglu_ref, wa_ref, wb_ref, wo_ref,
                  cw_ref, cb_ref, fg_ref, o_hbm, xn_ref, xbuf, xsem, obuf, osem, *, set_rows, batch):
    blk = pl.program_id(0)
    tp = pl.program_id(1)
    step = pl.program_id(0) * pl.num_programs(1) + tp
    n_steps = pl.num_programs(0) * pl.num_programs(1)
    x = _fetch_sets(x_hbm, xbuf, xsem, step, n_steps, SETS_PER_STEP)
    xn_ref[...] = _rmsnorm(x, ng_ref[...]).astype(BF16)

    def proj(w_ref):
        return jnp.dot(xn_ref[...], w_ref[...], preferred_element_type=F32)

    cvprev = cvprev_ref[...].astype(F32)
    cvnext = cvnext_ref[...].astype(F32)
    halo_prev = jnp.where(blk == 0, 0.0, haloprev_ref[...].astype(F32)[batch:])
    halo_next = jnp.where(blk == pl.num_programs(0) - 1, 0.0, halonext_ref[...].astype(F32)[:batch])
    cvprev = jnp.where(tp == 0,
                       jnp.concatenate([halo_prev, cvprev[:set_rows - batch]], axis=0), cvprev)
    cvnext = jnp.where(tp == CHUNK // SETS_PER_STEP - 1,
                       jnp.concatenate([cvnext[batch:], halo_next], axis=0), cvnext)
    cvs = [cv_ref[:, k * D_MODEL:(k + 1) * D_MODEL].astype(F32) for k in range(SETS_PER_STEP)]
    cvm = jnp.concatenate([cvprev] + cvs[:-1], axis=0)
    cvp = jnp.concatenate(cvs[1:] + [cvnext], axis=0)
    cw = cw_ref[...]
    conv = cb_ref[...] + cvm * cw[0:1] + jnp.concatenate(cvs, axis=0) * cw[1:2] + cvp * cw[2:3]

    z_b = proj(pzb_ref)
    y_b = (proj(pb_ref) * conv * (z_b * _sigmoid(z_b))).astype(BF16)
    merged = _sigmoid(proj(pgb_ref)) * jnp.dot(y_b, wb_ref[...], preferred_element_type=F32)

    yg = _gelu_tanh(jnp.concatenate([yt_ref[k].T for k in range(SETS_PER_STEP)], axis=0))
    yg = yg * _sigmoid(jnp.dot(yg.astype(BF16), wglu_ref[...], preferred_element_type=F32))
    z_a = proj(pza_ref)
    y_a = (yg * (z_a * _sigmoid(z_a))).astype(BF16)
    merged = merged + _sigmoid(proj(pga_ref)) * jnp.dot(y_a, wa_ref[...], preferred_element_type=F32)

    h = x + jnp.dot(merged.astype(BF16), wo_ref[...], preferred_element_type=F32)
    out = _rmsnorm(h, fg_ref[...])

    slot = step % 2

    @pl.when(step >= 2)
    def _():
        for c in _set_copies(o_hbm, obuf, osem, step - 2, slot, SETS_PER_STEP, to_hbm=True):
            c.wait()

    for k in range(SETS_PER_STEP):
        obuf[slot, k] = out[k * set_rows:(k + 1) * set_rows].reshape(CHUNK_BLOCK, batch, D_MODEL)
    for c in _set_copies(o_hbm, obuf, osem, step, slot, SETS_PER_STEP, to_hbm=True):
        c.start()

    @pl.when(step == n_steps - 1)
    def _():
        @pl.when(step >= 1)
        def _():
            for c in _set_copies(o_hbm, obuf, osem, step - 1, 1 - slot, SETS_PER_STEP, to_hbm=True):
                c.wait()
        for c in _set_copies(o_hbm, obuf, osem, step, slot, SETS_PER_STEP, to_hbm=True):
            c.wait()


def _block_t(x4, yt, cv3, norm_g, w_in, w_glu, w_a, w_b, w_o, cw, cb, final_g, *, bsz, n_chunks):
    s = SETS_PER_STEP
    n_blocks = n_chunks // CHUNK_BLOCK
    set_rows = CHUNK_BLOCK * bsz
    halo = 2 * bsz
    tok_spec = pl.BlockSpec((None, set_rows, s * D_MODEL), lambda cb, tp: (cb, 0, tp))
    prev_spec = pl.BlockSpec((None, set_rows, D_MODEL),
                             lambda cb, tp: (cb, 0, (tp * s + CHUNK - 1) % CHUNK))
    next_spec = pl.BlockSpec((None, set_rows, D_MODEL), lambda cb, tp: (cb, 0, (tp * s + s) % CHUNK))
    haloprev_spec = pl.BlockSpec((None, halo, D_MODEL),
                                 lambda cb, tp: (jnp.maximum(cb - 1, 0), set_rows // halo - 1, CHUNK - 1))
    halonext_spec = pl.BlockSpec((None, halo, D_MODEL),
                                 lambda cb, tp: (jnp.minimum(cb + 1, n_blocks - 1), 0, 0))
    kern = functools.partial(_block_kernel, set_rows=set_rows, batch=bsz)
    return pl.pallas_call(
        kern,
        grid=(n_blocks, CHUNK // s),
        in_specs=[
            pl.BlockSpec(memory_space=pl.ANY),
            pl.BlockSpec((s, D_MODEL, set_rows), lambda cb, tp: (tp, 0, cb)),
            tok_spec, prev_spec, next_spec, haloprev_spec, halonext_spec,
            _const_spec((1, D_MODEL)),
            _col_spec(COL_ZA), _col_spec(COL_B), _col_spec(COL_ZB), _col_spec(COL_GA), _col_spec(COL_GB),
            _const_spec((D_MODEL, D_MODEL)),
            _const_spec((D_MODEL, D_MODEL)),
            _const_spec((D_MODEL, D_MODEL)),
            _const_spec((D_MODEL, D_MODEL)),
            _const_spec((3, D_MODEL)),
            _const_spec((1, D_MODEL)),
            _const_spec((1, D_MODEL)),
        ],
        out_specs=pl.BlockSpec(memory_space=pl.ANY),
        out_shape=jax.ShapeDtypeStruct(x4.shape, F32),
        scratch_shapes=([pltpu.VMEM((s * set_rows, D_MODEL), BF16)]
                        + _set_buffers(bsz, s) + _set_buffers(bsz, s)),
        compiler_params=_params(2),
        name="block_t",
    )(x4, yt, cv3, cv3, cv3, cv3, cv3, norm_g, w_in, w_in, w_in, w_in, w_in, w_glu, w_a, w_b, w_o, cw, cb,
      final_g)


def kernel(x, norm_g, w_in, lam_re, lam_im, log_dt, ssm_b_re, ssm_b_im, ssm_c_re, ssm_c_im, ssm_d,
           w_glu, conv_w, conv_b, w_branch_a, w_branch_b, w_out, final_g):
    bsz, seq_len, d = x.shape
    assert norm_g.shape[0] == 1 and d == D_MODEL and bsz == SUBLANES
    n_chunks = seq_len // CHUNK
    assert seq_len == n_chunks * CHUNK and n_chunks % CHUNK_BLOCK == 0
    assert (CHUNK_BLOCK * bsz) % LANES == 0

    x4 = x.reshape(bsz, n_chunks, CHUNK, d)
    ng = norm_g[0].reshape(1, d)
    w_bf = w_in[0].astype(BF16)
    wut = w_bf[:, COL_U * d:(COL_U + 1) * d].T

    ut, cv3 = _in_proj_t(x4, ng, wut, w_bf, bsz=bsz, n_chunks=n_chunks)
    mt, wst, woutt, aq = _s5_operators(
        lam_re[0], lam_im[0], log_dt[0], ssm_b_re[0], ssm_b_im[0], ssm_c_re[0], ssm_c_im[0],
        ssm_d[0])
    yt = _s5_t(ut, mt, wst, woutt, aq, n_chunks=n_chunks, batch=bsz)

    out4 = _block_t(
        x4, yt, cv3, ng, w_bf, w_glu[0].astype(BF16), w_branch_a[0].astype(BF16),
        w_branch_b[0].astype(BF16), w_out[0].astype(BF16), conv_w[0], conv_b[0].reshape(1, d),
        final_g.reshape(1, d), bsz=bsz, n_chunks=n_chunks)
    return out4.reshape(bsz, seq_len, d)
```

```python
import functools
import math

import jax
import jax.numpy as jnp
from jax import lax
from jax.experimental import pallas as pl
from jax.experimental.pallas import tpu as pltpu

F32 = jnp.float32
BF16 = jnp.bfloat16

D_MODEL = 1024
SSM_GROUP = 16
N_GROUPS = D_MODEL // SSM_GROUP
STATE = 64
CHUNK = 16
CW = CHUNK * SSM_GROUP
CHUNK_BLOCK = 32
SW = 4 * STATE
HALF = 2 * STATE
LANES = 128
EPS = 1e-6
SUBLANES = 8
IN_SETS_PER_STEP = 4
SETS_PER_STEP = 2
GROUPS_PER_STEP = 4
OPS_GROUPS_PER_STEP = 8
VMEM_LIMIT_BYTES = 60 * 1024 * 1024

NT_DIMS = (((1,), (1,)), ((), ()))
TN_DIMS = (((0,), (0,)), ((), ()))


def _rmsnorm(x, g):
    ms = jnp.mean(x * x, axis=-1, keepdims=True)
    return x * lax.rsqrt(ms + EPS) * g


def _sigmoid(x):
    return 0.5 * jnp.tanh(0.5 * x) + 0.5


def _gelu_tanh(x):
    c = math.sqrt(2.0 / math.pi)
    return x * (0.5 * (1.0 + jnp.tanh(c * (x + 0.044715 * (x * x * x)))))


def _params(n_axes):
    return pltpu.CompilerParams(
        dimension_semantics=("arbitrary",) * n_axes, vmem_limit_bytes=VMEM_LIMIT_BYTES)


def _const_spec(shape):
    return pl.BlockSpec(shape, lambda *_: (0,) * len(shape), pipeline_mode=pl.Buffered(1))


COL_U, COL_ZA, COL_V, COL_B, COL_C, COL_ZB, COL_GA, COL_GB = range(8)


def _col_spec(col):
    return pl.BlockSpec((D_MODEL, D_MODEL), lambda *_: (0, col), pipeline_mode=pl.Buffered(1))


def _set_copies(x_hbm, buf, sem, step, slot, sets, *, to_hbm=False):
    steps_per_block = CHUNK // sets
    c0 = (step // steps_per_block) * CHUNK_BLOCK
    tl0 = (step % steps_per_block) * sets
    copies = []
    for k in range(sets):
        for b in range(x_hbm.shape[0]):
            hbm = x_hbm.at[b, pl.ds(c0, CHUNK_BLOCK), tl0 + k, :]
            vmem = buf.at[slot, k, :, b, :]
            src, dst = (vmem, hbm) if to_hbm else (hbm, vmem)
            copies.append(pltpu.make_async_copy(src, dst, sem.at[slot, k]))
    return copies


def _fetch_sets(x_hbm, xbuf, sem, step, n_steps, sets):
    slot = step % 2

    @pl.when(step == 0)
    def _():
        for c in _set_copies(x_hbm, xbuf, sem, step, slot, sets):
            c.start()

    @pl.when(step + 1 < n_steps)
    def _():
        for c in _set_copies(x_hbm, xbuf, sem, step + 1, 1 - slot, sets):
            c.start()

    for c in _set_copies(x_hbm, xbuf, sem, step, slot, sets):
        c.wait()
    rows = xbuf.shape[2] * xbuf.shape[3]
    return jnp.concatenate([xbuf[slot, k].reshape(rows, D_MODEL) for k in range(sets)], axis=0)


def _set_buffers(batch, sets):
    return [pltpu.VMEM((2, sets, CHUNK_BLOCK, batch, D_MODEL), F32),
            pltpu.SemaphoreType.DMA((2, sets))]


def _in_proj_kernel(x_hbm, g_ref, wut_ref, wv_ref, wc_ref, u_ref, cv_ref, xbuf, xsem, *, set_rows):
    step = pl.program_id(0) * pl.num_programs(1) + pl.program_id(1)
    n_steps = pl.num_programs(0) * pl.num_programs(1)
    x = _fetch_sets(x_hbm, xbuf, xsem, step, n_steps, IN_SETS_PER_STEP)
    xn = _rmsnorm(x, g_ref[...]).astype(BF16)
    ut = lax.dot_general(wut_ref[...], xn, NT_DIMS, preferred_element_type=F32)
    cv = (jnp.dot(xn, wv_ref[...], preferred_element_type=F32)
          * jnp.dot(xn, wc_ref[...], preferred_element_type=F32)).astype(BF16)
    for k in range(IN_SETS_PER_STEP):
        u_ref[k] = ut[:, k * set_rows:(k + 1) * set_rows].astype(BF16)
        cv_ref[:, k * D_MODEL:(k + 1) * D_MODEL] = cv[k * set_rows:(k + 1) * set_rows]


def _in_proj_t(x4, norm_g, wut, w_in, *, bsz, n_chunks):
    s = IN_SETS_PER_STEP
    n_blocks = n_chunks // CHUNK_BLOCK
    set_rows = CHUNK_BLOCK * bsz
    tok_spec = pl.BlockSpec((None, set_rows, s * D_MODEL), lambda cb, tp: (cb, 0, tp))
    return pl.pallas_call(
        functools.partial(_in_proj_kernel, set_rows=set_rows),
        grid=(n_blocks, CHUNK // s),
        in_specs=[pl.BlockSpec(memory_space=pl.ANY), _const_spec((1, D_MODEL)),
                  _const_spec((D_MODEL, D_MODEL)), _col_spec(COL_V), _col_spec(COL_C)],
        out_specs=[pl.BlockSpec((s, D_MODEL, set_rows), lambda cb, tp: (tp, 0, cb)), tok_spec],
        out_shape=[jax.ShapeDtypeStruct((CHUNK, D_MODEL, bsz * n_chunks), BF16),
                   jax.ShapeDtypeStruct((n_blocks, set_rows, CHUNK * D_MODEL), BF16)],
        scratch_shapes=_set_buffers(bsz, s),
        compiler_params=_params(2),
        name="in_proj_t",
    )(x4, norm_g, wut, w_in, w_in)


def _s5_kernel(u_ref, mt_ref, wst_ref, woutt_ref, aq_ref, y_ref, sloc_ref, pf_ref, pb_ref,
               *, n_chunks, batch):
    n = batch * n_chunks
    groups = range(GROUPS_PER_STEP)
    uts = [u_ref[:, j * SSM_GROUP:(j + 1) * SSM_GROUP, :].reshape(CW, n) for j in groups]

    for j in groups:
        sl = lax.dot_general(uts[j], wst_ref[j], TN_DIMS, preferred_element_type=F32)
        sloc_ref[j, 0] = sl[:, :HALF]
        sloc_ref[j, 1] = sl[:, HALF:]

    is_fwd = lax.broadcasted_iota(jnp.int32, (batch, HALF), 1) < STATE
    ar = [jnp.broadcast_to(aq_ref[j, 0:1, :], (batch, HALF)) for j in groups]
    ai = [jnp.broadcast_to(aq_ref[j, 1:2, :], (batch, HALF)) for j in groups]

    zero = jnp.zeros((batch, HALF), F32)
    last = n_chunks * batch
    for j in groups:
        for h in range(2):
            pf_ref[j, h, 0:batch, :] = zero
            pb_ref[j, h, last:last + batch, :] = zero

    def scan_step(k, carry):
        rf = k * batch
        rb = (n_chunks - 1 - k) * batch
        ro = (k + 1) * batch
        new = []
        for j in groups:
            sr, si = carry[j]
            xr = jnp.where(is_fwd, sloc_ref[j, 0, pl.ds(rf, batch), :],
                           sloc_ref[j, 0, pl.ds(rb, batch), :])
            xi = jnp.where(is_fwd, sloc_ref[j, 1, pl.ds(rf, batch), :],
                           sloc_ref[j, 1, pl.ds(rb, batch), :])
            nr = ar[j] * sr - ai[j] * si + xr
            ni = ar[j] * si + ai[j] * sr + xi
            pf_ref[j, 0, pl.ds(ro, batch), :] = nr
            pf_ref[j, 1, pl.ds(ro, batch), :] = ni
            pb_ref[j, 0, pl.ds(rb, batch), :] = nr
            pb_ref[j, 1, pl.ds(rb, batch), :] = ni
            new.append((nr, ni))
        return tuple(new)

    carry = tuple((zero, zero) for _ in groups)
    for k in range(n_chunks):
        carry = scan_step(k, carry)

    fwd_rows = lax.broadcasted_iota(jnp.int32, (n, HALF), 1) < STATE
    for j in groups:
        prev = jnp.concatenate(
            [jnp.where(fwd_rows, pf_ref[j, h, 0:n, :], pb_ref[j, h, batch:batch + n, :])
             for h in range(2)], axis=1)
        y = jnp.dot(mt_ref[j], uts[j], preferred_element_type=F32)
        y = y + lax.dot_general(woutt_ref[j], prev.astype(BF16), NT_DIMS, preferred_element_type=F32)
        y_ref[:, j * SSM_GROUP:(j + 1) * SSM_GROUP, :] = y.reshape(CHUNK, SSM_GROUP, n)


def _s5_t(ut, mt, wst, woutt, aq, *, n_chunks, batch):
    n = batch * n_chunks
    gps = GROUPS_PER_STEP
    kern = functools.partial(_s5_kernel, n_chunks=n_chunks, batch=batch)
    op_spec = pl.BlockSpec((gps, CW, CW), lambda g: (g, 0, 0))
    act_spec = pl.BlockSpec((CHUNK, gps * SSM_GROUP, n), lambda g: (0, g, 0))
    state_rows = pltpu.VMEM((gps, 2, n + batch, HALF), F32)
    return pl.pallas_call(
        kern,
        grid=(N_GROUPS // gps,),
        in_specs=[act_spec, op_spec, op_spec, op_spec,
                  pl.BlockSpec((gps, 2, HALF), lambda g: (g, 0, 0))],
        out_specs=act_spec,
        out_shape=jax.ShapeDtypeStruct((CHUNK, D_MODEL, n), F32),
        scratch_shapes=[pltpu.VMEM((gps, 2, n, HALF), F32), state_rows, state_rows],
        compiler_params=_params(1),
        name="s5_t",
    )(ut, mt, wst, woutt, aq)


def _prep_kernel(lam_ref, c_ref, bt_ref, dd_ref, win_ref, w1_ref, w2_ref, w3_ref, w4_ref,
                 mt_ref, wst_ref, woutt_ref, aq_ref, winbf_ref, wut_ref, w1bf_ref, w2bf_ref, w3bf_ref,
                 w4bf_ref, pwr_ref, pwi_ref):
    for g in range(OPS_GROUPS_PER_STEP):
        _s5_ops_group(g, lam_ref, c_ref, bt_ref, dd_ref, mt_ref, wst_ref, woutt_ref, aq_ref,
                      pwr_ref, pwi_ref)

    step = pl.program_id(0)
    winbf_ref[...] = win_ref[...].astype(BF16)

    @pl.when(step == COL_U)
    def _():
        wut_ref[...] = win_ref[...].T.astype(BF16)

    for k, (src, dst) in enumerate([(w1_ref, w1bf_ref), (w2_ref, w2bf_ref), (w3_ref, w3bf_ref),
                                    (w4_ref, w4bf_ref)]):
        @pl.when(step == k)
        def _(src=src, dst=dst):
            dst[...] = src[...].astype(BF16)


def _s5_ops_group(g, lam_ref, c_ref, bt_ref, dd_ref, mt_ref, wst_ref, woutt_ref, aq_ref,
                  pwr_ref, pwi_ref):
    lr, li = lam_ref[g, 0:1, :], lam_ref[g, 1:2, :]
    dt = jnp.exp(lam_ref[g, 2:3, :])
    c_re, c_im = c_ref[g, 0], c_ref[g, 1]
    bt_re, bt_im = bt_ref[g, 0], bt_ref[g, 1]

    n_rows = 2 * CHUNK + SUBLANES
    r = lax.broadcasted_iota(jnp.int32, (n_rows, HALF), 0)
    fwd = lax.broadcasted_iota(jnp.int32, (n_rows, HALF), 1) < STATE
    s = r - CHUNK
    e = jnp.where(r < CHUNK, jnp.where(fwd, r + 1, CHUNK - r),
                  jnp.where(r < 2 * CHUNK, jnp.where(fwd, CHUNK - 1 - s, s),
                            jnp.where(r == 2 * CHUNK, CHUNK, 1))).astype(F32)
    mag = jnp.exp(e * (lr * dt))
    ang = e * (li * dt)
    pwr_ref[g] = mag * jnp.cos(ang)
    pwi_ref[g] = mag * jnp.sin(ang)

    def power(row):
        return pwr_ref[g, row:row + 1, :], pwi_ref[g, row:row + 1, :]

    a_re, a_im = power(2 * CHUNK + 1)
    den = lr * lr + li * li
    nr, ni = a_re - 1.0, a_im
    coef_re = (nr * lr + ni * li) / den
    coef_im = (ni * lr - nr * li) / den
    bb_re = coef_re * bt_re - coef_im * bt_im
    bb_im = coef_re * bt_im + coef_im * bt_re

    st_re, st_im = [], []
    for k in range(CHUNK):
        p_re, p_im = power(CHUNK + k)
        st_re.append(p_re * bb_re - p_im * bb_im)
        st_im.append(p_re * bb_im + p_im * bb_re)
    wst = jnp.concatenate([jnp.concatenate(st_re, axis=0), jnp.concatenate(st_im, axis=0)], axis=1)
    wst_ref[g] = wst.astype(BF16)

    out_rows = []
    for k in range(CHUNK):
        p_re, p_im = power(k)
        out_rows.append(jnp.concatenate([c_re * p_re - c_im * p_im, -(c_re * p_im + c_im * p_re)],
                                        axis=1))
    woutt_ref[g] = jnp.concatenate(out_rows, axis=0).astype(BF16)

    aq_ref[g, 0:1, :], aq_ref[g, 1:2, :] = power(2 * CHUNK)

    fwd1 = lax.broadcasted_iota(jnp.int32, (SSM_GROUP, HALF), 1) < STATE
    zero = jnp.zeros_like(c_re)
    cmat = jnp.concatenate([
        jnp.concatenate([jnp.where(fwd1, c_re, zero), -jnp.where(fwd1, c_im, zero)], axis=1),
        jnp.concatenate([jnp.where(fwd1, zero, c_re), -jnp.where(fwd1, zero, c_im)], axis=1)], axis=0)
    kslab = lax.dot_general(cmat, wst, NT_DIMS, precision=lax.Precision.HIGHEST,
                            preferred_element_type=F32)
    k_f = kslab[:SSM_GROUP]
    k_b = kslab[SSM_GROUP:] + dd_ref[g]

    lane = lax.broadcasted_iota(jnp.int32, (SSM_GROUP, CW), 1)
    m_rows = []
    for t in range(CHUNK):
        left = SSM_GROUP * (CHUNK - 1 - t)
        right = SSM_GROUP * t
        a = k_f if left == 0 else jnp.where(lane < CW - left, pltpu.roll(k_f, CW - left, 1), 0.0)
        b = k_b if right == 0 else jnp.where(lane >= right, pltpu.roll(k_b, right, 1), 0.0)
        m_rows.append(a + b)
    mt_ref[g] = jnp.concatenate(m_rows, axis=0).astype(BF16)


def _prep(lam_re, lam_im, log_dt, b_re, b_im, c_re, c_im, ssm_d, w_in, w_glu, w_a, w_b, w_o):
    def lanes(x):
        return jnp.concatenate([x[0], x[1]], axis=-1)

    lam = jnp.stack([lanes(lam_re), lanes(lam_im),
                     lanes(jnp.broadcast_to(log_dt[..., None], lam_re.shape))], axis=1)
    c = jnp.stack([lanes(c_re), lanes(c_im)], axis=1)
    bt = jnp.stack([lanes(b_re.transpose(0, 1, 3, 2)), lanes(b_im.transpose(0, 1, 3, 2))], axis=1)
    dd = ssm_d.reshape(N_GROUPS, SSM_GROUP)[:, :, None] * jnp.eye(SSM_GROUP, CW, dtype=F32)

    gps = OPS_GROUPS_PER_STEP
    n_steps = N_GROUPS // gps
    assert n_steps == w_in.shape[1] // D_MODEL
    op_spec = pl.BlockSpec((gps, CW, CW), lambda g: (g, 0, 0))
    par_spec = pl.BlockSpec((gps, 2, SSM_GROUP, HALF), lambda g: (g, 0, 0, 0))
    op_shape = jax.ShapeDtypeStruct((N_GROUPS, CW, CW), BF16)
    sq = (D_MODEL, D_MODEL)
    sq_spec = pl.BlockSpec(sq, lambda g: (0, 0))
    sq_shape = jax.ShapeDtypeStruct(sq, BF16)
    col_spec = pl.BlockSpec(sq, lambda g: (0, g))
    return pl.pallas_call(
        _prep_kernel,
        grid=(n_steps,),
        in_specs=[pl.BlockSpec((gps, 3, HALF), lambda g: (g, 0, 0)), par_spec, par_spec,
                  pl.BlockSpec((gps, SSM_GROUP, CW), lambda g: (g, 0, 0)),
                  col_spec, _const_spec(sq), _const_spec(sq), _const_spec(sq), _const_spec(sq)],
        out_specs=[op_spec, op_spec, op_spec, pl.BlockSpec((gps, 2, HALF), lambda g: (g, 0, 0)),
                   col_spec, sq_spec, sq_spec, sq_spec, sq_spec, sq_spec],
        out_shape=[op_shape, op_shape, op_shape, jax.ShapeDtypeStruct((N_GROUPS, 2, HALF), F32),
                   jax.ShapeDtypeStruct(w_in.shape, BF16), sq_shape, sq_shape, sq_shape, sq_shape,
                   sq_shape],
        scratch_shapes=[pltpu.VMEM((gps, 2 * CHUNK + SUBLANES, HALF), F32)] * 2,
        compiler_params=_params(1),
        name="prep",
    )(lam, c, bt, dd, w_in, w_glu, w_a, w_b, w_o)


def _block_kernel(x_hbm, yt_ref, cv_ref, cvprev_ref, cvnext_ref, haloprev_ref, halonext_ref, ng_ref,
                  pza_ref, pb_ref, pzb_ref, pga_ref, pgb_ref, wglu_ref, wa_ref, wb_ref, wo_ref,
                  cw_ref, cb_ref, fg_ref, o_hbm, xn_ref, xbuf, xsem, obuf, osem, *, set_rows, batch):
    blk = pl.program_id(0)
    tp = pl.program_id(1)
    step = pl.program_id(0) * pl.num_programs(1) + tp
    n_steps = pl.num_programs(0) * pl.num_programs(1)
    x = _fetch_sets(x_hbm, xbuf, xsem, step, n_steps, SETS_PER_STEP)
    xn_ref[...] = _rmsnorm(x, ng_ref[...]).astype(BF16)

    def proj(w_ref):
        return jnp.dot(xn_ref[...], w_ref[...], preferred_element_type=F32)

    cvprev = cvprev_ref[...].astype(F32)
    cvnext = cvnext_ref[...].astype(F32)
    halo_prev = jnp.where(blk == 0, 0.0, haloprev_ref[...].astype(F32)[batch:])
    halo_next = jnp.where(blk == pl.num_programs(0) - 1, 0.0, halonext_ref[...].astype(F32)[:batch])
    cvprev = jnp.where(tp == 0,
                       jnp.concatenate([halo_prev, cvprev[:set_rows - batch]], axis=0), cvprev)
    cvnext = jnp.where(tp == CHUNK // SETS_PER_STEP - 1,
                       jnp.concatenate([cvnext[batch:], halo_next], axis=0), cvnext)
    cvs = [cv_ref[:, k * D_MODEL:(k + 1) * D_MODEL].astype(F32) for k in range(SETS_PER_STEP)]
    cvm = jnp.concatenate([cvprev] + cvs[:-1], axis=0)
    cvp = jnp.concatenate(cvs[1:] + [cvnext], axis=0)
    cw = cw_ref[...]
    conv = cb_ref[...] + cvm * cw[0:1] + jnp.concatenate(cvs, axis=0) * cw[1:2] + cvp * cw[2:3]

    z_b = proj(pzb_ref)
    y_b = (proj(pb_ref) * conv * (z_b * _sigmoid(z_b))).astype(BF16)
    merged = _sigmoid(proj(pgb_ref)) * jnp.dot(y_b, wb_ref[...], preferred_element_type=F32)

    yg = _gelu_tanh(jnp.concatenate([yt_ref[k].T for k in range(SETS_PER_STEP)], axis=0))
    yg = yg * _sigmoid(jnp.dot(yg.astype(BF16), wglu_ref[...], preferred_element_type=F32))
    z_a = proj(pza_ref)
    y_a = (yg * (z_a * _sigmoid(z_a))).astype(BF16)
    merged = merged + _sigmoid(proj(pga_ref)) * jnp.dot(y_a, wa_ref[...], preferred_element_type=F32)

    h = x + jnp.dot(merged.astype(BF16), wo_ref[...], preferred_element_type=F32)
    out = _rmsnorm(h, fg_ref[...])

    slot = step % 2

    @pl.when(step >= 2)
    def _():
        for c in _set_copies(o_hbm, obuf, osem, step - 2, slot, SETS_PER_STEP, to_hbm=True):
            c.wait()

    for k in range(SETS_PER_STEP):
        obuf[slot, k] = out[k * set_rows:(k + 1) * set_rows].reshape(CHUNK_BLOCK, batch, D_MODEL)
    for c in _set_copies(o_hbm, obuf, osem, step, slot, SETS_PER_STEP, to_hbm=True):
        c.start()

    @pl.when(step == n_steps - 1)
    def _():
        @pl.when(step >= 1)
        def _():
            for c in _set_copies(o_hbm, obuf, osem, step - 1, 1 - slot, SETS_PER_STEP, to_hbm=True):
                c.wait()
        for c in _set_copies(o_hbm, obuf, osem, step, slot, SETS_PER_STEP, to_hbm=True):
            c.wait()


def _block_t(x4, yt, cv3, norm_g, w_in, w_glu, w_a, w_b, w_o, cw, cb, final_g, *, bsz, n_chunks):
    s = SETS_PER_STEP
    n_blocks = n_chunks // CHUNK_BLOCK
    set_rows = CHUNK_BLOCK * bsz
    halo = 2 * bsz
    tok_spec = pl.BlockSpec((None, set_rows, s * D_MODEL), lambda cb, tp: (cb, 0, tp))
    prev_spec = pl.BlockSpec((None, set_rows, D_MODEL),
                             lambda cb, tp: (cb, 0, (tp * s + CHUNK - 1) % CHUNK))
    next_spec = pl.BlockSpec((None, set_rows, D_MODEL), lambda cb, tp: (cb, 0, (tp * s + s) % CHUNK))
    haloprev_spec = pl.BlockSpec((None, halo, D_MODEL),
                                 lambda cb, tp: (jnp.maximum(cb - 1, 0), set_rows // halo - 1, CHUNK - 1))
    halonext_spec = pl.BlockSpec((None, halo, D_MODEL),
                                 lambda cb, tp: (jnp.minimum(cb + 1, n_blocks - 1), 0, 0))
    kern = functools.partial(_block_kernel, set_rows=set_rows, batch=bsz)
    return pl.pallas_call(
        kern,
        grid=(n_blocks, CHUNK // s),
        in_specs=[
            pl.BlockSpec(memory_space=pl.ANY),
            pl.BlockSpec((s, D_MODEL, set_rows), lambda cb, tp: (tp, 0, cb)),
            tok_spec, prev_spec, next_spec, haloprev_spec, halonext_spec,
            _const_spec((1, D_MODEL)),
            _col_spec(COL_ZA), _col_spec(COL_B), _col_spec(COL_ZB), _col_spec(COL_GA), _col_spec(COL_GB),
            _const_spec((D_MODEL, D_MODEL)),
            _const_spec((D_MODEL, D_MODEL)),
            _const_spec((D_MODEL, D_MODEL)),
            _const_spec((D_MODEL, D_MODEL)),
            _const_spec((3, D_MODEL)),
            _const_spec((1, D_MODEL)),
            _const_spec((1, D_MODEL)),
        ],
        out_specs=pl.BlockSpec(memory_space=pl.ANY),
        out_shape=jax.ShapeDtypeStruct(x4.shape, F32),
        scratch_shapes=([pltpu.VMEM((s * set_rows, D_MODEL), BF16)]
                        + _set_buffers(bsz, s) + _set_buffers(bsz, s)),
        compiler_params=_params(2),
        name="block_t",
    )(x4, yt, cv3, cv3, cv3, cv3, cv3, norm_g, w_in, w_in, w_in, w_in, w_in, w_glu, w_a, w_b, w_o, cw, cb,
      final_g)


def kernel(x, norm_g, w_in, lam_re, lam_im, log_dt, ssm_b_re, ssm_b_im, ssm_c_re, ssm_c_im, ssm_d,
           w_glu, conv_w, conv_b, w_branch_a, w_branch_b, w_out, final_g):
    bsz, seq_len, d = x.shape
    assert norm_g.shape[0] == 1 and d == D_MODEL and bsz == SUBLANES
    n_chunks = seq_len // CHUNK
    assert seq_len == n_chunks * CHUNK and n_chunks % CHUNK_BLOCK == 0
    assert (CHUNK_BLOCK * bsz) % LANES == 0

    x4 = x.reshape(bsz, n_chunks, CHUNK, d)
    ng = norm_g[0].reshape(1, d)
    mt, wst, woutt, aq, w_bf, wut, wglu_bf, wa_bf, wb_bf, wo_bf = _prep(
        lam_re[0], lam_im[0], log_dt[0], ssm_b_re[0], ssm_b_im[0], ssm_c_re[0], ssm_c_im[0],
        ssm_d[0], w_in[0], w_glu[0], w_branch_a[0], w_branch_b[0], w_out[0])

    ut, cv3 = _in_proj_t(x4, ng, wut, w_bf, bsz=bsz, n_chunks=n_chunks)
    yt = _s5_t(ut, mt, wst, woutt, aq, n_chunks=n_chunks, batch=bsz)

    out4 = _block_t(
        x4, yt, cv3, ng, w_bf, wglu_bf, wa_bf, wb_bf, wo_bf, conv_w[0], conv_b[0].reshape(1, d),
        final_g.reshape(1, d), bsz=bsz, n_chunks=n_chunks)
    return out4.reshape(bsz, seq_len, d)
```

```python
import functools
import math

import jax
import jax.numpy as jnp
from jax import lax
from jax.experimental import pallas as pl
from jax.experimental.pallas import tpu as pltpu

F32 = jnp.float32
BF16 = jnp.bfloat16

D_MODEL = 1024
SSM_GROUP = 16
N_GROUPS = D_MODEL // SSM_GROUP
STATE = 64
CHUNK = 16
CW = CHUNK * SSM_GROUP
CHUNK_BLOCK = 32
SW = 4 * STATE
HALF = 2 * STATE
LANES = 128
EPS = 1e-6
SUBLANES = 8
IN_SETS_PER_STEP = 4
SETS_PER_STEP = 2
GROUPS_PER_STEP = 4
OPS_GROUPS_PER_STEP = 8
VMEM_LIMIT_BYTES = 60 * 1024 * 1024

NT_DIMS = (((1,), (1,)), ((), ()))
TN_DIMS = (((0,), (0,)), ((), ()))


def _rmsnorm(x, g):
    ms = jnp.mean(x * x, axis=-1, keepdims=True)
    return x * lax.rsqrt(ms + EPS) * g


def _sigmoid(x):
    return 0.5 * jnp.tanh(0.5 * x) + 0.5


def _gelu_tanh(x):
    c = math.sqrt(2.0 / math.pi)
    return x * (0.5 * (1.0 + jnp.tanh(c * (x + 0.044715 * (x * x * x)))))


def _params(n_axes):
    return pltpu.CompilerParams(
        dimension_semantics=("arbitrary",) * n_axes, vmem_limit_bytes=VMEM_LIMIT_BYTES)


def _const_spec(shape):
    return pl.BlockSpec(shape, lambda *_: (0,) * len(shape), pipeline_mode=pl.Buffered(1))


COL_U, COL_ZA, COL_V, COL_B, COL_C, COL_ZB, COL_GA, COL_GB = range(8)


def _col_spec(col):
    return pl.BlockSpec((D_MODEL, D_MODEL), lambda *_: (0, col), pipeline_mode=pl.Buffered(1))


def _set_copies(x_hbm, buf, sem, step, slot, sets, *, to_hbm=False):
    steps_per_block = CHUNK // sets
    c0 = (step // steps_per_block) * CHUNK_BLOCK
    tl0 = (step % steps_per_block) * sets
    copies = []
    for k in range(sets):
        for b in range(x_hbm.shape[0]):
            hbm = x_hbm.at[b, pl.ds(c0, CHUNK_BLOCK), tl0 + k, :]
            vmem = buf.at[slot, k, :, b, :]
            src, dst = (vmem, hbm) if to_hbm else (hbm, vmem)
            copies.append(pltpu.make_async_copy(src, dst, sem.at[slot, k]))
    return copies


def _fetch_sets(x_hbm, xbuf, sem, step, n_steps, sets):
    slot = step % 2

    @pl.when(step == 0)
    def _():
        for c in _set_copies(x_hbm, xbuf, sem, step, slot, sets):
            c.start()

    @pl.when(step + 1 < n_steps)
    def _():
        for c in _set_copies(x_hbm, xbuf, sem, step + 1, 1 - slot, sets):
            c.start()

    for c in _set_copies(x_hbm, xbuf, sem, step, slot, sets):
        c.wait()
    rows = xbuf.shape[2] * xbuf.shape[3]
    return jnp.concatenate([xbuf[slot, k].reshape(rows, D_MODEL) for k in range(sets)], axis=0)


def _set_buffers(batch, sets):
    return [pltpu.VMEM((2, sets, CHUNK_BLOCK, batch, D_MODEL), F32),
            pltpu.SemaphoreType.DMA((2, sets))]


def _in_proj_kernel(x_hbm, g_ref, wut_ref, wv_ref, wc_ref, u_ref, cv_ref, xbuf, xsem, *, set_rows):
    step = pl.program_id(0) * pl.num_programs(1) + pl.program_id(1)
    n_steps = pl.num_programs(0) * pl.num_programs(1)
    x = _fetch_sets(x_hbm, xbuf, xsem, step, n_steps, IN_SETS_PER_STEP)
    xn = _rmsnorm(x, g_ref[...]).astype(BF16)
    ut = lax.dot_general(wut_ref[...], xn, NT_DIMS, preferred_element_type=F32)
    cv = (jnp.dot(xn, wv_ref[...], preferred_element_type=F32)
          * jnp.dot(xn, wc_ref[...], preferred_element_type=F32)).astype(BF16)
    for k in range(IN_SETS_PER_STEP):
        u_ref[k] = ut[:, k * set_rows:(k + 1) * set_rows].astype(BF16)
        cv_ref[:, k * D_MODEL:(k + 1) * D_MODEL] = cv[k * set_rows:(k + 1) * set_rows]


def _in_proj_t(x4, norm_g, wut, w_in, *, bsz, n_chunks):
    s = IN_SETS_PER_STEP
    n_blocks = n_chunks // CHUNK_BLOCK
    set_rows = CHUNK_BLOCK * bsz
    tok_spec = pl.BlockSpec((None, set_rows, s * D_MODEL), lambda cb, tp: (cb, 0, tp))
    return pl.pallas_call(
        functools.partial(_in_proj_kernel, set_rows=set_rows),
        grid=(n_blocks, CHUNK // s),
        in_specs=[pl.BlockSpec(memory_space=pl.ANY), _const_spec((1, D_MODEL)),
                  _const_spec((D_MODEL, D_MODEL)), _col_spec(COL_V), _col_spec(COL_C)],
        out_specs=[pl.BlockSpec((s, D_MODEL, set_rows), lambda cb, tp: (tp, 0, cb)), tok_spec],
        out_shape=[jax.ShapeDtypeStruct((CHUNK, D_MODEL, bsz * n_chunks), BF16),
                   jax.ShapeDtypeStruct((n_blocks, set_rows, CHUNK * D_MODEL), BF16)],
        scratch_shapes=_set_buffers(bsz, s),
        compiler_params=_params(2),
        name="in_proj_t",
    )(x4, norm_g, wut, w_in, w_in)


def _s5_kernel(u_ref, mt_ref, wst_ref, woutt_ref, aq_ref, y_ref, sloc_ref, pf_ref, pb_ref,
               *, n_chunks, batch):
    n = batch * n_chunks
    groups = range(GROUPS_PER_STEP)
    uts = [u_ref[:, j * SSM_GROUP:(j + 1) * SSM_GROUP, :].reshape(CW, n) for j in groups]

    for j in groups:
        sl = lax.dot_general(uts[j], wst_ref[j], TN_DIMS, preferred_element_type=F32)
        sloc_ref[j, 0] = sl[:, :HALF]
        sloc_ref[j, 1] = sl[:, HALF:]

    is_fwd = lax.broadcasted_iota(jnp.int32, (batch, HALF), 1) < STATE
    ar = [jnp.broadcast_to(aq_ref[j, 0:1, :], (batch, HALF)) for j in groups]
    ai = [jnp.broadcast_to(aq_ref[j, 1:2, :], (batch, HALF)) for j in groups]

    zero = jnp.zeros((batch, HALF), F32)
    last = n_chunks * batch
    for j in groups:
        for h in range(2):
            pf_ref[j, h, 0:batch, :] = zero
            pb_ref[j, h, last:last + batch, :] = zero

    def scan_step(k, carry):
        rf = k * batch
        rb = (n_chunks - 1 - k) * batch
        ro = (k + 1) * batch
        new = []
        for j in groups:
            sr, si = carry[j]
            xr = jnp.where(is_fwd, sloc_ref[j, 0, pl.ds(rf, batch), :],
                           sloc_ref[j, 0, pl.ds(rb, batch), :])
            xi = jnp.where(is_fwd, sloc_ref[j, 1, pl.ds(rf, batch), :],
                           sloc_ref[j, 1, pl.ds(rb, batch), :])
            nr = ar[j] * sr - ai[j] * si + xr
            ni = ar[j] * si + ai[j] * sr + xi
            pf_ref[j, 0, pl.ds(ro, batch), :] = nr
            pf_ref[j, 1, pl.ds(ro, batch), :] = ni
            pb_ref[j, 0, pl.ds(rb, batch), :] = nr
            pb_ref[j, 1, pl.ds(rb, batch), :] = ni
            new.append((nr, ni))
        return tuple(new)

    carry = tuple((zero, zero) for _ in groups)
    for k in range(n_chunks):
        carry = scan_step(k, carry)

    fwd_rows = lax.broadcasted_iota(jnp.int32, (n, HALF), 1) < STATE
    for j in groups:
        prev = jnp.concatenate(
            [jnp.where(fwd_rows, pf_ref[j, h, 0:n, :], pb_ref[j, h, batch:batch + n, :])
             for h in range(2)], axis=1)
        y = jnp.dot(mt_ref[j], uts[j], preferred_element_type=F32)
        y = y + lax.dot_general(woutt_ref[j], prev.astype(BF16), NT_DIMS, preferred_element_type=F32)
        y_ref[:, j * SSM_GROUP:(j + 1) * SSM_GROUP, :] = y.reshape(CHUNK, SSM_GROUP, n)


def _s5_t(ut, mt, wst, woutt, aq, *, n_chunks, batch):
    n = batch * n_chunks
    gps = GROUPS_PER_STEP
    kern = functools.partial(_s5_kernel, n_chunks=n_chunks, batch=batch)
    op_spec = pl.BlockSpec((gps, CW, CW), lambda g: (g, 0, 0))
    act_spec = pl.BlockSpec((CHUNK, gps * SSM_GROUP, n), lambda g: (0, g, 0))
    state_rows = pltpu.VMEM((gps, 2, n + batch, HALF), F32)
    return pl.pallas_call(
        kern,
        grid=(N_GROUPS // gps,),
        in_specs=[act_spec, op_spec, op_spec, op_spec,
                  pl.BlockSpec((gps, 2, HALF), lambda g: (g, 0, 0))],
        out_specs=act_spec,
        out_shape=jax.ShapeDtypeStruct((CHUNK, D_MODEL, n), F32),
        scratch_shapes=[pltpu.VMEM((gps, 2, n, HALF), F32), state_rows, state_rows],
        compiler_params=_params(1),
        name="s5_t",
    )(ut, mt, wst, woutt, aq)


def _prep_kernel(lam_ref, c_ref, bt_ref, dd_ref, win_ref, w1_ref, w2_ref, w3_ref, w4_ref,
                 mt_ref, wst_ref, woutt_ref, aq_ref, winbf_ref, wut_ref, w1bf_ref, w2bf_ref, w3bf_ref,
                 w4bf_ref):
    for g in range(OPS_GROUPS_PER_STEP):
        _s5_ops_group(g, lam_ref, c_ref, bt_ref, dd_ref, mt_ref, wst_ref, woutt_ref, aq_ref)

    winbf_ref[...] = win_ref[...].astype(BF16)

    @pl.when(pl.program_id(0) == COL_U)
    def _():
        wut_ref[...] = win_ref[...].T.astype(BF16)

    for src, dst in ((w1_ref, w1bf_ref), (w2_ref, w2bf_ref), (w3_ref, w3bf_ref), (w4_ref, w4bf_ref)):
        dst[...] = src[...].astype(BF16)


def _s5_ops_group(g, lam_ref, c_ref, bt_ref, dd_ref, mt_ref, wst_ref, woutt_ref, aq_ref):
    lr, li = lam_ref[g, 0:1, :], lam_ref[g, 1:2, :]
    dt = jnp.exp(lam_ref[g, 2:3, :])
    c_re, c_im = c_ref[g, 0], c_ref[g, 1]
    bt_re, bt_im = bt_ref[g, 0], bt_ref[g, 1]

    mag = jnp.exp(lr * dt)
    a_re, a_im = mag * jnp.cos(li * dt), mag * jnp.sin(li * dt)
    pw = [(jnp.ones_like(a_re), jnp.zeros_like(a_im))]
    for _ in range(CHUNK):
        p_re, p_im = pw[-1]
        pw.append((p_re * a_re - p_im * a_im, p_re * a_im + p_im * a_re))
    fwd_row = lax.broadcasted_iota(jnp.int32, (1, HALF), 1) < STATE

    def power(e_fwd, e_bwd):
        return (jnp.where(fwd_row, pw[e_fwd][0], pw[e_bwd][0]),
                jnp.where(fwd_row, pw[e_fwd][1], pw[e_bwd][1]))

    den = lr * lr + li * li
    nr, ni = a_re - 1.0, a_im
    coef_re = (nr * lr + ni * li) / den
    coef_im = (ni * lr - nr * li) / den
    bb_re = coef_re * bt_re - coef_im * bt_im
    bb_im = coef_re * bt_im + coef_im * bt_re

    st_re, st_im = [], []
    for k in range(CHUNK):
        p_re, p_im = power(CHUNK - 1 - k, k)
        st_re.append(p_re * bb_re - p_im * bb_im)
        st_im.append(p_re * bb_im + p_im * bb_re)
    wst = jnp.concatenate([jnp.concatenate(st_re, axis=0), jnp.concatenate(st_im, axis=0)], axis=1)
    wst_ref[g] = wst.astype(BF16)

    out_rows = []
    for k in range(CHUNK):
        p_re, p_im = power(k + 1, CHUNK - k)
        out_rows.append(jnp.concatenate([c_re * p_re - c_im * p_im, -(c_re * p_im + c_im * p_re)],
                                        axis=1))
    woutt_ref[g] = jnp.concatenate(out_rows, axis=0).astype(BF16)

    aq_ref[g, 0:1, :], aq_ref[g, 1:2, :] = pw[CHUNK]

    fwd1 = lax.broadcasted_iota(jnp.int32, (SSM_GROUP, HALF), 1) < STATE
    zero = jnp.zeros_like(c_re)
    cmat = jnp.concatenate([
        jnp.concatenate([jnp.where(fwd1, c_re, zero), -jnp.where(fwd1, c_im, zero)], axis=1),
        jnp.concatenate([jnp.where(fwd1, zero, c_re), -jnp.where(fwd1, zero, c_im)], axis=1)], axis=0)
    kslab = lax.dot_general(cmat, wst, NT_DIMS, precision=lax.Precision.HIGHEST,
                            preferred_element_type=F32)
    k_f = kslab[:SSM_GROUP]
    k_b = kslab[SSM_GROUP:] + dd_ref[g]

    lane = lax.broadcasted_iota(jnp.int32, (SSM_GROUP, CW), 1)
    m_rows = []
    for t in range(CHUNK):
        left = SSM_GROUP * (CHUNK - 1 - t)
        right = SSM_GROUP * t
        a = k_f if left == 0 else jnp.where(lane < CW - left, pltpu.roll(k_f, CW - left, 1), 0.0)
        b = k_b if right == 0 else jnp.where(lane >= right, pltpu.roll(k_b, right, 1), 0.0)
        m_rows.append(a + b)
    mt_ref[g] = jnp.concatenate(m_rows, axis=0).astype(BF16)


def _prep(lam_re, lam_im, log_dt, b_re, b_im, c_re, c_im, ssm_d, w_in, w_glu, w_a, w_b, w_o):
    def lanes(x):
        return jnp.concatenate([x[0], x[1]], axis=-1)

    lam = jnp.stack([lanes(lam_re), lanes(lam_im),
                     lanes(jnp.broadcast_to(log_dt[..., None], lam_re.shape))], axis=1)
    c = jnp.stack([lanes(c_re), lanes(c_im)], axis=1)
    bt = jnp.stack([lanes(b_re.transpose(0, 1, 3, 2)), lanes(b_im.transpose(0, 1, 3, 2))], axis=1)
    dd = ssm_d.reshape(N_GROUPS, SSM_GROUP)[:, :, None] * jnp.eye(SSM_GROUP, CW, dtype=F32)

    gps = OPS_GROUPS_PER_STEP
    n_steps = N_GROUPS // gps
    assert n_steps == w_in.shape[1] // D_MODEL
    op_spec = pl.BlockSpec((gps, CW, CW), lambda g: (g, 0, 0))
    par_spec = pl.BlockSpec((gps, 2, SSM_GROUP, HALF), lambda g: (g, 0, 0, 0))
    op_shape = jax.ShapeDtypeStruct((N_GROUPS, CW, CW), BF16)
    sq = (D_MODEL, D_MODEL)
    sq_shape = jax.ShapeDtypeStruct(sq, BF16)
    col_spec = pl.BlockSpec(sq, lambda g: (0, g))
    row_spec = pl.BlockSpec((D_MODEL // n_steps, D_MODEL), lambda g: (g, 0))
    return pl.pallas_call(
        _prep_kernel,
        grid=(n_steps,),
        in_specs=[pl.BlockSpec((gps, 3, HALF), lambda g: (g, 0, 0)), par_spec, par_spec,
                  pl.BlockSpec((gps, SSM_GROUP, CW), lambda g: (g, 0, 0)),
                  col_spec, row_spec, row_spec, row_spec, row_spec],
        out_specs=[op_spec, op_spec, op_spec, pl.BlockSpec((gps, 2, HALF), lambda g: (g, 0, 0)),
                   col_spec, pl.BlockSpec(sq, lambda g: (0, 0)), row_spec, row_spec, row_spec, row_spec],
        out_shape=[op_shape, op_shape, op_shape, jax.ShapeDtypeStruct((N_GROUPS, 2, HALF), F32),
                   jax.ShapeDtypeStruct(w_in.shape, BF16), sq_shape, sq_shape, sq_shape, sq_shape,
                   sq_shape],
        compiler_params=_params(1),
        name="prep",
    )(lam, c, bt, dd, w_in, w_glu, w_a, w_b, w_o)


def _block_kernel(x_hbm, yt_ref, cv_ref, cvprev_ref, cvnext_ref, haloprev_ref, halonext_ref, ng_ref,
                  pza_ref, pb_ref, pzb_ref, pga_ref, pgb_ref, wglu_ref, wa_ref, wb_ref, wo_ref,
                  cw_ref, cb_ref, fg_ref, o_hbm, xn_ref, xbuf, xsem, obuf, osem, *, set_rows, batch):
    blk = pl.program_id(0)
    tp = pl.program_id(1)
    step = pl.program_id(0) * pl.num_programs(1) + tp
    n_steps = pl.num_programs(0) * pl.num_programs(1)
    x = _fetch_sets(x_hbm, xbuf, xsem, step, n_steps, SETS_PER_STEP)
    xn_ref[...] = _rmsnorm(x, ng_ref[...]).astype(BF16)

    def proj(w_ref):
        return jnp.dot(xn_ref[...], w_ref[...], preferred_element_type=F32)

    cvprev = cvprev_ref[...].astype(F32)
    cvnext = cvnext_ref[...].astype(F32)
    halo_prev = jnp.where(blk == 0, 0.0, haloprev_ref[...].astype(F32)[batch:])
    halo_next = jnp.where(blk == pl.num_programs(0) - 1, 0.0, halonext_ref[...].astype(F32)[:batch])
    cvprev = jnp.where(tp == 0,
                       jnp.concatenate([halo_prev, cvprev[:set_rows - batch]], axis=0), cvprev)
    cvnext = jnp.where(tp == CHUNK // SETS_PER_STEP - 1,
                       jnp.concatenate([cvnext[batch:], halo_next], axis=0), cvnext)
    cvs = [cv_ref[:, k * D_MODEL:(k + 1) * D_MODEL].astype(F32) for k in range(SETS_PER_STEP)]
    cvm = jnp.concatenate([cvprev] + cvs[:-1], axis=0)
    cvp = jnp.concatenate(cvs[1:] + [cvnext], axis=0)
    cw = cw_ref[...]
    conv = cb_ref[...] + cvm * cw[0:1] + jnp.concatenate(cvs, axis=0) * cw[1:2] + cvp * cw[2:3]

    z_b = proj(pzb_ref)
    y_b = (proj(pb_ref) * conv * (z_b * _sigmoid(z_b))).astype(BF16)
    merged = _sigmoid(proj(pgb_ref)) * jnp.dot(y_b, wb_ref[...], preferred_element_type=F32)

    yg = _gelu_tanh(jnp.concatenate([yt_ref[k].T for k in range(SETS_PER_STEP)], axis=0))
    yg = yg * _sigmoid(jnp.dot(yg.astype(BF16), wglu_ref[...], preferred_element_type=F32))
    z_a = proj(pza_ref)
    y_a = (yg * (z_a * _sigmoid(z_a))).astype(BF16)
    merged = merged + _sigmoid(proj(pga_ref)) * jnp.dot(y_a, wa_ref[...], preferred_element_type=F32)

    h = x + jnp.dot(merged.astype(BF16), wo_ref[...], preferred_element_type=F32)
    out = _rmsnorm(h, fg_ref[...])

    slot = step % 2

    @pl.when(step >= 2)
    def _():
        for c in _set_copies(o_hbm, obuf, osem, step - 2, slot, SETS_PER_STEP, to_hbm=True):
            c.wait()

    for k in range(SETS_PER_STEP):
        obuf[slot, k] = out[k * set_rows:(k + 1) * set_rows].reshape(CHUNK_BLOCK, batch, D_MODEL)
    for c in _set_copies(o_hbm, obuf, osem, step, slot, SETS_PER_STEP, to_hbm=True):
        c.start()

    @pl.when(step == n_steps - 1)
    def _():
        @pl.when(step >= 1)
        def _():
            for c in _set_copies(o_hbm, obuf, osem, step - 1, 1 - slot, SETS_PER_STEP, to_hbm=True):
                c.wait()
        for c in _set_copies(o_hbm, obuf, osem, step, slot, SETS_PER_STEP, to_hbm=True):
            c.wait()


def _block_t(x4, yt, cv3, norm_g, w_in, w_glu, w_a, w_b, w_o, cw, cb, final_g, *, bsz, n_chunks):
    s = SETS_PER_STEP
    n_blocks = n_chunks // CHUNK_BLOCK
    set_rows = CHUNK_BLOCK * bsz
    halo = 2 * bsz
    tok_spec = pl.BlockSpec((None, set_rows, s * D_MODEL), lambda cb, tp: (cb, 0, tp))
    prev_spec = pl.BlockSpec((None, set_rows, D_MODEL),
                             lambda cb, tp: (cb, 0, (tp * s + CHUNK - 1) % CHUNK))
    next_spec = pl.BlockSpec((None, set_rows, D_MODEL), lambda cb, tp: (cb, 0, (tp * s + s) % CHUNK))
    haloprev_spec = pl.BlockSpec((None, halo, D_MODEL),
                                 lambda cb, tp: (jnp.maximum(cb - 1, 0), set_rows // halo - 1, CHUNK - 1))
    halonext_spec = pl.BlockSpec((None, halo, D_MODEL),
                                 lambda cb, tp: (jnp.minimum(cb + 1, n_blocks - 1), 0, 0))
    kern = functools.partial(_block_kernel, set_rows=set_rows, batch=bsz)
    return pl.pallas_call(
        kern,
        grid=(n_blocks, CHUNK // s),
        in_specs=[
            pl.BlockSpec(memory_space=pl.ANY),
            pl.BlockSpec((s, D_MODEL, set_rows), lambda cb, tp: (tp, 0, cb)),
            tok_spec, prev_spec, next_spec, haloprev_spec, halonext_spec,
            _const_spec((1, D_MODEL)),
            _col_spec(COL_ZA), _col_spec(COL_B), _col_spec(COL_ZB), _col_spec(COL_GA), _col_spec(COL_GB),
            _const_spec((D_MODEL, D_MODEL)),
            _const_spec((D_MODEL, D_MODEL)),
            _const_spec((D_MODEL, D_MODEL)),
            _const_spec((D_MODEL, D_MODEL)),
            _const_spec((3, D_MODEL)),
            _const_spec((1, D_MODEL)),
            _const_spec((1, D_MODEL)),
        ],
        out_specs=pl.BlockSpec(memory_space=pl.ANY),
        out_shape=jax.ShapeDtypeStruct(x4.shape, F32),
        scratch_shapes=([pltpu.VMEM((s * set_rows, D_MODEL), BF16)]
                        + _set_buffers(bsz, s) + _set_buffers(bsz, s)),
        compiler_params=_params(2),
        name="block_t",
    )(x4, yt, cv3, cv3, cv3, cv3, cv3, norm_g, w_in, w_in, w_in, w_in, w_in, w_glu, w_a, w_b, w_o, cw, cb,
      final_g)


def kernel(x, norm_g, w_in, lam_re, lam_im, log_dt, ssm_b_re, ssm_b_im, ssm_c_re, ssm_c_im, ssm_d,
           w_glu, conv_w, conv_b, w_branch_a, w_branch_b, w_out, final_g):
    bsz, seq_len, d = x.shape
    assert norm_g.shape[0] == 1 and d == D_MODEL and bsz == SUBLANES
    n_chunks = seq_len // CHUNK
    assert seq_len == n_chunks * CHUNK and n_chunks % CHUNK_BLOCK == 0
    assert (CHUNK_BLOCK * bsz) % LANES == 0

    x4 = x.reshape(bsz, n_chunks, CHUNK, d)
    ng = norm_g[0].reshape(1, d)
    mt, wst, woutt, aq, w_bf, wut, wglu_bf, wa_bf, wb_bf, wo_bf = _prep(
        lam_re[0], lam_im[0], log_dt[0], ssm_b_re[0], ssm_b_im[0], ssm_c_re[0], ssm_c_im[0],
        ssm_d[0], w_in[0], w_glu[0], w_branch_a[0], w_branch_b[0], w_out[0])

    ut, cv3 = _in_proj_t(x4, ng, wut, w_bf, bsz=bsz, n_chunks=n_chunks)
    yt = _s5_t(ut, mt, wst, woutt, aq, n_chunks=n_chunks, batch=bsz)

    out4 = _block_t(
        x4, yt, cv3, ng, w_bf, wglu_bf, wa_bf, wb_bf, wo_bf, conv_w[0], conv_b[0].reshape(1, d),
        final_g.reshape(1, d), bsz=bsz, n_chunks=n_chunks)
    return out4.reshape(bsz, seq_len, d)
```

```python
import functools
import math

import jax
import jax.numpy as jnp
from jax import lax
from jax.experimental import pallas as pl
from jax.experimental.pallas import tpu as pltpu

F32 = jnp.float32
BF16 = jnp.bfloat16

D_MODEL = 1024
SSM_GROUP = 16
N_GROUPS = D_MODEL // SSM_GROUP
STATE = 64
CHUNK = 16
CW = CHUNK * SSM_GROUP
CHUNK_BLOCK = 32
SW = 4 * STATE
HALF = 2 * STATE
LANES = 128
EPS = 1e-6
SUBLANES = 8
IN_SETS_PER_STEP = 4
SETS_PER_STEP = 4
SUB_SETS = 2
GROUPS_PER_STEP = 4
OPS_GROUPS_PER_STEP = 8
VMEM_LIMIT_BYTES = 60 * 1024 * 1024

NT_DIMS = (((1,), (1,)), ((), ()))
TN_DIMS = (((0,), (0,)), ((), ()))


def _rmsnorm(x, g):
    ms = jnp.mean(x * x, axis=-1, keepdims=True)
    return x * lax.rsqrt(ms + EPS) * g


def _sigmoid(x):
    return 0.5 * jnp.tanh(0.5 * x) + 0.5


def _gelu_tanh(x):
    c = math.sqrt(2.0 / math.pi)
    return x * (0.5 * (1.0 + jnp.tanh(c * (x + 0.044715 * (x * x * x)))))


def _params(n_axes):
    return pltpu.CompilerParams(
        dimension_semantics=("arbitrary",) * n_axes, vmem_limit_bytes=VMEM_LIMIT_BYTES)


def _const_spec(shape):
    return pl.BlockSpec(shape, lambda *_: (0,) * len(shape), pipeline_mode=pl.Buffered(1))


COL_U, COL_ZA, COL_V, COL_B, COL_C, COL_ZB, COL_GA, COL_GB = range(8)


def _col_spec(col):
    return pl.BlockSpec((D_MODEL, D_MODEL), lambda *_: (0, col), pipeline_mode=pl.Buffered(1))


def _set_copies(x_hbm, buf, sem, step, slot, sets, *, to_hbm=False):
    steps_per_block = CHUNK // sets
    c0 = (step // steps_per_block) * CHUNK_BLOCK
    tl0 = (step % steps_per_block) * sets
    copies = []
    for k in range(sets):
        for b in range(x_hbm.shape[0]):
            hbm = x_hbm.at[b, pl.ds(c0, CHUNK_BLOCK), tl0 + k, :]
            vmem = buf.at[slot, k, :, b, :]
            src, dst = (vmem, hbm) if to_hbm else (hbm, vmem)
            copies.append(pltpu.make_async_copy(src, dst, sem.at[slot, k]))
    return copies


def _fetch_sets(x_hbm, xbuf, sem, step, n_steps, sets):
    slot = step % 2

    @pl.when(step == 0)
    def _():
        for c in _set_copies(x_hbm, xbuf, sem, step, slot, sets):
            c.start()

    @pl.when(step + 1 < n_steps)
    def _():
        for c in _set_copies(x_hbm, xbuf, sem, step + 1, 1 - slot, sets):
            c.start()

    for c in _set_copies(x_hbm, xbuf, sem, step, slot, sets):
        c.wait()
    rows = xbuf.shape[2] * xbuf.shape[3]
    return jnp.concatenate([xbuf[slot, k].reshape(rows, D_MODEL) for k in range(sets)], axis=0)


def _set_buffers(batch, sets):
    return [pltpu.VMEM((2, sets, CHUNK_BLOCK, batch, D_MODEL), F32),
            pltpu.SemaphoreType.DMA((2, sets))]


def _in_proj_kernel(x_hbm, g_ref, wut_ref, wv_ref, wc_ref, u_ref, cv_ref, xbuf, xsem, *, set_rows):
    step = pl.program_id(0) * pl.num_programs(1) + pl.program_id(1)
    n_steps = pl.num_programs(0) * pl.num_programs(1)
    x = _fetch_sets(x_hbm, xbuf, xsem, step, n_steps, IN_SETS_PER_STEP)
    xn = _rmsnorm(x, g_ref[...]).astype(BF16)
    ut = lax.dot_general(wut_ref[...], xn, NT_DIMS, preferred_element_type=F32)
    cv = (jnp.dot(xn, wv_ref[...], preferred_element_type=F32)
          * jnp.dot(xn, wc_ref[...], preferred_element_type=F32)).astype(BF16)
    for k in range(IN_SETS_PER_STEP):
        u_ref[k] = ut[:, k * set_rows:(k + 1) * set_rows].astype(BF16)
        cv_ref[:, k * D_MODEL:(k + 1) * D_MODEL] = cv[k * set_rows:(k + 1) * set_rows]


def _in_proj_t(x4, norm_g, wut, w_in, *, bsz, n_chunks):
    s = IN_SETS_PER_STEP
    n_blocks = n_chunks // CHUNK_BLOCK
    set_rows = CHUNK_BLOCK * bsz
    tok_spec = pl.BlockSpec((None, set_rows, s * D_MODEL), lambda cb, tp: (cb, 0, tp))
    return pl.pallas_call(
        functools.partial(_in_proj_kernel, set_rows=set_rows),
        grid=(n_blocks, CHUNK // s),
        in_specs=[pl.BlockSpec(memory_space=pl.ANY), _const_spec((1, D_MODEL)),
                  _const_spec((D_MODEL, D_MODEL)), _col_spec(COL_V), _col_spec(COL_C)],
        out_specs=[pl.BlockSpec((s, D_MODEL, set_rows), lambda cb, tp: (tp, 0, cb)), tok_spec],
        out_shape=[jax.ShapeDtypeStruct((CHUNK, D_MODEL, bsz * n_chunks), BF16),
                   jax.ShapeDtypeStruct((n_blocks, set_rows, CHUNK * D_MODEL), BF16)],
        scratch_shapes=_set_buffers(bsz, s),
        compiler_params=_params(2),
        name="in_proj_t",
    )(x4, norm_g, wut, w_in, w_in)


def _s5_kernel(u_ref, mt_ref, wst_ref, woutt_ref, aq_ref, y_ref, sloc_ref, pf_ref, pb_ref,
               *, n_chunks, batch):
    n = batch * n_chunks
    groups = range(GROUPS_PER_STEP)
    uts = [u_ref[:, j * SSM_GROUP:(j + 1) * SSM_GROUP, :].reshape(CW, n) for j in groups]

    for j in groups:
        sl = lax.dot_general(uts[j], wst_ref[j], TN_DIMS, preferred_element_type=F32)
        sloc_ref[j, 0] = sl[:, :HALF]
        sloc_ref[j, 1] = sl[:, HALF:]

    is_fwd = lax.broadcasted_iota(jnp.int32, (batch, HALF), 1) < STATE
    ar = [jnp.broadcast_to(aq_ref[j, 0:1, :], (batch, HALF)) for j in groups]
    ai = [jnp.broadcast_to(aq_ref[j, 1:2, :], (batch, HALF)) for j in groups]

    zero = jnp.zeros((batch, HALF), F32)
    last = n_chunks * batch
    for j in groups:
        for h in range(2):
            pf_ref[j, h, 0:batch, :] = zero
            pb_ref[j, h, last:last + batch, :] = zero

    def scan_step(k, carry):
        rf = k * batch
        rb = (n_chunks - 1 - k) * batch
        ro = (k + 1) * batch
        new = []
        for j in groups:
            sr, si = carry[j]
            xr = jnp.where(is_fwd, sloc_ref[j, 0, pl.ds(rf, batch), :],
                           sloc_ref[j, 0, pl.ds(rb, batch), :])
            xi = jnp.where(is_fwd, sloc_ref[j, 1, pl.ds(rf, batch), :],
                           sloc_ref[j, 1, pl.ds(rb, batch), :])
            nr = ar[j] * sr - ai[j] * si + xr
            ni = ar[j] * si + ai[j] * sr + xi
            pf_ref[j, 0, pl.ds(ro, batch), :] = nr
            pf_ref[j, 1, pl.ds(ro, batch), :] = ni
            pb_ref[j, 0, pl.ds(rb, batch), :] = nr
            pb_ref[j, 1, pl.ds(rb, batch), :] = ni
            new.append((nr, ni))
        return tuple(new)

    carry = tuple((zero, zero) for _ in groups)
    for k in range(n_chunks):
        carry = scan_step(k, carry)

    fwd_rows = lax.broadcasted_iota(jnp.int32, (n, HALF), 1) < STATE
    for j in groups:
        prev = jnp.concatenate(
            [jnp.where(fwd_rows, pf_ref[j, h, 0:n, :], pb_ref[j, h, batch:batch + n, :])
             for h in range(2)], axis=1)
        y = jnp.dot(mt_ref[j], uts[j], preferred_element_type=F32)
        y = y + lax.dot_general(woutt_ref[j], prev.astype(BF16), NT_DIMS, preferred_element_type=F32)
        y_ref[:, j * SSM_GROUP:(j + 1) * SSM_GROUP, :] = y.reshape(CHUNK, SSM_GROUP, n)


def _s5_t(ut, mt, wst, woutt, aq, *, n_chunks, batch):
    n = batch * n_chunks
    gps = GROUPS_PER_STEP
    kern = functools.partial(_s5_kernel, n_chunks=n_chunks, batch=batch)
    op_spec = pl.BlockSpec((gps, CW, CW), lambda g: (g, 0, 0))
    act_spec = pl.BlockSpec((CHUNK, gps * SSM_GROUP, n), lambda g: (0, g, 0))
    state_rows = pltpu.VMEM((gps, 2, n + batch, HALF), F32)
    return pl.pallas_call(
        kern,
        grid=(N_GROUPS // gps,),
        in_specs=[act_spec, op_spec, op_spec, op_spec,
                  pl.BlockSpec((gps, 2, HALF), lambda g: (g, 0, 0))],
        out_specs=act_spec,
        out_shape=jax.ShapeDtypeStruct((CHUNK, D_MODEL, n), F32),
        scratch_shapes=[pltpu.VMEM((gps, 2, n, HALF), F32), state_rows, state_rows],
        compiler_params=_params(1),
        name="s5_t",
    )(ut, mt, wst, woutt, aq)


def _prep_kernel(lam_ref, c_ref, bt_ref, dd_ref, win_ref, w1_ref, w2_ref, w3_ref, w4_ref,
                 mt_ref, wst_ref, woutt_ref, aq_ref, winbf_ref, wut_ref, w1bf_ref, w2bf_ref, w3bf_ref,
                 w4bf_ref):
    for g in range(OPS_GROUPS_PER_STEP):
        _s5_ops_group(g, lam_ref, c_ref, bt_ref, dd_ref, mt_ref, wst_ref, woutt_ref, aq_ref)

    winbf_ref[...] = win_ref[...].astype(BF16)

    @pl.when(pl.program_id(0) == COL_U)
    def _():
        wut_ref[...] = win_ref[...].T.astype(BF16)

    for src, dst in ((w1_ref, w1bf_ref), (w2_ref, w2bf_ref), (w3_ref, w3bf_ref), (w4_ref, w4bf_ref)):
        dst[...] = src[...].astype(BF16)


def _s5_ops_group(g, lam_ref, c_ref, bt_ref, dd_ref, mt_ref, wst_ref, woutt_ref, aq_ref):
    lr, li = lam_ref[g, 0:1, :], lam_ref[g, 1:2, :]
    dt = jnp.exp(lam_ref[g, 2:3, :])
    c_re, c_im = c_ref[g, 0], c_ref[g, 1]
    bt_re, bt_im = bt_ref[g, 0], bt_ref[g, 1]

    mag = jnp.exp(lr * dt)
    a_re, a_im = mag * jnp.cos(li * dt), mag * jnp.sin(li * dt)
    pw = [(jnp.ones_like(a_re), jnp.zeros_like(a_im))]
    for _ in range(CHUNK):
        p_re, p_im = pw[-1]
        pw.append((p_re * a_re - p_im * a_im, p_re * a_im + p_im * a_re))
    fwd_row = lax.broadcasted_iota(jnp.int32, (1, HALF), 1) < STATE

    def power(e_fwd, e_bwd):
        return (jnp.where(fwd_row, pw[e_fwd][0], pw[e_bwd][0]),
                jnp.where(fwd_row, pw[e_fwd][1], pw[e_bwd][1]))

    den = lr * lr + li * li
    nr, ni = a_re - 1.0, a_im
    coef_re = (nr * lr + ni * li) / den
    coef_im = (ni * lr - nr * li) / den
    bb_re = coef_re * bt_re - coef_im * bt_im
    bb_im = coef_re * bt_im + coef_im * bt_re

    st_re, st_im = [], []
    for k in range(CHUNK):
        p_re, p_im = power(CHUNK - 1 - k, k)
        st_re.append(p_re * bb_re - p_im * bb_im)
        st_im.append(p_re * bb_im + p_im * bb_re)
    wst = jnp.concatenate([jnp.concatenate(st_re, axis=0), jnp.concatenate(st_im, axis=0)], axis=1)
    wst_ref[g] = wst.astype(BF16)

    out_rows = []
    for k in range(CHUNK):
        p_re, p_im = power(k + 1, CHUNK - k)
        out_rows.append(jnp.concatenate([c_re * p_re - c_im * p_im, -(c_re * p_im + c_im * p_re)],
                                        axis=1))
    woutt_ref[g] = jnp.concatenate(out_rows, axis=0).astype(BF16)

    aq_ref[g, 0:1, :], aq_ref[g, 1:2, :] = pw[CHUNK]

    fwd1 = lax.broadcasted_iota(jnp.int32, (SSM_GROUP, HALF), 1) < STATE
    zero = jnp.zeros_like(c_re)
    cmat = jnp.concatenate([
        jnp.concatenate([jnp.where(fwd1, c_re, zero), -jnp.where(fwd1, c_im, zero)], axis=1),
        jnp.concatenate([jnp.where(fwd1, zero, c_re), -jnp.where(fwd1, zero, c_im)], axis=1)], axis=0)
    kslab = lax.dot_general(cmat, wst, NT_DIMS, precision=lax.Precision.HIGHEST,
                            preferred_element_type=F32)
    k_f = kslab[:SSM_GROUP]
    k_b = kslab[SSM_GROUP:] + dd_ref[g]

    lane = lax.broadcasted_iota(jnp.int32, (SSM_GROUP, CW), 1)
    m_rows = []
    for t in range(CHUNK):
        left = SSM_GROUP * (CHUNK - 1 - t)
        right = SSM_GROUP * t
        a = k_f if left == 0 else jnp.where(lane < CW - left, pltpu.roll(k_f, CW - left, 1), 0.0)
        b = k_b if right == 0 else jnp.where(lane >= right, pltpu.roll(k_b, right, 1), 0.0)
        m_rows.append(a + b)
    mt_ref[g] = jnp.concatenate(m_rows, axis=0).astype(BF16)


def _prep(lam_re, lam_im, log_dt, b_re, b_im, c_re, c_im, ssm_d, w_in, w_glu, w_a, w_b, w_o):
    def lanes(x):
        return jnp.concatenate([x[0], x[1]], axis=-1)

    lam = jnp.stack([lanes(lam_re), lanes(lam_im),
                     lanes(jnp.broadcast_to(log_dt[..., None], lam_re.shape))], axis=1)
    c = jnp.stack([lanes(c_re), lanes(c_im)], axis=1)
    bt = jnp.stack([lanes(b_re.transpose(0, 1, 3, 2)), lanes(b_im.transpose(0, 1, 3, 2))], axis=1)
    dd = ssm_d.reshape(N_GROUPS, SSM_GROUP)[:, :, None] * jnp.eye(SSM_GROUP, CW, dtype=F32)

    gps = OPS_GROUPS_PER_STEP
    n_steps = N_GROUPS // gps
    assert n_steps == w_in.shape[1] // D_MODEL
    op_spec = pl.BlockSpec((gps, CW, CW), lambda g: (g, 0, 0))
    par_spec = pl.BlockSpec((gps, 2, SSM_GROUP, HALF), lambda g: (g, 0, 0, 0))
    op_shape = jax.ShapeDtypeStruct((N_GROUPS, CW, CW), BF16)
    sq = (D_MODEL, D_MODEL)
    sq_shape = jax.ShapeDtypeStruct(sq, BF16)
    col_spec = pl.BlockSpec(sq, lambda g: (0, g))
    row_spec = pl.BlockSpec((D_MODEL // n_steps, D_MODEL), lambda g: (g, 0))
    return pl.pallas_call(
        _prep_kernel,
        grid=(n_steps,),
        in_specs=[pl.BlockSpec((gps, 3, HALF), lambda g: (g, 0, 0)), par_spec, par_spec,
                  pl.BlockSpec((gps, SSM_GROUP, CW), lambda g: (g, 0, 0)),
                  col_spec, row_spec, row_spec, row_spec, row_spec],
        out_specs=[op_spec, op_spec, op_spec, pl.BlockSpec((gps, 2, HALF), lambda g: (g, 0, 0)),
                   col_spec, pl.BlockSpec(sq, lambda g: (0, 0)), row_spec, row_spec, row_spec, row_spec],
        out_shape=[op_shape, op_shape, op_shape, jax.ShapeDtypeStruct((N_GROUPS, 2, HALF), F32),
                   jax.ShapeDtypeStruct(w_in.shape, BF16), sq_shape, sq_shape, sq_shape, sq_shape,
                   sq_shape],
        compiler_params=_params(1),
        name="prep",
    )(lam, c, bt, dd, w_in, w_glu, w_a, w_b, w_o)


def _block_kernel(x_hbm, yt_ref, cv_ref, cvprev_ref, cvnext_ref, haloprev_ref, halonext_ref, ng_ref,
                  pza_ref, pb_ref, pzb_ref, pga_ref, pgb_ref, wglu_ref, wa_ref, wb_ref, wo_ref,
                  cw_ref, cb_ref, fg_ref, o_hbm, xbuf, xsem, obuf, osem, *, set_rows, batch):
    blk = pl.program_id(0)
    tp = pl.program_id(1)
    step = pl.program_id(0) * pl.num_programs(1) + tp
    n_steps = pl.num_programs(0) * pl.num_programs(1)
    _fetch_sets(x_hbm, xbuf, xsem, step, n_steps, SETS_PER_STEP)
    slot = step % 2

    @pl.when(step >= 2)
    def _():
        for c in _set_copies(o_hbm, obuf, osem, step - 2, slot, SETS_PER_STEP, to_hbm=True):
            c.wait()

    cvprev = cvprev_ref[...].astype(F32)
    cvnext = cvnext_ref[...].astype(F32)
    halo_prev = jnp.where(blk == 0, 0.0, haloprev_ref[...].astype(F32)[batch:])
    halo_next = jnp.where(blk == pl.num_programs(0) - 1, 0.0, halonext_ref[...].astype(F32)[:batch])
    cvprev = jnp.where(tp == 0,
                       jnp.concatenate([halo_prev, cvprev[:set_rows - batch]], axis=0), cvprev)
    cvnext = jnp.where(tp == CHUNK // SETS_PER_STEP - 1,
                       jnp.concatenate([cvnext[batch:], halo_next], axis=0), cvnext)
    def cv_set(k):
        if k < 0:
            return cvprev
        if k == SETS_PER_STEP:
            return cvnext
        return cv_ref[:, k * D_MODEL:(k + 1) * D_MODEL].astype(F32)

    cw = cw_ref[...]

    for sub in range(SETS_PER_STEP // SUB_SETS):
        ks = range(sub * SUB_SETS, (sub + 1) * SUB_SETS)
        x = jnp.concatenate([xbuf[slot, k].reshape(set_rows, D_MODEL) for k in ks], axis=0)
        xn = _rmsnorm(x, ng_ref[...]).astype(BF16)

        def proj(w_ref, xn=xn):
            return jnp.dot(xn, w_ref[...], preferred_element_type=F32)

        cvm, cvc, cvp = (jnp.concatenate([cv_set(k + d) for k in ks], axis=0) for d in (-1, 0, 1))
        conv = cb_ref[...] + cvm * cw[0:1] + cvc * cw[1:2] + cvp * cw[2:3]

        z_b = proj(pzb_ref)
        y_b = (proj(pb_ref) * conv * (z_b * _sigmoid(z_b))).astype(BF16)
        merged = _sigmoid(proj(pgb_ref)) * jnp.dot(y_b, wb_ref[...], preferred_element_type=F32)

        yg = _gelu_tanh(jnp.concatenate([yt_ref[k].T for k in ks], axis=0))
        yg = yg * _sigmoid(jnp.dot(yg.astype(BF16), wglu_ref[...], preferred_element_type=F32))
        z_a = proj(pza_ref)
        y_a = (yg * (z_a * _sigmoid(z_a))).astype(BF16)
        merged = merged + _sigmoid(proj(pga_ref)) * jnp.dot(y_a, wa_ref[...],
                                                             preferred_element_type=F32)

        h = x + jnp.dot(merged.astype(BF16), wo_ref[...], preferred_element_type=F32)
        out = _rmsnorm(h, fg_ref[...])
        for i, k in enumerate(ks):
            obuf[slot, k] = out[i * set_rows:(i + 1) * set_rows].reshape(CHUNK_BLOCK, batch, D_MODEL)

    for c in _set_copies(o_hbm, obuf, osem, step, slot, SETS_PER_STEP, to_hbm=True):
        c.start()

    @pl.when(step == n_steps - 1)
    def _():
        @pl.when(step >= 1)
        def _():
            for c in _set_copies(o_hbm, obuf, osem, step - 1, 1 - slot, SETS_PER_STEP, to_hbm=True):
                c.wait()
        for c in _set_copies(o_hbm, obuf, osem, step, slot, SETS_PER_STEP, to_hbm=True):
            c.wait()


def _block_t(x4, yt, cv3, norm_g, w_in, w_glu, w_a, w_b, w_o, cw, cb, final_g, *, bsz, n_chunks):
    s = SETS_PER_STEP
    n_blocks = n_chunks // CHUNK_BLOCK
    set_rows = CHUNK_BLOCK * bsz
    halo = 2 * bsz
    tok_spec = pl.BlockSpec((None, set_rows, s * D_MODEL), lambda cb, tp: (cb, 0, tp))
    prev_spec = pl.BlockSpec((None, set_rows, D_MODEL),
                             lambda cb, tp: (cb, 0, (tp * s + CHUNK - 1) % CHUNK))
    next_spec = pl.BlockSpec((None, set_rows, D_MODEL), lambda cb, tp: (cb, 0, (tp * s + s) % CHUNK))
    haloprev_spec = pl.BlockSpec((None, halo, D_MODEL),
                                 lambda cb, tp: (jnp.maximum(cb - 1, 0), set_rows // halo - 1, CHUNK - 1))
    halonext_spec = pl.BlockSpec((None, halo, D_MODEL),
                                 lambda cb, tp: (jnp.minimum(cb + 1, n_blocks - 1), 0, 0))
    kern = functools.partial(_block_kernel, set_rows=set_rows, batch=bsz)
    return pl.pallas_call(
        kern,
        grid=(n_blocks, CHUNK // s),
        in_specs=[
            pl.BlockSpec(memory_space=pl.ANY),
            pl.BlockSpec((s, D_MODEL, set_rows), lambda cb, tp: (tp, 0, cb)),
            tok_spec, prev_spec, next_spec, haloprev_spec, halonext_spec,
            _const_spec((1, D_MODEL)),
            _col_spec(COL_ZA), _col_spec(COL_B), _col_spec(COL_ZB), _col_spec(COL_GA), _col_spec(COL_GB),
            _const_spec((D_MODEL, D_MODEL)),
            _const_spec((D_MODEL, D_MODEL)),
            _const_spec((D_MODEL, D_MODEL)),
            _const_spec((D_MODEL, D_MODEL)),
            _const_spec((3, D_MODEL)),
            _const_spec((1, D_MODEL)),
            _const_spec((1, D_MODEL)),
        ],
        out_specs=pl.BlockSpec(memory_space=pl.ANY),
        out_shape=jax.ShapeDtypeStruct(x4.shape, F32),
        scratch_shapes=_set_buffers(bsz, s) + _set_buffers(bsz, s),
        compiler_params=_params(2),
        name="block_t",
    )(x4, yt, cv3, cv3, cv3, cv3, cv3, norm_g, w_in, w_in, w_in, w_in, w_in, w_glu, w_a, w_b, w_o, cw, cb,
      final_g)


def kernel(x, norm_g, w_in, lam_re, lam_im, log_dt, ssm_b_re, ssm_b_im, ssm_c_re, ssm_c_im, ssm_d,
           w_glu, conv_w, conv_b, w_branch_a, w_branch_b, w_out, final_g):
    bsz, seq_len, d = x.shape
    assert norm_g.shape[0] == 1 and d == D_MODEL and bsz == SUBLANES
    n_chunks = seq_len // CHUNK
    assert seq_len == n_chunks * CHUNK and n_chunks % CHUNK_BLOCK == 0
    assert (CHUNK_BLOCK * bsz) % LANES == 0

    x4 = x.reshape(bsz, n_chunks, CHUNK, d)
    ng = norm_g[0].reshape(1, d)
    mt, wst, woutt, aq, w_bf, wut, wglu_bf, wa_bf, wb_bf, wo_bf = _prep(
        lam_re[0], lam_im[0], log_dt[0], ssm_b_re[0], ssm_b_im[0], ssm_c_re[0], ssm_c_im[0],
        ssm_d[0], w_in[0], w_glu[0], w_branch_a[0], w_branch_b[0], w_out[0])

    ut, cv3 = _in_proj_t(x4, ng, wut, w_bf, bsz=bsz, n_chunks=n_chunks)
    yt = _s5_t(ut, mt, wst, woutt, aq, n_chunks=n_chunks, batch=bsz)

    out4 = _block_t(
        x4, yt, cv3, ng, w_bf, wglu_bf, wa_bf, wb_bf, wo_bf, conv_w[0], conv_b[0].reshape(1, d),
        final_g.reshape(1, d), bsz=bsz, n_chunks=n_chunks)
    return out4.reshape(bsz, seq_len, d)
```

```python
import functools
import math

import jax
import jax.numpy as jnp
from jax import lax
from jax.experimental import pallas as pl
from jax.experimental.pallas import tpu as pltpu

F32 = jnp.float32
BF16 = jnp.bfloat16

D_MODEL = 1024
SSM_GROUP = 16
N_GROUPS = D_MODEL // SSM_GROUP
STATE = 64
CHUNK = 16
CW = CHUNK * SSM_GROUP
CHUNK_BLOCK = 32
SW = 4 * STATE
HALF = 2 * STATE
LANES = 128
EPS = 1e-6
SUBLANES = 8
IN_SETS_PER_STEP = 4
SETS_PER_STEP = 4
SUB_SETS = 2
GROUPS_PER_STEP = 4
OPS_GROUPS_PER_STEP = 8
VMEM_LIMIT_BYTES = 60 * 1024 * 1024

NT_DIMS = (((1,), (1,)), ((), ()))
TN_DIMS = (((0,), (0,)), ((), ()))


def _rmsnorm(x, g):
    ms = jnp.mean(x * x, axis=-1, keepdims=True)
    return x * lax.rsqrt(ms + EPS) * g


def _sigmoid(x):
    return 0.5 * jnp.tanh(0.5 * x) + 0.5


def _gelu_tanh(x):
    c = math.sqrt(2.0 / math.pi)
    return x * (0.5 * (1.0 + jnp.tanh(c * (x + 0.044715 * (x * x * x)))))


def _params(n_axes):
    return pltpu.CompilerParams(
        dimension_semantics=("arbitrary",) * n_axes, vmem_limit_bytes=VMEM_LIMIT_BYTES)


def _const_spec(shape):
    return pl.BlockSpec(shape, lambda *_: (0,) * len(shape), pipeline_mode=pl.Buffered(1))


COL_U, COL_ZA, COL_V, COL_B, COL_C, COL_ZB, COL_GA, COL_GB = range(8)


def _col_spec(col):
    return pl.BlockSpec((D_MODEL, D_MODEL), lambda *_: (0, col), pipeline_mode=pl.Buffered(1))


def _set_copies(x_hbm, buf, sem, step, slot, sets, *, to_hbm=False):
    steps_per_block = CHUNK // sets
    c0 = (step // steps_per_block) * CHUNK_BLOCK
    tl0 = (step % steps_per_block) * sets
    copies = []
    for k in range(sets):
        for b in range(x_hbm.shape[0]):
            hbm = x_hbm.at[b, pl.ds(c0, CHUNK_BLOCK), tl0 + k, :]
            vmem = buf.at[slot, k, :, b, :]
            src, dst = (vmem, hbm) if to_hbm else (hbm, vmem)
            copies.append(pltpu.make_async_copy(src, dst, sem.at[slot, k]))
    return copies


def _fetch_sets(x_hbm, xbuf, sem, step, n_steps, sets):
    slot = step % 2

    @pl.when(step == 0)
    def _():
        for c in _set_copies(x_hbm, xbuf, sem, step, slot, sets):
            c.start()

    @pl.when(step + 1 < n_steps)
    def _():
        for c in _set_copies(x_hbm, xbuf, sem, step + 1, 1 - slot, sets):
            c.start()

    for c in _set_copies(x_hbm, xbuf, sem, step, slot, sets):
        c.wait()
    rows = xbuf.shape[2] * xbuf.shape[3]
    return jnp.concatenate([xbuf[slot, k].reshape(rows, D_MODEL) for k in range(sets)], axis=0)


def _set_buffers(batch, sets):
    return [pltpu.VMEM((2, sets, CHUNK_BLOCK, batch, D_MODEL), F32),
            pltpu.SemaphoreType.DMA((2, sets))]


def _in_proj_kernel(x_hbm, g_ref, wut_ref, wv_ref, wc_ref, u_ref, cv_ref, xbuf, xsem, *, set_rows):
    step = pl.program_id(0) * pl.num_programs(1) + pl.program_id(1)
    n_steps = pl.num_programs(0) * pl.num_programs(1)
    x = _fetch_sets(x_hbm, xbuf, xsem, step, n_steps, IN_SETS_PER_STEP)
    xn = _rmsnorm(x, g_ref[...]).astype(BF16)
    ut = lax.dot_general(wut_ref[...], xn, NT_DIMS, preferred_element_type=F32)
    cv = (jnp.dot(xn, wv_ref[...], preferred_element_type=F32)
          * jnp.dot(xn, wc_ref[...], preferred_element_type=F32)).astype(BF16)
    for k in range(IN_SETS_PER_STEP):
        u_ref[k] = ut[:, k * set_rows:(k + 1) * set_rows].astype(BF16)
        cv_ref[:, k * D_MODEL:(k + 1) * D_MODEL] = cv[k * set_rows:(k + 1) * set_rows]


def _in_proj_t(x4, norm_g, wut, w_in, *, bsz, n_chunks):
    s = IN_SETS_PER_STEP
    n_blocks = n_chunks // CHUNK_BLOCK
    set_rows = CHUNK_BLOCK * bsz
    tok_spec = pl.BlockSpec((None, set_rows, s * D_MODEL), lambda cb, tp: (cb, 0, tp))
    return pl.pallas_call(
        functools.partial(_in_proj_kernel, set_rows=set_rows),
        grid=(n_blocks, CHUNK // s),
        in_specs=[pl.BlockSpec(memory_space=pl.ANY), _const_spec((1, D_MODEL)),
                  _const_spec((D_MODEL, D_MODEL)), _col_spec(COL_V), _col_spec(COL_C)],
        out_specs=[pl.BlockSpec((s, D_MODEL, set_rows), lambda cb, tp: (tp, 0, cb)), tok_spec],
        out_shape=[jax.ShapeDtypeStruct((CHUNK, D_MODEL, bsz * n_chunks), BF16),
                   jax.ShapeDtypeStruct((n_blocks, set_rows, CHUNK * D_MODEL), BF16)],
        scratch_shapes=_set_buffers(bsz, s),
        compiler_params=_params(2),
        name="in_proj_t",
    )(x4, norm_g, wut, w_in, w_in)


def _s5_kernel(u_ref, mt_ref, wst_ref, woutt_ref, aq_ref, y_ref, sloc_ref, pf_ref, pb_ref,
               *, n_chunks, batch):
    n = batch * n_chunks
    groups = range(GROUPS_PER_STEP)
    uts = [u_ref[:, j * SSM_GROUP:(j + 1) * SSM_GROUP, :].reshape(CW, n) for j in groups]

    for j in groups:
        sl = lax.dot_general(uts[j], wst_ref[j], TN_DIMS, preferred_element_type=F32)
        sloc_ref[j, 0] = sl[:, :HALF]
        sloc_ref[j, 1] = sl[:, HALF:]

    is_fwd = lax.broadcasted_iota(jnp.int32, (batch, HALF), 1) < STATE
    ar = [jnp.broadcast_to(aq_ref[j, 0:1, :], (batch, HALF)) for j in groups]
    ai = [jnp.broadcast_to(aq_ref[j, 1:2, :], (batch, HALF)) for j in groups]

    zero = jnp.zeros((batch, HALF), F32)
    last = n_chunks * batch
    for j in groups:
        for h in range(2):
            pf_ref[j, h, 0:batch, :] = zero
            pb_ref[j, h, last:last + batch, :] = zero

    def scan_step(k, carry):
        rf = k * batch
        rb = (n_chunks - 1 - k) * batch
        ro = (k + 1) * batch
        new = []
        for j in groups:
            sr, si = carry[j]
            xr = jnp.where(is_fwd, sloc_ref[j, 0, pl.ds(rf, batch), :],
                           sloc_ref[j, 0, pl.ds(rb, batch), :])
            xi = jnp.where(is_fwd, sloc_ref[j, 1, pl.ds(rf, batch), :],
                           sloc_ref[j, 1, pl.ds(rb, batch), :])
            nr = ar[j] * sr - ai[j] * si + xr
            ni = ar[j] * si + ai[j] * sr + xi
            pf_ref[j, 0, pl.ds(ro, batch), :] = nr
            pf_ref[j, 1, pl.ds(ro, batch), :] = ni
            pb_ref[j, 0, pl.ds(rb, batch), :] = nr
            pb_ref[j, 1, pl.ds(rb, batch), :] = ni
            new.append((nr, ni))
        return tuple(new)

    carry = tuple((zero, zero) for _ in groups)
    for k in range(n_chunks):
        carry = scan_step(k, carry)

    fwd_rows = lax.broadcasted_iota(jnp.int32, (n, HALF), 1) < STATE
    for j in groups:
        prev = jnp.concatenate(
            [jnp.where(fwd_rows, pf_ref[j, h, 0:n, :], pb_ref[j, h, batch:batch + n, :])
             for h in range(2)], axis=1)
        y = jnp.dot(mt_ref[j], uts[j], preferred_element_type=F32)
        y = y + lax.dot_general(woutt_ref[j], prev.astype(BF16), NT_DIMS, preferred_element_type=F32)
        y_ref[:, j * SSM_GROUP:(j + 1) * SSM_GROUP, :] = y.reshape(CHUNK, SSM_GROUP, n)


def _s5_t(ut, mt, wst, woutt, aq, *, n_chunks, batch):
    n = batch * n_chunks
    gps = GROUPS_PER_STEP
    kern = functools.partial(_s5_kernel, n_chunks=n_chunks, batch=batch)
    op_spec = pl.BlockSpec((gps, CW, CW), lambda g: (g, 0, 0))
    act_spec = pl.BlockSpec((CHUNK, gps * SSM_GROUP, n), lambda g: (0, g, 0))
    state_rows = pltpu.VMEM((gps, 2, n + batch, HALF), F32)
    return pl.pallas_call(
        kern,
        grid=(N_GROUPS // gps,),
        in_specs=[act_spec, op_spec, op_spec, op_spec,
                  pl.BlockSpec((gps, 2, HALF), lambda g: (g, 0, 0))],
        out_specs=act_spec,
        out_shape=jax.ShapeDtypeStruct((CHUNK, D_MODEL, n), F32),
        scratch_shapes=[pltpu.VMEM((gps, 2, n, HALF), F32), state_rows, state_rows],
        compiler_params=_params(1),
        name="s5_t",
    )(ut, mt, wst, woutt, aq)


def _prep_kernel(par_ref, win_ref, w1_ref, w2_ref, w3_ref, w4_ref,
                 mt_ref, wst_ref, woutt_ref, aq_ref, winbf_ref, wut_ref, w1bf_ref, w2bf_ref, w3bf_ref,
                 w4bf_ref):
    for g in range(OPS_GROUPS_PER_STEP):
        _s5_ops_group(g, par_ref, mt_ref, wst_ref, woutt_ref, aq_ref)

    winbf_ref[...] = win_ref[...].astype(BF16)

    @pl.when(pl.program_id(0) == COL_U)
    def _():
        wut_ref[...] = win_ref[...].T.astype(BF16)

    for src, dst in ((w1_ref, w1bf_ref), (w2_ref, w2bf_ref), (w3_ref, w3bf_ref), (w4_ref, w4bf_ref)):
        dst[...] = src[...].astype(BF16)


def _s5_ops_group(g, par_ref, mt_ref, wst_ref, woutt_ref, aq_ref):
    def rows(k):
        return par_ref[g, k * SSM_GROUP:(k + 1) * SSM_GROUP, :]

    lr, li = par_ref[g, 0:1, :], par_ref[g, 1:2, :]
    dt = jnp.exp(par_ref[g, 2:3, :])
    c_re, c_im = rows(1), rows(2)
    bt_re, bt_im = rows(3), rows(4)
    d_rows = rows(5)

    mag = jnp.exp(lr * dt)
    a_re, a_im = mag * jnp.cos(li * dt), mag * jnp.sin(li * dt)
    pw = [(jnp.ones_like(a_re), jnp.zeros_like(a_im))]
    for _ in range(CHUNK):
        p_re, p_im = pw[-1]
        pw.append((p_re * a_re - p_im * a_im, p_re * a_im + p_im * a_re))
    fwd_row = lax.broadcasted_iota(jnp.int32, (1, HALF), 1) < STATE

    def power(e_fwd, e_bwd):
        return (jnp.where(fwd_row, pw[e_fwd][0], pw[e_bwd][0]),
                jnp.where(fwd_row, pw[e_fwd][1], pw[e_bwd][1]))

    den = lr * lr + li * li
    nr, ni = a_re - 1.0, a_im
    coef_re = (nr * lr + ni * li) / den
    coef_im = (ni * lr - nr * li) / den
    bb_re = coef_re * bt_re - coef_im * bt_im
    bb_im = coef_re * bt_im + coef_im * bt_re

    st_re, st_im = [], []
    for k in range(CHUNK):
        p_re, p_im = power(CHUNK - 1 - k, k)
        st_re.append(p_re * bb_re - p_im * bb_im)
        st_im.append(p_re * bb_im + p_im * bb_re)
    wst = jnp.concatenate([jnp.concatenate(st_re, axis=0), jnp.concatenate(st_im, axis=0)], axis=1)
    wst_ref[g] = wst.astype(BF16)

    out_rows = []
    for k in range(CHUNK):
        p_re, p_im = power(k + 1, CHUNK - k)
        out_rows.append(jnp.concatenate([c_re * p_re - c_im * p_im, -(c_re * p_im + c_im * p_re)],
                                        axis=1))
    woutt_ref[g] = jnp.concatenate(out_rows, axis=0).astype(BF16)

    aq_ref[g, 0:1, :], aq_ref[g, 1:2, :] = pw[CHUNK]

    fwd1 = lax.broadcasted_iota(jnp.int32, (SSM_GROUP, HALF), 1) < STATE
    zero = jnp.zeros_like(c_re)
    cmat = jnp.concatenate([
        jnp.concatenate([jnp.where(fwd1, c_re, zero), -jnp.where(fwd1, c_im, zero)], axis=1),
        jnp.concatenate([jnp.where(fwd1, zero, c_re), -jnp.where(fwd1, zero, c_im)], axis=1)], axis=0)
    kslab = lax.dot_general(cmat, wst, NT_DIMS, precision=lax.Precision.HIGHEST,
                            preferred_element_type=F32)
    k_f = kslab[:SSM_GROUP]
    diag = (lax.broadcasted_iota(jnp.int32, (SSM_GROUP, HALF), 0)
            == lax.broadcasted_iota(jnp.int32, (SSM_GROUP, HALF), 1))
    dd = jnp.concatenate([jnp.where(diag, d_rows, 0.0), jnp.zeros((SSM_GROUP, CW - HALF), F32)], axis=1)
    k_b = kslab[SSM_GROUP:] + dd

    lane = lax.broadcasted_iota(jnp.int32, (SSM_GROUP, CW), 1)
    m_rows = []
    for t in range(CHUNK):
        left = SSM_GROUP * (CHUNK - 1 - t)
        right = SSM_GROUP * t
        a = k_f if left == 0 else jnp.where(lane < CW - left, pltpu.roll(k_f, CW - left, 1), 0.0)
        b = k_b if right == 0 else jnp.where(lane >= right, pltpu.roll(k_b, right, 1), 0.0)
        m_rows.append(a + b)
    mt_ref[g] = jnp.concatenate(m_rows, axis=0).astype(BF16)


def _prep(lam_re, lam_im, log_dt, b_re, b_im, c_re, c_im, ssm_d, w_in, w_glu, w_a, w_b, w_o):
    def lanes(x):
        return jnp.concatenate([x[0], x[1]], axis=-1)

    head = jnp.stack([lanes(lam_re), lanes(lam_im),
                      lanes(jnp.broadcast_to(log_dt[..., None], lam_re.shape))], axis=1)
    head = jnp.pad(head, ((0, 0), (0, SSM_GROUP - 3), (0, 0)))
    d_rows = jnp.broadcast_to(ssm_d.reshape(N_GROUPS, SSM_GROUP, 1), (N_GROUPS, SSM_GROUP, HALF))
    par = jnp.concatenate([head, lanes(c_re), lanes(c_im), lanes(b_re.transpose(0, 1, 3, 2)),
                           lanes(b_im.transpose(0, 1, 3, 2)), d_rows], axis=1)

    gps = OPS_GROUPS_PER_STEP
    n_steps = N_GROUPS // gps
    assert n_steps == w_in.shape[1] // D_MODEL
    op_spec = pl.BlockSpec((gps, CW, CW), lambda g: (g, 0, 0))
    op_shape = jax.ShapeDtypeStruct((N_GROUPS, CW, CW), BF16)
    sq = (D_MODEL, D_MODEL)
    sq_shape = jax.ShapeDtypeStruct(sq, BF16)
    col_spec = pl.BlockSpec(sq, lambda g: (0, g))
    row_spec = pl.BlockSpec((D_MODEL // n_steps, D_MODEL), lambda g: (g, 0))
    return pl.pallas_call(
        _prep_kernel,
        grid=(n_steps,),
        in_specs=[pl.BlockSpec((gps, 6 * SSM_GROUP, HALF), lambda g: (g, 0, 0)),
                  col_spec, row_spec, row_spec, row_spec, row_spec],
        out_specs=[op_spec, op_spec, op_spec, pl.BlockSpec((gps, 2, HALF), lambda g: (g, 0, 0)),
                   col_spec, pl.BlockSpec(sq, lambda g: (0, 0)), row_spec, row_spec, row_spec, row_spec],
        out_shape=[op_shape, op_shape, op_shape, jax.ShapeDtypeStruct((N_GROUPS, 2, HALF), F32),
                   jax.ShapeDtypeStruct(w_in.shape, BF16), sq_shape, sq_shape, sq_shape, sq_shape,
                   sq_shape],
        compiler_params=_params(1),
        name="prep",
    )(par, w_in, w_glu, w_a, w_b, w_o)


def _block_kernel(x_hbm, yt_ref, cv_ref, cvprev_ref, cvnext_ref, haloprev_ref, halonext_ref, ng_ref,
                  pza_ref, pb_ref, pzb_ref, pga_ref, pgb_ref, wglu_ref, wa_ref, wb_ref, wo_ref,
                  cw_ref, cb_ref, fg_ref, o_hbm, xbuf, xsem, obuf, osem, *, set_rows, batch):
    blk = pl.program_id(0)
    tp = pl.program_id(1)
    step = pl.program_id(0) * pl.num_programs(1) + tp
    n_steps = pl.num_programs(0) * pl.num_programs(1)
    _fetch_sets(x_hbm, xbuf, xsem, step, n_steps, SETS_PER_STEP)
    slot = step % 2

    @pl.when(step >= 2)
    def _():
        for c in _set_copies(o_hbm, obuf, osem, step - 2, slot, SETS_PER_STEP, to_hbm=True):
            c.wait()

    cvprev = cvprev_ref[...].astype(F32)
    cvnext = cvnext_ref[...].astype(F32)
    halo_prev = jnp.where(blk == 0, 0.0, haloprev_ref[...].astype(F32)[batch:])
    halo_next = jnp.where(blk == pl.num_programs(0) - 1, 0.0, halonext_ref[...].astype(F32)[:batch])
    cvprev = jnp.where(tp == 0,
                       jnp.concatenate([halo_prev, cvprev[:set_rows - batch]], axis=0), cvprev)
    cvnext = jnp.where(tp == CHUNK // SETS_PER_STEP - 1,
                       jnp.concatenate([cvnext[batch:], halo_next], axis=0), cvnext)
    def cv_set(k):
        if k < 0:
            return cvprev
        if k == SETS_PER_STEP:
            return cvnext
        return cv_ref[:, k * D_MODEL:(k + 1) * D_MODEL].astype(F32)

    cw = cw_ref[...]

    for sub in range(SETS_PER_STEP // SUB_SETS):
        ks = range(sub * SUB_SETS, (sub + 1) * SUB_SETS)
        x = jnp.concatenate([xbuf[slot, k].reshape(set_rows, D_MODEL) for k in ks], axis=0)
        xn = _rmsnorm(x, ng_ref[...]).astype(BF16)

        def proj(w_ref, xn=xn):
            return jnp.dot(xn, w_ref[...], preferred_element_type=F32)

        cvm, cvc, cvp = (jnp.concatenate([cv_set(k + d) for k in ks], axis=0) for d in (-1, 0, 1))
        conv = cb_ref[...] + cvm * cw[0:1] + cvc * cw[1:2] + cvp * cw[2:3]

        z_b = proj(pzb_ref)
        y_b = (proj(pb_ref) * conv * (z_b * _sigmoid(z_b))).astype(BF16)
        merged = _sigmoid(proj(pgb_ref)) * jnp.dot(y_b, wb_ref[...], preferred_element_type=F32)

        yg = _gelu_tanh(jnp.concatenate([yt_ref[k].T for k in ks], axis=0))
        yg = yg * _sigmoid(jnp.dot(yg.astype(BF16), wglu_ref[...], preferred_element_type=F32))
        z_a = proj(pza_ref)
        y_a = (yg * (z_a * _sigmoid(z_a))).astype(BF16)
        merged = merged + _sigmoid(proj(pga_ref)) * jnp.dot(y_a, wa_ref[...],
                                                             preferred_element_type=F32)

        h = x + jnp.dot(merged.astype(BF16), wo_ref[...], preferred_element_type=F32)
        out = _rmsnorm(h, fg_ref[...])
        for i, k in enumerate(ks):
            obuf[slot, k] = out[i * set_rows:(i + 1) * set_rows].reshape(CHUNK_BLOCK, batch, D_MODEL)

    for c in _set_copies(o_hbm, obuf, osem, step, slot, SETS_PER_STEP, to_hbm=True):
        c.start()

    @pl.when(step == n_steps - 1)
    def _():
        @pl.when(step >= 1)
        def _():
            for c in _set_copies(o_hbm, obuf, osem, step - 1, 1 - slot, SETS_PER_STEP, to_hbm=True):
                c.wait()
        for c in _set_copies(o_hbm, obuf, osem, step, slot, SETS_PER_STEP, to_hbm=True):
            c.wait()


def _block_t(x4, yt, cv3, norm_g, w_in, w_glu, w_a, w_b, w_o, cw, cb, final_g, *, bsz, n_chunks):
    s = SETS_PER_STEP
    n_blocks = n_chunks // CHUNK_BLOCK
    set_rows = CHUNK_BLOCK * bsz
    halo = 2 * bsz
    tok_spec = pl.BlockSpec((None, set_rows, s * D_MODEL), lambda cb, tp: (cb, 0, tp))
    prev_spec = pl.BlockSpec((None, set_rows, D_MODEL),
                             lambda cb, tp: (cb, 0, (tp * s + CHUNK - 1) % CHUNK))
    next_spec = pl.BlockSpec((None, set_rows, D_MODEL), lambda cb, tp: (cb, 0, (tp * s + s) % CHUNK))
    haloprev_spec = pl.BlockSpec((None, halo, D_MODEL),
                                 lambda cb, tp: (jnp.maximum(cb - 1, 0), set_rows // halo - 1, CHUNK - 1))
    halonext_spec = pl.BlockSpec((None, halo, D_MODEL),
                                 lambda cb, tp: (jnp.minimum(cb + 1, n_blocks - 1), 0, 0))
    kern = functools.partial(_block_kernel, set_rows=set_rows, batch=bsz)
    return pl.pallas_call(
        kern,
        grid=(n_blocks, CHUNK // s),
        in_specs=[
            pl.BlockSpec(memory_space=pl.ANY),
            pl.BlockSpec((s, D_MODEL, set_rows), lambda cb, tp: (tp, 0, cb)),
            tok_spec, prev_spec, next_spec, haloprev_spec, halonext_spec,
            _const_spec((1, D_MODEL)),
            _col_spec(COL_ZA), _col_spec(COL_B), _col_spec(COL_ZB), _col_spec(COL_GA), _col_spec(COL_GB),
            _const_spec((D_MODEL, D_MODEL)),
            _const_spec((D_MODEL, D_MODEL)),
            _const_spec((D_MODEL, D_MODEL)),
            _const_spec((D_MODEL, D_MODEL)),
            _const_spec((3, D_MODEL)),
            _const_spec((1, D_MODEL)),
            _const_spec((1, D_MODEL)),
        ],
        out_specs=pl.BlockSpec(memory_space=pl.ANY),
        out_shape=jax.ShapeDtypeStruct(x4.shape, F32),
        scratch_shapes=_set_buffers(bsz, s) + _set_buffers(bsz, s),
        compiler_params=_params(2),
        name="block_t",
    )(x4, yt, cv3, cv3, cv3, cv3, cv3, norm_g, w_in, w_in, w_in, w_in, w_in, w_glu, w_a, w_b, w_o, cw, cb,
      final_g)


def kernel(x, norm_g, w_in, lam_re, lam_im, log_dt, ssm_b_re, ssm_b_im, ssm_c_re, ssm_c_im, ssm_d,
           w_glu, conv_w, conv_b, w_branch_a, w_branch_b, w_out, final_g):
    bsz, seq_len, d = x.shape
    assert norm_g.shape[0] == 1 and d == D_MODEL and bsz == SUBLANES
    n_chunks = seq_len // CHUNK
    assert seq_len == n_chunks * CHUNK and n_chunks % CHUNK_BLOCK == 0
    assert (CHUNK_BLOCK * bsz) % LANES == 0

    x4 = x.reshape(bsz, n_chunks, CHUNK, d)
    ng = norm_g[0].reshape(1, d)
    mt, wst, woutt, aq, w_bf, wut, wglu_bf, wa_bf, wb_bf, wo_bf = _prep(
        lam_re[0], lam_im[0], log_dt[0], ssm_b_re[0], ssm_b_im[0], ssm_c_re[0], ssm_c_im[0],
        ssm_d[0], w_in[0], w_glu[0], w_branch_a[0], w_branch_b[0], w_out[0])

    ut, cv3 = _in_proj_t(x4, ng, wut, w_bf, bsz=bsz, n_chunks=n_chunks)
    yt = _s5_t(ut, mt, wst, woutt, aq, n_chunks=n_chunks, batch=bsz)

    out4 = _block_t(
        x4, yt, cv3, ng, w_bf, wglu_bf, wa_bf, wb_bf, wo_bf, conv_w[0], conv_b[0].reshape(1, d),
        final_g.reshape(1, d), bsz=bsz, n_chunks=n_chunks)
    return out4.reshape(bsz, seq_len, d)
```

```python
import functools
import math

import jax
import jax.numpy as jnp
from jax import lax
from jax.experimental import pallas as pl
from jax.experimental.pallas import tpu as pltpu

F32 = jnp.float32
BF16 = jnp.bfloat16

D_MODEL = 1024
SSM_GROUP = 16
N_GROUPS = D_MODEL // SSM_GROUP
STATE = 64
CHUNK = 16
CW = CHUNK * SSM_GROUP
CHUNK_BLOCK = 64
SW = 4 * STATE
HALF = 2 * STATE
LANES = 128
EPS = 1e-6
SUBLANES = 8
IN_SETS_PER_STEP = 2
SETS_PER_STEP = 2
SUB_SETS = 1
GROUPS_PER_STEP = 4
OPS_GROUPS_PER_STEP = 8
VMEM_LIMIT_BYTES = 60 * 1024 * 1024

NT_DIMS = (((1,), (1,)), ((), ()))
TN_DIMS = (((0,), (0,)), ((), ()))


def _rmsnorm(x, g):
    ms = jnp.mean(x * x, axis=-1, keepdims=True)
    return x * lax.rsqrt(ms + EPS) * g


def _sigmoid(x):
    return 0.5 * jnp.tanh(0.5 * x) + 0.5


def _gelu_tanh(x):
    c = math.sqrt(2.0 / math.pi)
    return x * (0.5 * (1.0 + jnp.tanh(c * (x + 0.044715 * (x * x * x)))))


def _params(n_axes):
    return pltpu.CompilerParams(
        dimension_semantics=("arbitrary",) * n_axes, vmem_limit_bytes=VMEM_LIMIT_BYTES)


def _const_spec(shape):
    return pl.BlockSpec(shape, lambda *_: (0,) * len(shape), pipeline_mode=pl.Buffered(1))


COL_U, COL_ZA, COL_V, COL_B, COL_C, COL_ZB, COL_GA, COL_GB = range(8)


def _col_spec(col):
    return pl.BlockSpec((D_MODEL, D_MODEL), lambda *_: (0, col), pipeline_mode=pl.Buffered(1))


def _set_copies(x_hbm, buf, sem, step, slot, sets, *, to_hbm=False):
    steps_per_block = CHUNK // sets
    c0 = (step // steps_per_block) * CHUNK_BLOCK
    tl0 = (step % steps_per_block) * sets
    copies = []
    for k in range(sets):
        for b in range(x_hbm.shape[0]):
            hbm = x_hbm.at[b, pl.ds(c0, CHUNK_BLOCK), tl0 + k, :]
            vmem = buf.at[slot, k, :, b, :]
            src, dst = (vmem, hbm) if to_hbm else (hbm, vmem)
            copies.append(pltpu.make_async_copy(src, dst, sem.at[slot, k]))
    return copies


def _fetch_sets(x_hbm, xbuf, sem, step, n_steps, sets):
    slot = step % 2

    @pl.when(step == 0)
    def _():
        for c in _set_copies(x_hbm, xbuf, sem, step, slot, sets):
            c.start()

    @pl.when(step + 1 < n_steps)
    def _():
        for c in _set_copies(x_hbm, xbuf, sem, step + 1, 1 - slot, sets):
            c.start()

    for c in _set_copies(x_hbm, xbuf, sem, step, slot, sets):
        c.wait()
    rows = xbuf.shape[2] * xbuf.shape[3]
    return jnp.concatenate([xbuf[slot, k].reshape(rows, D_MODEL) for k in range(sets)], axis=0)


def _set_buffers(batch, sets):
    return [pltpu.VMEM((2, sets, CHUNK_BLOCK, batch, D_MODEL), F32),
            pltpu.SemaphoreType.DMA((2, sets))]


def _in_proj_kernel(x_hbm, g_ref, wut_ref, wv_ref, wc_ref, u_ref, cv_ref, xbuf, xsem, *, set_rows):
    step = pl.program_id(0) * pl.num_programs(1) + pl.program_id(1)
    n_steps = pl.num_programs(0) * pl.num_programs(1)
    x = _fetch_sets(x_hbm, xbuf, xsem, step, n_steps, IN_SETS_PER_STEP)
    xn = _rmsnorm(x, g_ref[...]).astype(BF16)
    ut = lax.dot_general(wut_ref[...], xn, NT_DIMS, preferred_element_type=F32)
    cv = (jnp.dot(xn, wv_ref[...], preferred_element_type=F32)
          * jnp.dot(xn, wc_ref[...], preferred_element_type=F32)).astype(BF16)
    for k in range(IN_SETS_PER_STEP):
        u_ref[k] = ut[:, k * set_rows:(k + 1) * set_rows].astype(BF16)
        cv_ref[:, k * D_MODEL:(k + 1) * D_MODEL] = cv[k * set_rows:(k + 1) * set_rows]


def _in_proj_t(x4, norm_g, wut, w_in, *, bsz, n_chunks):
    s = IN_SETS_PER_STEP
    n_blocks = n_chunks // CHUNK_BLOCK
    set_rows = CHUNK_BLOCK * bsz
    tok_spec = pl.BlockSpec((None, set_rows, s * D_MODEL), lambda cb, tp: (cb, 0, tp))
    return pl.pallas_call(
        functools.partial(_in_proj_kernel, set_rows=set_rows),
        grid=(n_blocks, CHUNK // s),
        in_specs=[pl.BlockSpec(memory_space=pl.ANY), _const_spec((1, D_MODEL)),
                  _const_spec((D_MODEL, D_MODEL)), _col_spec(COL_V), _col_spec(COL_C)],
        out_specs=[pl.BlockSpec((s, D_MODEL, set_rows), lambda cb, tp: (tp, 0, cb)), tok_spec],
        out_shape=[jax.ShapeDtypeStruct((CHUNK, D_MODEL, bsz * n_chunks), BF16),
                   jax.ShapeDtypeStruct((n_blocks, set_rows, CHUNK * D_MODEL), BF16)],
        scratch_shapes=_set_buffers(bsz, s),
        compiler_params=_params(2),
        name="in_proj_t",
    )(x4, norm_g, wut, w_in, w_in)


def _s5_kernel(u_ref, mt_ref, wst_ref, woutt_ref, aq_ref, y_ref, sloc_ref, pf_ref, pb_ref,
               *, n_chunks, batch):
    n = batch * n_chunks
    groups = range(GROUPS_PER_STEP)
    uts = [u_ref[:, j * SSM_GROUP:(j + 1) * SSM_GROUP, :].reshape(CW, n) for j in groups]

    for j in groups:
        sl = lax.dot_general(uts[j], wst_ref[j], TN_DIMS, preferred_element_type=F32)
        sloc_ref[j, 0] = sl[:, :HALF]
        sloc_ref[j, 1] = sl[:, HALF:]

    is_fwd = lax.broadcasted_iota(jnp.int32, (batch, HALF), 1) < STATE
    ar = [jnp.broadcast_to(aq_ref[j, 0:1, :], (batch, HALF)) for j in groups]
    ai = [jnp.broadcast_to(aq_ref[j, 1:2, :], (batch, HALF)) for j in groups]

    zero = jnp.zeros((batch, HALF), F32)
    last = n_chunks * batch
    for j in groups:
        for h in range(2):
            pf_ref[j, h, 0:batch, :] = zero
            pb_ref[j, h, last:last + batch, :] = zero

    def scan_step(k, carry):
        rf = k * batch
        rb = (n_chunks - 1 - k) * batch
        ro = (k + 1) * batch
        new = []
        for j in groups:
            sr, si = carry[j]
            xr = jnp.where(is_fwd, sloc_ref[j, 0, pl.ds(rf, batch), :],
                           sloc_ref[j, 0, pl.ds(rb, batch), :])
            xi = jnp.where(is_fwd, sloc_ref[j, 1, pl.ds(rf, batch), :],
                           sloc_ref[j, 1, pl.ds(rb, batch), :])
            nr = ar[j] * sr - ai[j] * si + xr
            ni = ar[j] * si + ai[j] * sr + xi
            pf_ref[j, 0, pl.ds(ro, batch), :] = nr
            pf_ref[j, 1, pl.ds(ro, batch), :] = ni
            pb_ref[j, 0, pl.ds(rb, batch), :] = nr
            pb_ref[j, 1, pl.ds(rb, batch), :] = ni
            new.append((nr, ni))
        return tuple(new)

    carry = tuple((zero, zero) for _ in groups)
    for k in range(n_chunks):
        carry = scan_step(k, carry)

    fwd_rows = lax.broadcasted_iota(jnp.int32, (n, HALF), 1) < STATE
    for j in groups:
        prev = jnp.concatenate(
            [jnp.where(fwd_rows, pf_ref[j, h, 0:n, :], pb_ref[j, h, batch:batch + n, :])
             for h in range(2)], axis=1)
        y = jnp.dot(mt_ref[j], uts[j], preferred_element_type=F32)
        y = y + lax.dot_general(woutt_ref[j], prev.astype(BF16), NT_DIMS, preferred_element_type=F32)
        y_ref[:, j * SSM_GROUP:(j + 1) * SSM_GROUP, :] = y.reshape(CHUNK, SSM_GROUP, n)


def _s5_t(ut, mt, wst, woutt, aq, *, n_chunks, batch):
    n = batch * n_chunks
    gps = GROUPS_PER_STEP
    kern = functools.partial(_s5_kernel, n_chunks=n_chunks, batch=batch)
    op_spec = pl.BlockSpec((gps, CW, CW), lambda g: (g, 0, 0))
    act_spec = pl.BlockSpec((CHUNK, gps * SSM_GROUP, n), lambda g: (0, g, 0))
    state_rows = pltpu.VMEM((gps, 2, n + batch, HALF), F32)
    return pl.pallas_call(
        kern,
        grid=(N_GROUPS // gps,),
        in_specs=[act_spec, op_spec, op_spec, op_spec,
                  pl.BlockSpec((gps, 2, HALF), lambda g: (g, 0, 0))],
        out_specs=act_spec,
        out_shape=jax.ShapeDtypeStruct((CHUNK, D_MODEL, n), F32),
        scratch_shapes=[pltpu.VMEM((gps, 2, n, HALF), F32), state_rows, state_rows],
        compiler_params=_params(1),
        name="s5_t",
    )(ut, mt, wst, woutt, aq)


def _prep_kernel(par_ref, win_ref, w1_ref, w2_ref, w3_ref, w4_ref,
                 mt_ref, wst_ref, woutt_ref, aq_ref, winbf_ref, wut_ref, w1bf_ref, w2bf_ref, w3bf_ref,
                 w4bf_ref):
    for g in range(OPS_GROUPS_PER_STEP):
        _s5_ops_group(g, par_ref, mt_ref, wst_ref, woutt_ref, aq_ref)

    winbf_ref[...] = win_ref[...].astype(BF16)

    @pl.when(pl.program_id(0) == COL_U)
    def _():
        wut_ref[...] = win_ref[...].T.astype(BF16)

    for src, dst in ((w1_ref, w1bf_ref), (w2_ref, w2bf_ref), (w3_ref, w3bf_ref), (w4_ref, w4bf_ref)):
        dst[...] = src[...].astype(BF16)


def _s5_ops_group(g, par_ref, mt_ref, wst_ref, woutt_ref, aq_ref):
    def rows(k):
        return par_ref[g, k * SSM_GROUP:(k + 1) * SSM_GROUP, :]

    lr, li = par_ref[g, 0:1, :], par_ref[g, 1:2, :]
    dt = jnp.exp(par_ref[g, 2:3, :])
    c_re, c_im = rows(1), rows(2)
    bt_re, bt_im = rows(3), rows(4)
    d_rows = rows(5)

    mag = jnp.exp(lr * dt)
    a_re, a_im = mag * jnp.cos(li * dt), mag * jnp.sin(li * dt)
    pw = [(jnp.ones_like(a_re), jnp.zeros_like(a_im))]
    for _ in range(CHUNK):
        p_re, p_im = pw[-1]
        pw.append((p_re * a_re - p_im * a_im, p_re * a_im + p_im * a_re))
    fwd_row = lax.broadcasted_iota(jnp.int32, (1, HALF), 1) < STATE

    def power(e_fwd, e_bwd):
        return (jnp.where(fwd_row, pw[e_fwd][0], pw[e_bwd][0]),
                jnp.where(fwd_row, pw[e_fwd][1], pw[e_bwd][1]))

    den = lr * lr + li * li
    nr, ni = a_re - 1.0, a_im
    coef_re = (nr * lr + ni * li) / den
    coef_im = (ni * lr - nr * li) / den
    bb_re = coef_re * bt_re - coef_im * bt_im
    bb_im = coef_re * bt_im + coef_im * bt_re

    st_re, st_im = [], []
    for k in range(CHUNK):
        p_re, p_im = power(CHUNK - 1 - k, k)
        st_re.append(p_re * bb_re - p_im * bb_im)
        st_im.append(p_re * bb_im + p_im * bb_re)
    wst = jnp.concatenate([jnp.concatenate(st_re, axis=0), jnp.concatenate(st_im, axis=0)], axis=1)
    wst_ref[g] = wst.astype(BF16)

    out_rows = []
    for k in range(CHUNK):
        p_re, p_im = power(k + 1, CHUNK - k)
        out_rows.append(jnp.concatenate([c_re * p_re - c_im * p_im, -(c_re * p_im + c_im * p_re)],
                                        axis=1))
    woutt_ref[g] = jnp.concatenate(out_rows, axis=0).astype(BF16)

    aq_ref[g, 0:1, :], aq_ref[g, 1:2, :] = pw[CHUNK]

    fwd1 = lax.broadcasted_iota(jnp.int32, (SSM_GROUP, HALF), 1) < STATE
    zero = jnp.zeros_like(c_re)
    cmat = jnp.concatenate([
        jnp.concatenate([jnp.where(fwd1, c_re, zero), -jnp.where(fwd1, c_im, zero)], axis=1),
        jnp.concatenate([jnp.where(fwd1, zero, c_re), -jnp.where(fwd1, zero, c_im)], axis=1)], axis=0)
    kslab = lax.dot_general(cmat, wst, NT_DIMS, precision=lax.Precision.HIGHEST,
                            preferred_element_type=F32)
    k_f = kslab[:SSM_GROUP]
    diag = (lax.broadcasted_iota(jnp.int32, (SSM_GROUP, HALF), 0)
            == lax.broadcasted_iota(jnp.int32, (SSM_GROUP, HALF), 1))
    dd = jnp.concatenate([jnp.where(diag, d_rows, 0.0), jnp.zeros((SSM_GROUP, CW - HALF), F32)], axis=1)
    k_b = kslab[SSM_GROUP:] + dd

    lane = lax.broadcasted_iota(jnp.int32, (SSM_GROUP, CW), 1)
    m_rows = []
    for t in range(CHUNK):
        left = SSM_GROUP * (CHUNK - 1 - t)
        right = SSM_GROUP * t
        a = k_f if left == 0 else jnp.where(lane < CW - left, pltpu.roll(k_f, CW - left, 1), 0.0)
        b = k_b if right == 0 else jnp.where(lane >= right, pltpu.roll(k_b, right, 1), 0.0)
        m_rows.append(a + b)
    mt_ref[g] = jnp.concatenate(m_rows, axis=0).astype(BF16)


def _prep(lam_re, lam_im, log_dt, b_re, b_im, c_re, c_im, ssm_d, w_in, w_glu, w_a, w_b, w_o):
    def lanes(x):
        return jnp.concatenate([x[0], x[1]], axis=-1)

    head = jnp.stack([lanes(lam_re), lanes(lam_im),
                      lanes(jnp.broadcast_to(log_dt[..., None], lam_re.shape))], axis=1)
    head = jnp.pad(head, ((0, 0), (0, SSM_GROUP - 3), (0, 0)))
    d_rows = jnp.broadcast_to(ssm_d.reshape(N_GROUPS, SSM_GROUP, 1), (N_GROUPS, SSM_GROUP, HALF))
    par = jnp.concatenate([head, lanes(c_re), lanes(c_im), lanes(b_re.transpose(0, 1, 3, 2)),
                           lanes(b_im.transpose(0, 1, 3, 2)), d_rows], axis=1)

    gps = OPS_GROUPS_PER_STEP
    n_steps = N_GROUPS // gps
    assert n_steps == w_in.shape[1] // D_MODEL
    op_spec = pl.BlockSpec((gps, CW, CW), lambda g: (g, 0, 0))
    op_shape = jax.ShapeDtypeStruct((N_GROUPS, CW, CW), BF16)
    sq = (D_MODEL, D_MODEL)
    sq_shape = jax.ShapeDtypeStruct(sq, BF16)
    col_spec = pl.BlockSpec(sq, lambda g: (0, g))
    row_spec = pl.BlockSpec((D_MODEL // n_steps, D_MODEL), lambda g: (g, 0))
    return pl.pallas_call(
        _prep_kernel,
        grid=(n_steps,),
        in_specs=[pl.BlockSpec((gps, 6 * SSM_GROUP, HALF), lambda g: (g, 0, 0)),
                  col_spec, row_spec, row_spec, row_spec, row_spec],
        out_specs=[op_spec, op_spec, op_spec, pl.BlockSpec((gps, 2, HALF), lambda g: (g, 0, 0)),
                   col_spec, pl.BlockSpec(sq, lambda g: (0, 0)), row_spec, row_spec, row_spec, row_spec],
        out_shape=[op_shape, op_shape, op_shape, jax.ShapeDtypeStruct((N_GROUPS, 2, HALF), F32),
                   jax.ShapeDtypeStruct(w_in.shape, BF16), sq_shape, sq_shape, sq_shape, sq_shape,
                   sq_shape],
        compiler_params=_params(1),
        name="prep",
    )(par, w_in, w_glu, w_a, w_b, w_o)


def _block_kernel(x_hbm, yt_ref, cv_ref, cvprev_ref, cvnext_ref, haloprev_ref, halonext_ref, ng_ref,
                  pza_ref, pb_ref, pzb_ref, pga_ref, pgb_ref, wglu_ref, wa_ref, wb_ref, wo_ref,
                  cw_ref, cb_ref, fg_ref, o_hbm, xbuf, xsem, obuf, osem, *, set_rows, batch):
    blk = pl.program_id(0)
    tp = pl.program_id(1)
    step = pl.program_id(0) * pl.num_programs(1) + tp
    n_steps = pl.num_programs(0) * pl.num_programs(1)
    _fetch_sets(x_hbm, xbuf, xsem, step, n_steps, SETS_PER_STEP)
    slot = step % 2

    @pl.when(step >= 2)
    def _():
        for c in _set_copies(o_hbm, obuf, osem, step - 2, slot, SETS_PER_STEP, to_hbm=True):
            c.wait()

    cvprev = cvprev_ref[...].astype(F32)
    cvnext = cvnext_ref[...].astype(F32)
    halo_prev = jnp.where(blk == 0, 0.0, haloprev_ref[...].astype(F32)[batch:])
    halo_next = jnp.where(blk == pl.num_programs(0) - 1, 0.0, halonext_ref[...].astype(F32)[:batch])
    cvprev = jnp.where(tp == 0,
                       jnp.concatenate([halo_prev, cvprev[:set_rows - batch]], axis=0), cvprev)
    cvnext = jnp.where(tp == CHUNK // SETS_PER_STEP - 1,
                       jnp.concatenate([cvnext[batch:], halo_next], axis=0), cvnext)
    def cv_set(k):
        if k < 0:
            return cvprev
        if k == SETS_PER_STEP:
            return cvnext
        return cv_ref[:, k * D_MODEL:(k + 1) * D_MODEL].astype(F32)

    cw = cw_ref[...]

    for sub in range(SETS_PER_STEP // SUB_SETS):
        ks = range(sub * SUB_SETS, (sub + 1) * SUB_SETS)
        x = jnp.concatenate([xbuf[slot, k].reshape(set_rows, D_MODEL) for k in ks], axis=0)
        xn = _rmsnorm(x, ng_ref[...]).astype(BF16)

        def proj(w_ref, xn=xn):
            return jnp.dot(xn, w_ref[...], preferred_element_type=F32)

        cvm, cvc, cvp = (jnp.concatenate([cv_set(k + d) for k in ks], axis=0) for d in (-1, 0, 1))
        conv = cb_ref[...] + cvm * cw[0:1] + cvc * cw[1:2] + cvp * cw[2:3]

        z_b = proj(pzb_ref)
        y_b = (proj(pb_ref) * conv * (z_b * _sigmoid(z_b))).astype(BF16)
        merged = _sigmoid(proj(pgb_ref)) * jnp.dot(y_b, wb_ref[...], preferred_element_type=F32)

        yg = _gelu_tanh(jnp.concatenate([yt_ref[k].T for k in ks], axis=0))
        yg = yg * _sigmoid(jnp.dot(yg.astype(BF16), wglu_ref[...], preferred_element_type=F32))
        z_a = proj(pza_ref)
        y_a = (yg * (z_a * _sigmoid(z_a))).astype(BF16)
        merged = merged + _sigmoid(proj(pga_ref)) * jnp.dot(y_a, wa_ref[...],
                                                             preferred_element_type=F32)

        h = x + jnp.dot(merged.astype(BF16), wo_ref[...], preferred_element_type=F32)
        out = _rmsnorm(h, fg_ref[...])
        for i, k in enumerate(ks):
            obuf[slot, k] = out[i * set_rows:(i + 1) * set_rows].reshape(CHUNK_BLOCK, batch, D_MODEL)

    for c in _set_copies(o_hbm, obuf, osem, step, slot, SETS_PER_STEP, to_hbm=True):
        c.start()

    @pl.when(step == n_steps - 1)
    def _():
        @pl.when(step >= 1)
        def _():
            for c in _set_copies(o_hbm, obuf, osem, step - 1, 1 - slot, SETS_PER_STEP, to_hbm=True):
                c.wait()
        for c in _set_copies(o_hbm, obuf, osem, step, slot, SETS_PER_STEP, to_hbm=True):
            c.wait()


def _block_t(x4, yt, cv3, norm_g, w_in, w_glu, w_a, w_b, w_o, cw, cb, final_g, *, bsz, n_chunks):
    s = SETS_PER_STEP
    n_blocks = n_chunks // CHUNK_BLOCK
    set_rows = CHUNK_BLOCK * bsz
    halo = 2 * bsz
    tok_spec = pl.BlockSpec((None, set_rows, s * D_MODEL), lambda cb, tp: (cb, 0, tp))
    prev_spec = pl.BlockSpec((None, set_rows, D_MODEL),
                             lambda cb, tp: (cb, 0, (tp * s + CHUNK - 1) % CHUNK))
    next_spec = pl.BlockSpec((None, set_rows, D_MODEL), lambda cb, tp: (cb, 0, (tp * s + s) % CHUNK))
    haloprev_spec = pl.BlockSpec((None, halo, D_MODEL),
                                 lambda cb, tp: (jnp.maximum(cb - 1, 0), set_rows // halo - 1, CHUNK - 1))
    halonext_spec = pl.BlockSpec((None, halo, D_MODEL),
                                 lambda cb, tp: (jnp.minimum(cb + 1, n_blocks - 1), 0, 0))
    kern = functools.partial(_block_kernel, set_rows=set_rows, batch=bsz)
    return pl.pallas_call(
        kern,
        grid=(n_blocks, CHUNK // s),
        in_specs=[
            pl.BlockSpec(memory_space=pl.ANY),
            pl.BlockSpec((s, D_MODEL, set_rows), lambda cb, tp: (tp, 0, cb)),
            tok_spec, prev_spec, next_spec, haloprev_spec, halonext_spec,
            _const_spec((1, D_MODEL)),
            _col_spec(COL_ZA), _col_spec(COL_B), _col_spec(COL_ZB), _col_spec(COL_GA), _col_spec(COL_GB),
            _const_spec((D_MODEL, D_MODEL)),
            _const_spec((D_MODEL, D_MODEL)),
            _const_spec((D_MODEL, D_MODEL)),
            _const_spec((D_MODEL, D_MODEL)),
            _const_spec((3, D_MODEL)),
            _const_spec((1, D_MODEL)),
            _const_spec((1, D_MODEL)),
        ],
        out_specs=pl.BlockSpec(memory_space=pl.ANY),
        out_shape=jax.ShapeDtypeStruct(x4.shape, F32),
        scratch_shapes=_set_buffers(bsz, s) + _set_buffers(bsz, s),
        compiler_params=_params(2),
        name="block_t",
    )(x4, yt, cv3, cv3, cv3, cv3, cv3, norm_g, w_in, w_in, w_in, w_in, w_in, w_glu, w_a, w_b, w_o, cw, cb,
      final_g)


def kernel(x, norm_g, w_in, lam_re, lam_im, log_dt, ssm_b_re, ssm_b_im, ssm_c_re, ssm_c_im, ssm_d,
           w_glu, conv_w, conv_b, w_branch_a, w_branch_b, w_out, final_g):
    bsz, seq_len, d = x.shape
    assert norm_g.shape[0] == 1 and d == D_MODEL and bsz == SUBLANES
    n_chunks = seq_len // CHUNK
    assert seq_len == n_chunks * CHUNK and n_chunks % CHUNK_BLOCK == 0
    assert (CHUNK_BLOCK * bsz) % LANES == 0

    x4 = x.reshape(bsz, n_chunks, CHUNK, d)
    ng = norm_g[0].reshape(1, d)
    mt, wst, woutt, aq, w_bf, wut, wglu_bf, wa_bf, wb_bf, wo_bf = _prep(
        lam_re[0], lam_im[0], log_dt[0], ssm_b_re[0], ssm_b_im[0], ssm_c_re[0], ssm_c_im[0],
        ssm_d[0], w_in[0], w_glu[0], w_branch_a[0], w_branch_b[0], w_out[0])

    ut, cv3 = _in_proj_t(x4, ng, wut, w_bf, bsz=bsz, n_chunks=n_chunks)
    yt = _s5_t(ut, mt, wst, woutt, aq, n_chunks=n_chunks, batch=bsz)

    out4 = _block_t(
        x4, yt, cv3, ng, w_bf, wglu_bf, wa_bf, wb_bf, wo_bf, conv_w[0], conv_b[0].reshape(1, d),
        final_g.reshape(1, d), bsz=bsz, n_chunks=n_chunks)
    return out4.reshape(bsz, seq_len, d)
```

```python
import functools
import math

import jax
import jax.numpy as jnp
from jax import lax
from jax.experimental import pallas as pl
from jax.experimental.pallas import tpu as pltpu

F32 = jnp.float32
BF16 = jnp.bfloat16

D_MODEL = 1024
SSM_GROUP = 16
N_GROUPS = D_MODEL // SSM_GROUP
STATE = 64
CHUNK = 16
CW = CHUNK * SSM_GROUP
CHUNK_BLOCK = 64
SW = 4 * STATE
HALF = 2 * STATE
LANES = 128
EPS = 1e-6
SUBLANES = 8
IN_SETS_PER_STEP = 2
SETS_PER_STEP = 2
SUB_SETS = 1
GROUPS_PER_STEP = 4
OPS_GROUPS_PER_STEP = 8
VMEM_LIMIT_BYTES = 60 * 1024 * 1024

NT_DIMS = (((1,), (1,)), ((), ()))
TN_DIMS = (((0,), (0,)), ((), ()))


def _rmsnorm(x, g):
    ms = jnp.mean(x * x, axis=-1, keepdims=True)
    return x * lax.rsqrt(ms + EPS) * g


def _sigmoid(x):
    return 0.5 * jnp.tanh(0.5 * x) + 0.5


def _gelu_tanh(x):
    c = math.sqrt(2.0 / math.pi)
    return x * (0.5 * (1.0 + jnp.tanh(c * (x + 0.044715 * (x * x * x)))))


def _params(n_axes):
    return pltpu.CompilerParams(
        dimension_semantics=("arbitrary",) * n_axes, vmem_limit_bytes=VMEM_LIMIT_BYTES)


def _const_spec(shape):
    return pl.BlockSpec(shape, lambda *_: (0,) * len(shape), pipeline_mode=pl.Buffered(1))


COL_U, COL_ZA, COL_V, COL_B, COL_C, COL_ZB, COL_GA, COL_GB = range(8)
UVC_COLS = (COL_U, COL_V, COL_C)


def _col_spec(col):
    return pl.BlockSpec((D_MODEL, D_MODEL), lambda *_: (0, col), pipeline_mode=pl.Buffered(1))


def _set_copies(x_hbm, buf, sem, step, slot, sets, *, to_hbm=False):
    steps_per_block = CHUNK // sets
    c0 = (step // steps_per_block) * CHUNK_BLOCK
    tl0 = (step % steps_per_block) * sets
    copies = []
    for k in range(sets):
        for b in range(x_hbm.shape[0]):
            hbm = x_hbm.at[b, pl.ds(c0, CHUNK_BLOCK), tl0 + k, :]
            vmem = buf.at[slot, k, :, b, :]
            src, dst = (vmem, hbm) if to_hbm else (hbm, vmem)
            copies.append(pltpu.make_async_copy(src, dst, sem.at[slot, k]))
    return copies


def _fetch_sets(x_hbm, xbuf, sem, step, n_steps, sets):
    slot = step % 2

    @pl.when(step == 0)
    def _():
        for c in _set_copies(x_hbm, xbuf, sem, step, slot, sets):
            c.start()

    @pl.when(step + 1 < n_steps)
    def _():
        for c in _set_copies(x_hbm, xbuf, sem, step + 1, 1 - slot, sets):
            c.start()

    for c in _set_copies(x_hbm, xbuf, sem, step, slot, sets):
        c.wait()
    rows = xbuf.shape[2] * xbuf.shape[3]
    return jnp.concatenate([xbuf[slot, k].reshape(rows, D_MODEL) for k in range(sets)], axis=0)


def _set_buffers(batch, sets):
    return [pltpu.VMEM((2, sets, CHUNK_BLOCK, batch, D_MODEL), F32),
            pltpu.SemaphoreType.DMA((2, sets))]


def _in_proj_kernel(x_hbm, g_ref, wut_ref, wv_ref, wc_ref, win_ref, w1_ref, w2_ref, w3_ref, w4_ref,
                    u_ref, cv_ref, winbf_ref, w1bf_ref, w2bf_ref, w3bf_ref, w4bf_ref, xbuf, xsem,
                    *, set_rows):
    step = pl.program_id(0) * pl.num_programs(1) + pl.program_id(1)
    n_steps = pl.num_programs(0) * pl.num_programs(1)
    x = _fetch_sets(x_hbm, xbuf, xsem, step, n_steps, IN_SETS_PER_STEP)
    xn = _rmsnorm(x, g_ref[...]).astype(BF16)
    ut = lax.dot_general(wut_ref[...], xn, NT_DIMS, preferred_element_type=F32)
    cv = (jnp.dot(xn, wv_ref[...], preferred_element_type=F32)
          * jnp.dot(xn, wc_ref[...], preferred_element_type=F32)).astype(BF16)
    for k in range(IN_SETS_PER_STEP):
        u_ref[k] = ut[:, k * set_rows:(k + 1) * set_rows].astype(BF16)
        cv_ref[:, k * D_MODEL:(k + 1) * D_MODEL] = cv[k * set_rows:(k + 1) * set_rows]
    for src, dst in ((win_ref, winbf_ref), (w1_ref, w1bf_ref), (w2_ref, w2bf_ref), (w3_ref, w3bf_ref),
                     (w4_ref, w4bf_ref)):
        dst[...] = src[...].astype(BF16)


def _in_proj_t(x4, norm_g, wut, wuvc, w_in, w_glu, w_a, w_b, w_o, *, bsz, n_chunks):
    s = IN_SETS_PER_STEP
    n_blocks = n_chunks // CHUNK_BLOCK
    set_rows = CHUNK_BLOCK * bsz
    steps_per_block = CHUNK // s
    slab = D_MODEL // (n_blocks * steps_per_block)
    tok_spec = pl.BlockSpec((None, set_rows, s * D_MODEL), lambda cb, tp: (cb, 0, tp))

    def slab_spec(cols):
        return pl.BlockSpec((slab, cols), lambda cb, tp: (cb * steps_per_block + tp, 0))

    slab_specs = [slab_spec(w_in.shape[1])] + [slab_spec(D_MODEL)] * 4
    sq_shape = jax.ShapeDtypeStruct((D_MODEL, D_MODEL), BF16)
    return pl.pallas_call(
        functools.partial(_in_proj_kernel, set_rows=set_rows),
        grid=(n_blocks, steps_per_block),
        in_specs=[pl.BlockSpec(memory_space=pl.ANY), _const_spec((1, D_MODEL)),
                  _const_spec((D_MODEL, D_MODEL)), _col_spec(UVC_COLS.index(COL_V)),
                  _col_spec(UVC_COLS.index(COL_C))] + slab_specs,
        out_specs=[pl.BlockSpec((s, D_MODEL, set_rows), lambda cb, tp: (tp, 0, cb)), tok_spec]
        + slab_specs,
        out_shape=[jax.ShapeDtypeStruct((CHUNK, D_MODEL, bsz * n_chunks), BF16),
                   jax.ShapeDtypeStruct((n_blocks, set_rows, CHUNK * D_MODEL), BF16),
                   jax.ShapeDtypeStruct(w_in.shape, BF16), sq_shape, sq_shape, sq_shape, sq_shape],
        scratch_shapes=_set_buffers(bsz, s),
        compiler_params=_params(2),
        name="in_proj_t",
    )(x4, norm_g, wut, wuvc, wuvc, w_in, w_glu, w_a, w_b, w_o)


def _s5_kernel(u_ref, mt_ref, wst_ref, woutt_ref, aq_ref, y_ref, sloc_ref, pf_ref, pb_ref,
               *, n_chunks, batch):
    n = batch * n_chunks
    groups = range(GROUPS_PER_STEP)
    uts = [u_ref[:, j * SSM_GROUP:(j + 1) * SSM_GROUP, :].reshape(CW, n) for j in groups]

    for j in groups:
        sl = lax.dot_general(uts[j], wst_ref[j], TN_DIMS, preferred_element_type=F32)
        sloc_ref[j, 0] = sl[:, :HALF]
        sloc_ref[j, 1] = sl[:, HALF:]

    is_fwd = lax.broadcasted_iota(jnp.int32, (batch, HALF), 1) < STATE
    ar = [jnp.broadcast_to(aq_ref[j, 0:1, :], (batch, HALF)) for j in groups]
    ai = [jnp.broadcast_to(aq_ref[j, 1:2, :], (batch, HALF)) for j in groups]

    zero = jnp.zeros((batch, HALF), F32)
    last = n_chunks * batch
    for j in groups:
        for h in range(2):
            pf_ref[j, h, 0:batch, :] = zero
            pb_ref[j, h, last:last + batch, :] = zero

    def scan_step(k, carry):
        rf = k * batch
        rb = (n_chunks - 1 - k) * batch
        ro = (k + 1) * batch
        new = []
        for j in groups:
            sr, si = carry[j]
            xr = jnp.where(is_fwd, sloc_ref[j, 0, pl.ds(rf, batch), :],
                           sloc_ref[j, 0, pl.ds(rb, batch), :])
            xi = jnp.where(is_fwd, sloc_ref[j, 1, pl.ds(rf, batch), :],
                           sloc_ref[j, 1, pl.ds(rb, batch), :])
            nr = ar[j] * sr - ai[j] * si + xr
            ni = ar[j] * si + ai[j] * sr + xi
            pf_ref[j, 0, pl.ds(ro, batch), :] = nr
            pf_ref[j, 1, pl.ds(ro, batch), :] = ni
            pb_ref[j, 0, pl.ds(rb, batch), :] = nr
            pb_ref[j, 1, pl.ds(rb, batch), :] = ni
            new.append((nr, ni))
        return tuple(new)

    carry = tuple((zero, zero) for _ in groups)
    for k in range(n_chunks):
        carry = scan_step(k, carry)

    fwd_rows = lax.broadcasted_iota(jnp.int32, (n, HALF), 1) < STATE
    for j in groups:
        prev = jnp.concatenate(
            [jnp.where(fwd_rows, pf_ref[j, h, 0:n, :], pb_ref[j, h, batch:batch + n, :])
             for h in range(2)], axis=1)
        y = jnp.dot(mt_ref[j], uts[j], preferred_element_type=F32)
        y = y + lax.dot_general(woutt_ref[j], prev.astype(BF16), NT_DIMS, preferred_element_type=F32)
        y_ref[:, j * SSM_GROUP:(j + 1) * SSM_GROUP, :] = y.reshape(CHUNK, SSM_GROUP, n)


def _s5_t(ut, mt, wst, woutt, aq, *, n_chunks, batch):
    n = batch * n_chunks
    gps = GROUPS_PER_STEP
    kern = functools.partial(_s5_kernel, n_chunks=n_chunks, batch=batch)
    op_spec = pl.BlockSpec((gps, CW, CW), lambda g: (g, 0, 0))
    act_spec = pl.BlockSpec((CHUNK, gps * SSM_GROUP, n), lambda g: (0, g, 0))
    state_rows = pltpu.VMEM((gps, 2, n + batch, HALF), F32)
    return pl.pallas_call(
        kern,
        grid=(N_GROUPS // gps,),
        in_specs=[act_spec, op_spec, op_spec, op_spec,
                  pl.BlockSpec((gps, 2, HALF), lambda g: (g, 0, 0))],
        out_specs=act_spec,
        out_shape=jax.ShapeDtypeStruct((CHUNK, D_MODEL, n), F32),
        scratch_shapes=[pltpu.VMEM((gps, 2, n, HALF), F32), state_rows, state_rows],
        compiler_params=_params(1),
        name="s5_t",
    )(ut, mt, wst, woutt, aq)


def _prep_kernel(par_ref, win_ref, mt_ref, wst_ref, woutt_ref, aq_ref, wuvc_ref, wut_ref):
    for g in range(OPS_GROUPS_PER_STEP):
        _s5_ops_group(g, par_ref, mt_ref, wst_ref, woutt_ref, aq_ref)

    step = pl.program_id(0)

    @pl.when(step < len(UVC_COLS))
    def _():
        wuvc_ref[...] = win_ref[...].astype(BF16)

    @pl.when(step == 0)
    def _():
        wut_ref[...] = win_ref[...].T.astype(BF16)


def _s5_ops_group(g, par_ref, mt_ref, wst_ref, woutt_ref, aq_ref):
    def rows(k):
        return par_ref[g, k * SSM_GROUP:(k + 1) * SSM_GROUP, :]

    lr, li = par_ref[g, 0:1, :], par_ref[g, 1:2, :]
    dt = jnp.exp(par_ref[g, 2:3, :])
    c_re, c_im = rows(1), rows(2)
    bt_re, bt_im = rows(3), rows(4)
    d_rows = rows(5)

    mag = jnp.exp(lr * dt)
    a_re, a_im = mag * jnp.cos(li * dt), mag * jnp.sin(li * dt)
    pw = [(jnp.ones_like(a_re), jnp.zeros_like(a_im))]
    for _ in range(CHUNK):
        p_re, p_im = pw[-1]
        pw.append((p_re * a_re - p_im * a_im, p_re * a_im + p_im * a_re))
    fwd_row = lax.broadcasted_iota(jnp.int32, (1, HALF), 1) < STATE

    def power(e_fwd, e_bwd):
        return (jnp.where(fwd_row, pw[e_fwd][0], pw[e_bwd][0]),
                jnp.where(fwd_row, pw[e_fwd][1], pw[e_bwd][1]))

    den = lr * lr + li * li
    nr, ni = a_re - 1.0, a_im
    coef_re = (nr * lr + ni * li) / den
    coef_im = (ni * lr - nr * li) / den
    bb_re = coef_re * bt_re - coef_im * bt_im
    bb_im = coef_re * bt_im + coef_im * bt_re

    st_re, st_im = [], []
    for k in range(CHUNK):
        p_re, p_im = power(CHUNK - 1 - k, k)
        st_re.append(p_re * bb_re - p_im * bb_im)
        st_im.append(p_re * bb_im + p_im * bb_re)
    wst = jnp.concatenate([jnp.concatenate(st_re, axis=0), jnp.concatenate(st_im, axis=0)], axis=1)
    wst_ref[g] = wst.astype(BF16)

    out_rows = []
    for k in range(CHUNK):
        p_re, p_im = power(k + 1, CHUNK - k)
        out_rows.append(jnp.concatenate([c_re * p_re - c_im * p_im, -(c_re * p_im + c_im * p_re)],
                                        axis=1))
    woutt_ref[g] = jnp.concatenate(out_rows, axis=0).astype(BF16)

    aq_ref[g, 0:1, :], aq_ref[g, 1:2, :] = pw[CHUNK]

    fwd1 = lax.broadcasted_iota(jnp.int32, (SSM_GROUP, HALF), 1) < STATE
    zero = jnp.zeros_like(c_re)
    cmat = jnp.concatenate([
        jnp.concatenate([jnp.where(fwd1, c_re, zero), -jnp.where(fwd1, c_im, zero)], axis=1),
        jnp.concatenate([jnp.where(fwd1, zero, c_re), -jnp.where(fwd1, zero, c_im)], axis=1)], axis=0)
    kslab = lax.dot_general(cmat, wst, NT_DIMS, precision=lax.Precision.HIGHEST,
                            preferred_element_type=F32)
    k_f = kslab[:SSM_GROUP]
    diag = (lax.broadcasted_iota(jnp.int32, (SSM_GROUP, HALF), 0)
            == lax.broadcasted_iota(jnp.int32, (SSM_GROUP, HALF), 1))
    dd = jnp.concatenate([jnp.where(diag, d_rows, 0.0), jnp.zeros((SSM_GROUP, CW - HALF), F32)], axis=1)
    k_b = kslab[SSM_GROUP:] + dd

    lane = lax.broadcasted_iota(jnp.int32, (SSM_GROUP, CW), 1)
    m_rows = []
    for t in range(CHUNK):
        left = SSM_GROUP * (CHUNK - 1 - t)
        right = SSM_GROUP * t
        a = k_f if left == 0 else jnp.where(lane < CW - left, pltpu.roll(k_f, CW - left, 1), 0.0)
        b = k_b if right == 0 else jnp.where(lane >= right, pltpu.roll(k_b, right, 1), 0.0)
        m_rows.append(a + b)
    mt_ref[g] = jnp.concatenate(m_rows, axis=0).astype(BF16)


def _prep(lam_re, lam_im, log_dt, b_re, b_im, c_re, c_im, ssm_d, w_in):
    def lanes(x):
        return jnp.concatenate([x[0], x[1]], axis=-1)

    head = jnp.stack([lanes(lam_re), lanes(lam_im),
                      lanes(jnp.broadcast_to(log_dt[..., None], lam_re.shape))], axis=1)
    head = jnp.pad(head, ((0, 0), (0, SSM_GROUP - 3), (0, 0)))
    d_rows = jnp.broadcast_to(ssm_d.reshape(N_GROUPS, SSM_GROUP, 1), (N_GROUPS, SSM_GROUP, HALF))
    par = jnp.concatenate([head, lanes(c_re), lanes(c_im), lanes(b_re.transpose(0, 1, 3, 2)),
                           lanes(b_im.transpose(0, 1, 3, 2)), d_rows], axis=1)

    gps = OPS_GROUPS_PER_STEP
    n_steps = N_GROUPS // gps
    assert n_steps >= len(UVC_COLS) and UVC_COLS == (COL_U, COL_V, COL_C) == (0, 2, 4)
    op_spec = pl.BlockSpec((gps, CW, CW), lambda g: (g, 0, 0))
    op_shape = jax.ShapeDtypeStruct((N_GROUPS, CW, CW), BF16)
    sq = (D_MODEL, D_MODEL)
    last = len(UVC_COLS) - 1
    return pl.pallas_call(
        _prep_kernel,
        grid=(n_steps,),
        in_specs=[pl.BlockSpec((gps, 6 * SSM_GROUP, HALF), lambda g: (g, 0, 0)),
                  pl.BlockSpec(sq, lambda g: (0, 2 * jnp.minimum(g, last)))],
        out_specs=[op_spec, op_spec, op_spec, pl.BlockSpec((gps, 2, HALF), lambda g: (g, 0, 0)),
                   pl.BlockSpec(sq, lambda g: (0, jnp.minimum(g, last))),
                   pl.BlockSpec(sq, lambda g: (0, 0))],
        out_shape=[op_shape, op_shape, op_shape, jax.ShapeDtypeStruct((N_GROUPS, 2, HALF), F32),
                   jax.ShapeDtypeStruct((D_MODEL, len(UVC_COLS) * D_MODEL), BF16),
                   jax.ShapeDtypeStruct(sq, BF16)],
        compiler_params=_params(1),
        name="prep",
    )(par, w_in)


def _block_kernel(x_hbm, yt_ref, cv_ref, cvprev_ref, cvnext_ref, haloprev_ref, halonext_ref, ng_ref,
                  pza_ref, pb_ref, pzb_ref, pga_ref, pgb_ref, wglu_ref, wa_ref, wb_ref, wo_ref,
                  cw_ref, cb_ref, fg_ref, o_hbm, xbuf, xsem, obuf, osem, *, set_rows, batch):
    blk = pl.program_id(0)
    tp = pl.program_id(1)
    step = pl.program_id(0) * pl.num_programs(1) + tp
    n_steps = pl.num_programs(0) * pl.num_programs(1)
    _fetch_sets(x_hbm, xbuf, xsem, step, n_steps, SETS_PER_STEP)
    slot = step % 2

    @pl.when(step >= 2)
    def _():
        for c in _set_copies(o_hbm, obuf, osem, step - 2, slot, SETS_PER_STEP, to_hbm=True):
            c.wait()

    cvprev = cvprev_ref[...].astype(F32)
    cvnext = cvnext_ref[...].astype(F32)
    halo_prev = jnp.where(blk == 0, 0.0, haloprev_ref[...].astype(F32)[batch:])
    halo_next = jnp.where(blk == pl.num_programs(0) - 1, 0.0, halonext_ref[...].astype(F32)[:batch])
    cvprev = jnp.where(tp == 0,
                       jnp.concatenate([halo_prev, cvprev[:set_rows - batch]], axis=0), cvprev)
    cvnext = jnp.where(tp == CHUNK // SETS_PER_STEP - 1,
                       jnp.concatenate([cvnext[batch:], halo_next], axis=0), cvnext)
    def cv_set(k):
        if k < 0:
            return cvprev
        if k == SETS_PER_STEP:
            return cvnext
        return cv_ref[:, k * D_MODEL:(k + 1) * D_MODEL].astype(F32)

    cw = cw_ref[...]

    for sub in range(SETS_PER_STEP // SUB_SETS):
        ks = range(sub * SUB_SETS, (sub + 1) * SUB_SETS)
        x = jnp.concatenate([xbuf[slot, k].reshape(set_rows, D_MODEL) for k in ks], axis=0)
        xn = _rmsnorm(x, ng_ref[...]).astype(BF16)

        def proj(w_ref, xn=xn):
            return jnp.dot(xn, w_ref[...], preferred_element_type=F32)

        cvm, cvc, cvp = (jnp.concatenate([cv_set(k + d) for k in ks], axis=0) for d in (-1, 0, 1))
        conv = cb_ref[...] + cvm * cw[0:1] + cvc * cw[1:2] + cvp * cw[2:3]

        z_b = proj(pzb_ref)
        y_b = (proj(pb_ref) * conv * (z_b * _sigmoid(z_b))).astype(BF16)
        merged = _sigmoid(proj(pgb_ref)) * jnp.dot(y_b, wb_ref[...], preferred_element_type=F32)

        yg = _gelu_tanh(jnp.concatenate([yt_ref[k].T for k in ks], axis=0))
        yg = yg * _sigmoid(jnp.dot(yg.astype(BF16), wglu_ref[...], preferred_element_type=F32))
        z_a = proj(pza_ref)
        y_a = (yg * (z_a * _sigmoid(z_a))).astype(BF16)
        merged = merged + _sigmoid(proj(pga_ref)) * jnp.dot(y_a, wa_ref[...],
                                                             preferred_element_type=F32)

        h = x + jnp.dot(merged.astype(BF16), wo_ref[...], preferred_element_type=F32)
        out = _rmsnorm(h, fg_ref[...])
        for i, k in enumerate(ks):
            obuf[slot, k] = out[i * set_rows:(i + 1) * set_rows].reshape(CHUNK_BLOCK, batch, D_MODEL)

    for c in _set_copies(o_hbm, obuf, osem, step, slot, SETS_PER_STEP, to_hbm=True):
        c.start()

    @pl.when(step == n_steps - 1)
    def _():
        @pl.when(step >= 1)
        def _():
            for c in _set_copies(o_hbm, obuf, osem, step - 1, 1 - slot, SETS_PER_STEP, to_hbm=True):
                c.wait()
        for c in _set_copies(o_hbm, obuf, osem, step, slot, SETS_PER_STEP, to_hbm=True):
            c.wait()


def _block_t(x4, yt, cv3, norm_g, w_in, w_glu, w_a, w_b, w_o, cw, cb, final_g, *, bsz, n_chunks):
    s = SETS_PER_STEP
    n_blocks = n_chunks // CHUNK_BLOCK
    set_rows = CHUNK_BLOCK * bsz
    halo = 2 * bsz
    tok_spec = pl.BlockSpec((None, set_rows, s * D_MODEL), lambda cb, tp: (cb, 0, tp))
    prev_spec = pl.BlockSpec((None, set_rows, D_MODEL),
                             lambda cb, tp: (cb, 0, (tp * s + CHUNK - 1) % CHUNK))
    next_spec = pl.BlockSpec((None, set_rows, D_MODEL), lambda cb, tp: (cb, 0, (tp * s + s) % CHUNK))
    haloprev_spec = pl.BlockSpec((None, halo, D_MODEL),
                                 lambda cb, tp: (jnp.maximum(cb - 1, 0), set_rows // halo - 1, CHUNK - 1))
    halonext_spec = pl.BlockSpec((None, halo, D_MODEL),
                                 lambda cb, tp: (jnp.minimum(cb + 1, n_blocks - 1), 0, 0))
    kern = functools.partial(_block_kernel, set_rows=set_rows, batch=bsz)
    return pl.pallas_call(
        kern,
        grid=(n_blocks, CHUNK // s),
        in_specs=[
            pl.BlockSpec(memory_space=pl.ANY),
            pl.BlockSpec((s, D_MODEL, set_rows), lambda cb, tp: (tp, 0, cb)),
            tok_spec, prev_spec, next_spec, haloprev_spec, halonext_spec,
            _const_spec((1, D_MODEL)),
            _col_spec(COL_ZA), _col_spec(COL_B), _col_spec(COL_ZB), _col_spec(COL_GA), _col_spec(COL_GB),
            _const_spec((D_MODEL, D_MODEL)),
            _const_spec((D_MODEL, D_MODEL)),
            _const_spec((D_MODEL, D_MODEL)),
            _const_spec((D_MODEL, D_MODEL)),
            _const_spec((3, D_MODEL)),
            _const_spec((1, D_MODEL)),
            _const_spec((1, D_MODEL)),
        ],
        out_specs=pl.BlockSpec(memory_space=pl.ANY),
        out_shape=jax.ShapeDtypeStruct(x4.shape, F32),
        scratch_shapes=_set_buffers(bsz, s) + _set_buffers(bsz, s),
        compiler_params=_params(2),
        name="block_t",
    )(x4, yt, cv3, cv3, cv3, cv3, cv3, norm_g, w_in, w_in, w_in, w_in, w_in, w_glu, w_a, w_b, w_o, cw, cb,
      final_g)


def kernel(x, norm_g, w_in, lam_re, lam_im, log_dt, ssm_b_re, ssm_b_im, ssm_c_re, ssm_c_im, ssm_d,
           w_glu, conv_w, conv_b, w_branch_a, w_branch_b, w_out, final_g):
    bsz, seq_len, d = x.shape
    assert norm_g.shape[0] == 1 and d == D_MODEL and bsz == SUBLANES
    n_chunks = seq_len // CHUNK
    assert seq_len == n_chunks * CHUNK and n_chunks % CHUNK_BLOCK == 0
    assert (CHUNK_BLOCK * bsz) % LANES == 0

    x4 = x.reshape(bsz, n_chunks, CHUNK, d)
    ng = norm_g[0].reshape(1, d)
    mt, wst, woutt, aq, wuvc, wut = _prep(
        lam_re[0], lam_im[0], log_dt[0], ssm_b_re[0], ssm_b_im[0], ssm_c_re[0], ssm_c_im[0],
        ssm_d[0], w_in[0])

    ut, cv3, w_bf, wglu_bf, wa_bf, wb_bf, wo_bf = _in_proj_t(
        x4, ng, wut, wuvc, w_in[0], w_glu[0], w_branch_a[0], w_branch_b[0], w_out[0],
        bsz=bsz, n_chunks=n_chunks)
    yt = _s5_t(ut, mt, wst, woutt, aq, n_chunks=n_chunks, batch=bsz)

    out4 = _block_t(
        x4, yt, cv3, ng, w_bf, wglu_bf, wa_bf, wb_bf, wo_bf, conv_w[0], conv_b[0].reshape(1, d),
        final_g.reshape(1, d), bsz=bsz, n_chunks=n_chunks)
    return out4.reshape(bsz, seq_len, d)
```
